```python
import jax
import jax.numpy as jnp
from jax import lax
import numpy as np

D_MODEL = 1024
BATCH = 8
SEQ = 8192
DEPTH = 2

HEAD_DIM = 64
HEADS_PER_MIXER = 4
N_MIXERS = 4
N_HEADS = N_MIXERS * HEADS_PER_MIXER
MIX_WIDTH = HEADS_PER_MIXER * HEAD_DIM
Q_BLOCK = 128
IDX_HEADS = 8
IDX_DIM = 64
TOPK_KEYS_MAX = 256
DILATED_PAIRS = ((128, 1), (512, 4), (2048, 16))
MOBA_BLOCK = 256
MOBA_TOPK = 3
MOBA_QBLOCK = 64
N_EXPERTS = 16
N_GROUPS = 4
TOPK_GROUPS = 1
TOPK_EXPERTS = 2
D_EXPERT = 512
N_ADA = 6
DN_ALPHA = (2 * DEPTH) ** 0.25
DN_BETA = (8 * DEPTH) ** -0.25
LN_EPS = 1e-5
IN_SPLITS = (N_HEADS * HEAD_DIM, N_HEADS * HEAD_DIM, N_HEADS * HEAD_DIM,
             IDX_HEADS * IDX_DIM, IDX_DIM, IDX_HEADS, HEADS_PER_MIXER)
IN_COLS = sum(IN_SPLITS)

kernel_name = 'hybrid_gated_sparse_mixers_moe'


def _layer_norm(x, g, b):
    xf = x.astype(jnp.float32)
    mu = jnp.mean(xf, axis=-1, keepdims=True)
    var = jnp.mean(jnp.square(xf - mu), axis=-1, keepdims=True)
    y = (xf - mu) * lax.rsqrt(var + LN_EPS) * g.astype(jnp.float32) + b.astype(jnp.float32)
    return y.astype(x.dtype)


def _alibi_slopes():
    n = 3 * HEADS_PER_MIXER
    s = 2.0 ** (-8.0 * np.arange(1, n + 1) / n)
    return jnp.asarray(s.reshape(HEADS_PER_MIXER, 3).T, dtype=jnp.float32)


def _masked_softmax(logits, mask):
    return jax.nn.softmax(jnp.where(mask, logits, -jnp.inf), axis=-1)


def _sweep_query_blocks(fn, block, *arrays):
    S = arrays[0].shape[1]
    nblk = S // block
    blocks = tuple(jnp.swapaxes(a.reshape(a.shape[0], nblk, block, *a.shape[2:]), 0, 1) for a in arrays)
    starts = jnp.arange(nblk, dtype=jnp.int32) * block
    out = lax.map(lambda xs: fn(xs[0], *xs[1]), (starts, blocks))
    out = jnp.swapaxes(out, 0, 1)
    return out.reshape(out.shape[0], S, *out.shape[3:])


def _dsa_attention(q, k, v, qi, ki, wi, slopes):
    B, S, H, Dh = q.shape
    k_top = min(TOPK_KEYS_MAX, S // 4)
    key_pos = jnp.arange(S, dtype=jnp.int32)
    b_idx = jnp.arange(B)[:, None, None]
    wi = wi.astype(jnp.float32) * (IDX_HEADS ** -0.5)
    scale = Dh ** -0.5

    def block(t0, qb, qib, wib):
        t = t0 + jnp.arange(Q_BLOCK, dtype=jnp.int32)
        rel = jax.nn.relu(jnp.einsum('bthd,bsd->bths', qib, ki).astype(jnp.float32))
        score = jnp.einsum('bths,bth->bts', rel, wib) * (IDX_DIM ** -0.5)
        score = jnp.where(key_pos[None, None, :] <= t[None, :, None], score, -jnp.inf)
        _, sel = lax.top_k(score, k_top)
        valid = sel <= t[None, :, None]
        kg = k[b_idx, sel]
        vg = v[b_idx, sel]
        logits = jnp.einsum('bthd,btkhd->bthk', qb, kg).astype(jnp.float32) * scale
        dist = (t[None, :, None] - sel).astype(jnp.float32)
        logits = logits - slopes[None, None, :, None] * dist[:, :, None, :]
        p = _masked_softmax(logits, valid[:, :, None, :])
        return jnp.einsum('bthk,btkhd->bthd', p.astype(v.dtype), vg)

    return _sweep_query_blocks(block, Q_BLOCK, q, qi, wi)


def _dilated_attention(q, k, v, slopes):
    B, S, H, Dh = q.shape
    scale = Dh ** -0.5

    def block(t0, qb):
        t = t0 + jnp.arange(Q_BLOCK, dtype=jnp.int32)
        outs, lses = [], []
        for window, dil in DILATED_PAIRS:
            offs = jnp.arange(window // dil + 1, dtype=jnp.int32) * dil
            pos = t[:, None] - offs[None, :]
            idx = jnp.maximum(pos, 0)
            kg = k[:, idx]
            vg = v[:, idx]
            logits = jnp.einsum('bthd,btnhd->bthn', qb, kg).astype(jnp.float32) * scale
            logits = logits - slopes[:, None] * offs.astype(jnp.float32)[None, :]
            logits = jnp.where((pos >= 0)[None, :, None, :], logits, -jnp.inf)
            lse = jax.nn.logsumexp(logits, axis=-1)
            p = jnp.exp(logits - lse[..., None])
            outs.append(jnp.einsum('bthn,btnhd->bthd', p.astype(v.dtype), vg))
            lses.append(lse)
        w = jax.nn.softmax(jnp.stack(lses, axis=0), axis=0)
        return jnp.sum(w[..., None].astype(v.dtype) * jnp.stack(outs, axis=0), axis=0)

    return _sweep_query_blocks(block, Q_BLOCK, q)


def _moba_attention(q, k, v, slopes):
    B, S, H, Dh = q.shape
    n_kb = -(-S // MOBA_BLOCK)
    pad = n_kb * MOBA_BLOCK - S
    kp = jnp.pad(k, ((0, 0), (0, pad), (0, 0), (0, 0)))
    vp = jnp.pad(v, ((0, 0), (0, pad), (0, 0), (0, 0)))
    kbh = kp.reshape(B, n_kb, MOBA_BLOCK, H, Dh).transpose(0, 3, 1, 2, 4)
    vbh = vp.reshape(B, n_kb, MOBA_BLOCK, H, Dh).transpose(0, 3, 1, 2, 4)
    k_mean = jnp.mean(kbh.astype(jnp.float32), axis=3)
    k_top = min(MOBA_TOPK, n_kb)
    n_sel = k_top * MOBA_BLOCK
    blk_ids = jnp.arange(n_kb, dtype=jnp.int32)
    in_blk = jnp.arange(MOBA_BLOCK, dtype=jnp.int32)
    b_idx = jnp.arange(B)[:, None, None, None]
    h_idx = jnp.arange(H)[None, None, :, None]
    scale = Dh ** -0.5

    def block(t0, qb):
        t = t0 + jnp.arange(MOBA_QBLOCK, dtype=jnp.int32)
        own = t0 // MOBA_BLOCK
        gate = jnp.einsum('bthd,bhnd->bthn', qb.astype(jnp.float32), k_mean)
        gate = jnp.where(blk_ids < own, gate, -jnp.inf)
        _, sel = lax.top_k(gate, k_top)
        sel_valid = sel < own
        kg = kbh[b_idx, h_idx, sel]
        vg = vbh[b_idx, h_idx, sel]
        lg_sel = jnp.einsum('bthd,bthkjd->bthkj', qb, kg).astype(jnp.float32) * scale
        dist_sel = (t[None, :, None, None, None] - (sel[..., None] * MOBA_BLOCK + in_blk)).astype(jnp.float32)
        lg_sel = lg_sel - slopes[None, None, :, None, None] * dist_sel
        lg_sel = jnp.where(sel_valid[..., None], lg_sel, -jnp.inf).reshape(B, MOBA_QBLOCK, H, n_sel)
        start = own * MOBA_BLOCK
        k_own = lax.dynamic_slice_in_dim(kp, start, MOBA_BLOCK, axis=1)
        v_own = lax.dynamic_slice_in_dim(vp, start, MOBA_BLOCK, axis=1)
        dist_own = (t[:, None] - (start + in_blk)[None, :]).astype(jnp.float32)
        lg_own = jnp.einsum('bthd,bshd->bths', qb, k_own).astype(jnp.float32) * scale
        lg_own = lg_own - slopes[None, None, :, None] * dist_own[None, :, None, :]
        lg_own = jnp.where((dist_own >= 0)[None, :, None, :], lg_own, -jnp.inf)
        p = jax.nn.softmax(jnp.concatenate([lg_sel, lg_own], axis=-1), axis=-1).astype(v.dtype)
        o_sel = jnp.einsum('bthn,bthnd->bthd', p[..., :n_sel], vg.reshape(B, MOBA_QBLOCK, H, n_sel, Dh))
        o_own = jnp.einsum('bths,bshd->bthd', p[..., n_sel:], v_own)
        return o_sel + o_own

    return _sweep_query_blocks(block, MOBA_QBLOCK, q)


def _forgetting_attention(q, k, v, f_logit):
    B, S, H, Dh = q.shape
    cum = jnp.cumsum(jax.nn.log_sigmoid(f_logit.astype(jnp.float32)), axis=1)
    cum_keys = jnp.transpose(cum, (0, 2, 1))
    key_pos = jnp.arange(S, dtype=jnp.int32)
    scale = Dh ** -0.5

    def block(t0, qb, cum_q):
        t = t0 + jnp.arange(Q_BLOCK, dtype=jnp.int32)
        logits = jnp.einsum('bthd,bshd->bhts', qb, k).astype(jnp.float32) * scale
        logits = logits + jnp.transpose(cum_q, (0, 2, 1))[..., None] - cum_keys[:, :, None, :]
        p = _masked_softmax(logits, (key_pos[None, :] <= t[:, None])[None, None])
        return jnp.einsum('bhts,bshd->bthd', p.astype(v.dtype), v)

    return _sweep_query_blocks(block, Q_BLOCK, q, cum)


def _grouped_moe(h, w_router, b_router, w_gate, w_up, w_down):
    B, S, D = h.shape
    per_group = N_EXPERTS // N_GROUPS
    scores = jax.nn.sigmoid((h @ w_router).astype(jnp.float32))
    biased = scores + b_router.astype(jnp.float32)
    g = biased.reshape(B, S, N_GROUPS, per_group)
    group_score = jnp.sum(lax.top_k(g, 2)[0], axis=-1)
    _, top_g = lax.top_k(group_score, TOPK_GROUPS)
    group_mask = jnp.sum(jax.nn.one_hot(top_g, N_GROUPS), axis=-2)
    expert_mask = jnp.repeat(group_mask, per_group, axis=-1) > 0
    _, top_e = lax.top_k(jnp.where(expert_mask, biased, -jnp.inf), TOPK_EXPERTS)
    sel = jnp.take_along_axis(scores, top_e, axis=-1)
    weights = sel / jnp.sum(sel, axis=-1, keepdims=True)
    combine = jnp.sum(jax.nn.one_hot(top_e, N_EXPERTS) * weights[..., None], axis=-2).astype(h.dtype)
    out = None
    for e in range(N_EXPERTS):
        hid = jax.nn.silu(h @ w_gate[e]) * (h @ w_up[e])
        term = combine[..., e:e + 1] * (hid @ w_down[e])
        out = term if out is None else out + term
    return out


def setup_inputs(seed: int = 0) -> dict:
    key = jax.random.key(seed)
    ks = jax.random.split(key, 18)

    def nrm(k, shape, scale):
        return jax.random.normal(k, shape, jnp.float32) * scale

    return {
        'x': nrm(ks[0], (BATCH, SEQ, D_MODEL), 1.0),
        'c': nrm(ks[1], (BATCH, D_MODEL), 1.0),
        'w_in': nrm(ks[2], (DEPTH, D_MODEL, IN_COLS), D_MODEL ** -0.5),
        'b_forget': 2.0 + nrm(ks[3], (DEPTH, HEADS_PER_MIXER), 0.5),
        'w_branch': nrm(ks[4], (DEPTH, N_MIXERS, MIX_WIDTH, D_MODEL), MIX_WIDTH ** -0.5),
        'w_merge_gate': nrm(ks[5], (DEPTH, N_MIXERS, D_MODEL, D_MODEL), D_MODEL ** -0.5),
        'w_out': nrm(ks[6], (DEPTH, D_MODEL, D_MODEL), DN_BETA * D_MODEL ** -0.5),
        'w_ada': nrm(ks[7], (DEPTH, D_MODEL, N_ADA * D_MODEL), 0.5 * D_MODEL ** -0.5),
        'b_ada': nrm(ks[8], (DEPTH, N_ADA * D_MODEL), 0.02),
        'ln1_g': 1.0 + nrm(ks[9], (DEPTH, D_MODEL), 0.02),
        'ln1_b': nrm(ks[10], (DEPTH, D_MODEL), 0.02),
        'ln2_g': 1.0 + nrm(ks[11], (DEPTH, D_MODEL), 0.02),
        'ln2_b': nrm(ks[12], (DEPTH, D_MODEL), 0.02),
        'w_router': nrm(ks[13], (D_MODEL, N_EXPERTS), D_MODEL ** -0.5),
        'b_router': nrm(ks[14], (N_EXPERTS,), 0.01),
        'w_exp_gate': nrm(ks[15], (DEPTH, N_EXPERTS, D_MODEL, D_EXPERT), D_MODEL ** -0.5),
        'w_exp_up': nrm(ks[16], (DEPTH, N_EXPERTS, D_MODEL, D_EXPERT), D_MODEL ** -0.5),
        'w_exp_down': nrm(ks[17], (DEPTH, N_EXPERTS, D_EXPERT, D_MODEL), DN_BETA * D_EXPERT ** -0.5),
    }


def reference(x, c, w_in, b_forget, w_branch, w_merge_gate, w_out, w_ada, b_ada,
              ln1_g, ln1_b, ln2_g, ln2_b, w_router, b_router,
              w_exp_gate, w_exp_up, w_exp_down):
    B, S, _ = x.shape
    slopes = _alibi_slopes()
    split_at = [int(v) for v in np.cumsum(IN_SPLITS)[:-1]]

    def heads(a, m):
        return a[:, :, m * HEADS_PER_MIXER:(m + 1) * HEADS_PER_MIXER]

    for l in range(DEPTH):
        ada = (c @ w_ada[l] + b_ada[l])[:, None, :]
        sh1, sc1, g1, sh2, sc2, g2 = jnp.split(ada, N_ADA, axis=-1)

        h = x * (1.0 + sc1) + sh1
        proj = h @ w_in[l]
        q, k, v, qi, ki, wi, fl = jnp.split(proj, split_at, axis=-1)
        q = q.reshape(B, S, N_HEADS, HEAD_DIM)
        k = k.reshape(B, S, N_HEADS, HEAD_DIM)
        v = v.reshape(B, S, N_HEADS, HEAD_DIM)
        qi = qi.reshape(B, S, IDX_HEADS, IDX_DIM)
        o_a = _dsa_attention(heads(q, 0), heads(k, 0), heads(v, 0), qi, ki, wi, slopes[0])
        o_b = _dilated_attention(heads(q, 1), heads(k, 1), heads(v, 1), slopes[1])
        o_c = _moba_attention(heads(q, 2), heads(k, 2), heads(v, 2), slopes[2])
        o_d = _forgetting_attention(heads(q, 3), heads(k, 3), heads(v, 3), fl + b_forget[l])
        mixed = None
        for m, o in enumerate((o_a, o_b, o_c, o_d)):
            gate = jax.nn.sigmoid(h @ w_merge_gate[l, m])
            term = gate * (o.reshape(B, S, MIX_WIDTH) @ w_branch[l, m])
            mixed = term if mixed is None else mixed + term
        y = mixed @ w_out[l]
        x = _layer_norm(DN_ALPHA * x + g1 * y, ln1_g[l], ln1_b[l])

        h2 = x * (1.0 + sc2) + sh2
        y2 = _grouped_moe(h2, w_router, b_router, w_exp_gate[l], w_exp_up[l], w_exp_down[l])
        x = _layer_norm(DN_ALPHA * x + g2 * y2, ln2_g[l], ln2_b[l])
    return x
```

```python
import functools

import numpy as np
import jax
import jax.numpy as jnp
from jax import lax
from jax.experimental import pallas as pl
from jax.experimental.pallas import tpu as pltpu

D_MODEL = 1024
HEAD_DIM = 64
HEADS = 4
MIX_W = HEADS * HEAD_DIM
N_MIX = 4
IDX_HEADS = 8
IDX_DIM = 64
TOPK_KEYS = 256
MOBA_BLOCK = 256
MOBA_TOPK = 3
N_EXPERTS = 16
N_GROUPS = 4
D_EXPERT = 512
DEPTH = 2
DN_ALPHA = (2 * DEPTH) ** 0.25
LN_EPS = 1e-5
IN_COLS = 3 * N_MIX * MIX_W + IDX_HEADS * IDX_DIM + IDX_DIM + IDX_HEADS + HEADS

CH = 256
LANES = 128
MISC_W = 128
WI_ROW = IDX_DIM
FL_ROW = IDX_DIM + IDX_HEADS
LOG2E = 1.4426950408889634
NEG = -1e30
QSCALE = HEAD_DIM ** -0.5 * LOG2E
INT_MIN = -(2 ** 31)
VMEM_LIMIT = 56 * 1024 * 1024

F32 = jnp.float32
BF16 = jnp.bfloat16
HI = lax.Precision.HIGHEST
NT = (((1,), (1,)), ((), ()))


def _alibi_slopes():
    n = 3 * HEADS
    s = 2.0 ** (-8.0 * np.arange(1, n + 1) / n)
    return s.reshape(HEADS, 3).T


def _dot(a, b):
    return jnp.dot(a, b, preferred_element_type=F32)


def _dot_hi(a, b):
    return jnp.dot(a, b, preferred_element_type=F32, precision=HI)


def _params(n_axes):
    return pltpu.CompilerParams(dimension_semantics=("arbitrary",) * n_axes,
                                vmem_limit_bytes=VMEM_LIMIT)


def _layer_norm(z, g, b):
    mu = jnp.mean(z, axis=-1, keepdims=True)
    var = jnp.mean(jnp.square(z - mu), axis=-1, keepdims=True)
    return (z - mu) * lax.rsqrt(var + LN_EPS) * g + b


def _ada_kernel(c_ref, w_ref, b_ref, o_ref):
    o_ref[0] = _dot_hi(c_ref[...], w_ref[0]) + b_ref[0]


def _ada(c, w_ada, b_ada):
    depth, d, n = w_ada.shape
    bsz = c.shape[0]
    tn = D_MODEL
    return pl.pallas_call(
        _ada_kernel,
        out_shape=jax.ShapeDtypeStruct((depth, bsz, n), F32),
        grid=(depth, n // tn),
        in_specs=[pl.BlockSpec((bsz, d), lambda l, j: (0, 0)),
                  pl.BlockSpec((1, d, tn), lambda l, j: (l, 0, j)),
                  pl.BlockSpec((1, 1, tn), lambda l, j: (l, 0, j))],
        out_specs=pl.BlockSpec((1, bsz, tn), lambda l, j: (l, 0, j)),
        compiler_params=_params(2),
        name="ada",
    )(c, w_ada, b_ada.reshape(depth, 1, n))


def _proj_kernel(x_ref, ada_ref, wk_ref, wm_ref, wt_ref, wmt_ref,
                 k_ref, kmean_ref, misc_ref, qT_ref, vT_ref, qiT_ref, miscT_ref, *, tm):
    x = x_ref[0]
    sh = ada_ref[0, 0:1, :]
    sc = ada_ref[0, 1:2, :]
    h = (x * (1.0 + sc) + sh).astype(BF16)
    kf = _dot(h, wk_ref[...])
    k_ref[0] = kf.astype(BF16)
    for g in range(tm // MOBA_BLOCK):
        kmean_ref[0, 0, g:g + 1, :] = jnp.mean(kf[g * MOBA_BLOCK:(g + 1) * MOBA_BLOCK], axis=0, keepdims=True)
    misc_ref[0] = _dot(h, wm_ref[...])
    t = lax.dot_general(wt_ref[...], h, NT, preferred_element_type=F32)
    nq = N_MIX * MIX_W
    qT_ref[0] = (t[0:nq] * QSCALE).astype(BF16)
    for g in range(tm // CH):
        vT_ref[0, g] = t[nq:2 * nq, g * CH:(g + 1) * CH].astype(BF16)
    qiT_ref[0] = t[2 * nq:].astype(BF16)
    miscT_ref[0] = lax.dot_general(wmt_ref[...], h, NT, preferred_element_type=F32)


def _proj(x, ada, w_in, tm=512):
    bsz, seq, d = x.shape
    nq = N_MIX * MIX_W
    nqi = IDX_HEADS * IDX_DIM
    w = w_in.astype(BF16)
    wk = w[:, nq:2 * nq]
    wm = jnp.pad(w[:, 3 * nq + nqi:], ((0, 0), (0, MISC_W - (IN_COLS - 3 * nq - nqi))))
    wt = jnp.concatenate([w[:, 0:nq], w[:, 2 * nq:3 * nq], w[:, 3 * nq:3 * nq + nqi]], axis=1).T
    wmt = wm.T
    nt = seq // tm
    full = lambda b, i: (0, 0)
    outs = pl.pallas_call(
        functools.partial(_proj_kernel, tm=tm),
        out_shape=(jax.ShapeDtypeStruct((bsz, seq, nq), BF16),
                   jax.ShapeDtypeStruct((bsz, nt, tm // MOBA_BLOCK, nq), F32),
                   jax.ShapeDtypeStruct((bsz, seq, MISC_W), F32),
                   jax.ShapeDtypeStruct((bsz, nq, seq), BF16),
                   jax.ShapeDtypeStruct((bsz, seq // CH, nq, CH), BF16),
                   jax.ShapeDtypeStruct((bsz, nqi, seq), BF16),
                   jax.ShapeDtypeStruct((bsz, MISC_W, seq), F32)),
        grid=(bsz, nt),
        in_specs=[pl.BlockSpec((1, tm, d), lambda b, i: (b, i, 0)),
                  pl.BlockSpec((1, 6, d), lambda b, i: (b, 0, 0)),
                  pl.BlockSpec(wk.shape, full),
                  pl.BlockSpec(wm.shape, full),
                  pl.BlockSpec(wt.shape, full),
                  pl.BlockSpec(wmt.shape, full)],
        out_specs=(pl.BlockSpec((1, tm, nq), lambda b, i: (b, i, 0)),
                   pl.BlockSpec((1, 1, tm // MOBA_BLOCK, nq), lambda b, i: (b, i, 0, 0)),
                   pl.BlockSpec((1, tm, MISC_W), lambda b, i: (b, i, 0)),
                   pl.BlockSpec((1, nq, tm), lambda b, i: (b, 0, i)),
                   pl.BlockSpec((1, tm // CH, nq, CH), lambda b, i: (b, i, 0, 0)),
                   pl.BlockSpec((1, nqi, tm), lambda b, i: (b, 0, i)),
                   pl.BlockSpec((1, MISC_W, tm), lambda b, i: (b, 0, i))),
        compiler_params=_params(2),
        name="proj",
    )(x, ada, wk, wm, wt, wmt)
    k, kmean, misc, qT, vT, qiT, miscT = outs
    return k, kmean.reshape(bsz, seq // MOBA_BLOCK, nq), misc, qT, vT, qiT, miscT


def _mask_heads(qT_ref, qm_ref):
    q = qT_ref[0]
    rowh = lax.broadcasted_iota(jnp.int32, q.shape, 0) // HEAD_DIM
    for h in range(HEADS):
        qm_ref[h] = jnp.where(rowh == h, q, jnp.zeros_like(q))


def _init_state(m_ref, l_ref, acc_ref):
    m_ref[...] = jnp.full(m_ref.shape, NEG, F32)
    l_ref[...] = jnp.zeros(l_ref.shape, F32)
    acc_ref[...] = jnp.zeros(acc_ref.shape, F32)


def _softmax_update(h, x, vT_c, m_ref, l_ref, acc_ref):
    m_old = m_ref[h]
    m_new = jnp.maximum(m_old, jnp.max(x, axis=0, keepdims=True))
    alpha = jnp.exp2(m_old - m_new)
    p = jnp.exp2(x - m_new)
    l_ref[h] = alpha * l_ref[h] + jnp.sum(p, axis=0, keepdims=True)
    pv = _dot(vT_c[h * HEAD_DIM:(h + 1) * HEAD_DIM, :], p.astype(BF16))
    acc_ref[h] = alpha * acc_ref[h] + pv
    m_ref[h] = m_new


def _finish(o_ref, l_ref, acc_ref):
    oT = jnp.concatenate([acc_ref[h] / l_ref[h] for h in range(HEADS)], axis=0)
    o_ref[0] = oT.T.astype(BF16)


def _lanes(tile, tq):
    return tile if tq == LANES else jnp.concatenate([tile] * (tq // LANES), axis=1)


def _causal_neg(tq):
    s_i = lax.broadcasted_iota(jnp.int32, (CH, tq), 0)
    t_i = lax.broadcasted_iota(jnp.int32, (CH, tq), 1)
    return jnp.where(s_i <= t_i, 0.0, NEG).astype(F32)


def _key_pos_bias(bias_ref, slopes):
    s_i = lax.broadcasted_iota(jnp.int32, (CH, LANES), 0).astype(F32)
    for h in range(HEADS):
        bias_ref[h] = s_i * float(slopes[h] * LOG2E)


def _attn_specs(mixer, seq):
    return [pl.BlockSpec((1, MIX_W, CH), lambda b, i: (b, mixer, i)),
            pl.BlockSpec((1, seq, MIX_W), lambda b, i: (b, 0, mixer)),
            pl.BlockSpec((1, seq // CH, MIX_W, CH), lambda b, i: (b, 0, mixer, 0))]


def _attn_scratch():
    return [pltpu.VMEM((HEADS, MIX_W, CH), BF16),
            pltpu.VMEM((HEADS, 1, CH), F32),
            pltpu.VMEM((HEADS, 1, CH), F32),
            pltpu.VMEM((HEADS, HEAD_DIM, CH), F32)]


def _fox_kernel(qT_ref, k_ref, vT_ref, misc_ref, bf_ref, o_ref,
                cum_ref, qm_ref, m_ref, l_ref, acc_ref, *, n_chunks):
    i = pl.program_id(1)

    @pl.when(i == 0)
    def _cumulative_gates():
        rr = lax.broadcasted_iota(jnp.int32, (CH, CH), 0)
        cc = lax.broadcasted_iota(jnp.int32, (CH, CH), 1)
        tri = (cc <= rr).astype(F32)
        erow = lax.broadcasted_iota(jnp.int32, (LANES, LANES), 0)

        def body(blk, carry):
            off = pl.multiple_of(blk * CH, CH)
            z = misc_ref[0, pl.ds(off, CH), :] + bf_ref[...]
            ls = jnp.minimum(z, 0.0) - jnp.log1p(jnp.exp(-jnp.abs(z)))
            new = []
            for h in range(HEADS):
                sel = (erow == FL_ROW + h).astype(F32)
                col = _dot_hi(ls, sel)
                cum = _dot_hi(tri, col) + carry[h]
                cum_ref[h, pl.ds(off, CH), :] = cum * LOG2E
                new.append(cum[CH - 1:CH, :])
            return tuple(new)

        lax.fori_loop(0, n_chunks, body, tuple(jnp.zeros((1, LANES), F32) for _ in range(HEADS)))

    _mask_heads(qT_ref, qm_ref)
    _init_state(m_ref, l_ref, acc_ref)

    def chunk(c, neg):
        off = pl.multiple_of(c * CH, CH)
        kc = k_ref[0, pl.ds(off, CH), :]
        vc = vT_ref[0, c]
        for h in range(HEADS):
            x = _dot(kc, qm_ref[h]) - _lanes(cum_ref[h, pl.ds(off, CH), :], CH)
            if neg is not None:
                x = x + neg
            _softmax_update(h, x, vc, m_ref, l_ref, acc_ref)

    chunk(i, _causal_neg(CH))
    lax.fori_loop(0, i, lambda c, _: (chunk(c, None), 0)[1], 0)
    _finish(o_ref, l_ref, acc_ref)


def _fox(qT, k, vT, misc, b_forget_l):
    bsz, seq, _ = k.shape
    n_chunks = seq // CH
    bf = jnp.zeros((1, MISC_W), F32).at[0, FL_ROW:FL_ROW + HEADS].set(b_forget_l)
    return pl.pallas_call(
        functools.partial(_fox_kernel, n_chunks=n_chunks),
        out_shape=jax.ShapeDtypeStruct((bsz, seq, MIX_W), BF16),
        grid=(bsz, n_chunks),
        in_specs=_attn_specs(3, seq) + [
            pl.BlockSpec((1, seq, MISC_W), lambda b, i: (b, 0, 0)),
            pl.BlockSpec((1, MISC_W), lambda b, i: (0, 0))],
        out_specs=pl.BlockSpec((1, CH, MIX_W), lambda b, i: (b, i, 0)),
        scratch_shapes=[pltpu.VMEM((HEADS, seq, LANES), F32)] + _attn_scratch(),
        compiler_params=_params(2),
        name="fox",
    )(qT, k, vT, misc, bf)


def _moba_kernel(qT_ref, k_ref, vT_ref, kmean_ref, o_ref,
                 qm_ref, m_ref, l_ref, acc_ref, bias_ref, rowadd_ref, *, slopes, n_blocks):
    i = pl.program_id(1)
    _mask_heads(qT_ref, qm_ref)
    _init_state(m_ref, l_ref, acc_ref)
    _key_pos_bias(bias_ref, slopes)

    n_i = lax.broadcasted_iota(jnp.int32, (n_blocks, CH), 0)
    n_f = n_i.astype(F32)
    past = n_i < i
    for h in range(HEADS):
        gate = _dot_hi(kmean_ref[0], qm_ref[h].astype(F32))
        gate = jnp.where(past, gate, -jnp.inf)
        chosen = jnp.zeros((n_blocks, CH), F32)
        for _ in range(MOBA_TOPK):
            mx = jnp.max(gate, axis=0, keepdims=True)
            first = jnp.min(jnp.where(gate == mx, n_f, float(n_blocks)), axis=0, keepdims=True)
            pick = n_f == first
            chosen = jnp.where(pick, 1.0, chosen)
            gate = jnp.where(pick, -jnp.inf, gate)
        chosen = jnp.where(past, chosen, 0.0)
        blk_shift = n_f * float(slopes[h] * LOG2E * MOBA_BLOCK)
        rowadd_ref[h] = jnp.where(chosen > 0.5, blk_shift, NEG)

    def chunk(c, diag):
        off = pl.multiple_of(c * CH, CH)
        kc = k_ref[0, pl.ds(off, CH), :]
        vc = vT_ref[0, c]
        for h in range(HEADS):
            x = _dot(kc, qm_ref[h]) + _lanes(bias_ref[h], CH)
            if diag:
                shift = i.astype(F32) * float(slopes[h] * LOG2E * MOBA_BLOCK)
                x = x + (_causal_neg(CH) + shift)
            else:
                x = x + rowadd_ref[h, pl.ds(c, 1), :]
            _softmax_update(h, x, vc, m_ref, l_ref, acc_ref)

    chunk(i, True)
    lax.fori_loop(0, i, lambda c, _: (chunk(c, False), 0)[1], 0)
    _finish(o_ref, l_ref, acc_ref)


def _moba(qT, k, vT, kmean, slopes):
    bsz, seq, _ = k.shape
    n_blocks = seq // MOBA_BLOCK
    return pl.pallas_call(
        functools.partial(_moba_kernel, slopes=tuple(float(s) for s in slopes), n_blocks=n_blocks),
        out_shape=jax.ShapeDtypeStruct((bsz, seq, MIX_W), BF16),
        grid=(bsz, seq // CH),
        in_specs=_attn_specs(2, seq) + [
            pl.BlockSpec((1, n_blocks, MIX_W), lambda b, i: (b, 0, 2))],
        out_specs=pl.BlockSpec((1, CH, MIX_W), lambda b, i: (b, i, 0)),
        scratch_shapes=_attn_scratch() + [pltpu.VMEM((HEADS, CH, LANES), F32),
                                          pltpu.VMEM((HEADS, n_blocks, CH), F32)],
        compiler_params=_params(2),
        name="moba",
    )(qT, k, vT, kmean)


DIL_SPAN = 2048 // CH + 1


def _dilated_kernel(qT_ref, k_ref, vT_ref, o_ref,
                    qm_ref, m_ref, l_ref, acc_ref, table_ref, *, slopes):
    b = pl.program_id(0)
    i = pl.program_id(1)

    @pl.when(jnp.logical_and(b == 0, i == 0))
    def _bias_table():
        s_i = lax.broadcasted_iota(jnp.int32, (CH, CH), 0)
        t_i = lax.broadcasted_iota(jnp.int32, (CH, CH), 1)
        for j in range(DIL_SPAN):
            d = t_i - s_i + j * CH
            ok = d >= 0
            mult = (jnp.where(jnp.logical_and(ok, d <= 128), 1.0, 0.0)
                    + jnp.where(jnp.logical_and(ok, jnp.logical_and(d <= 512, (d & 3) == 0)), 1.0, 0.0)
                    + jnp.where(jnp.logical_and(ok, jnp.logical_and(d <= 2048, (d & 15) == 0)), 1.0, 0.0))
            logm = jnp.where(mult > 0.5, jnp.log2(jnp.maximum(mult, 1.0)), NEG)
            df = d.astype(F32)
            for h in range(HEADS):
                table_ref[h, j] = logm - df * float(slopes[h] * LOG2E)

    _mask_heads(qT_ref, qm_ref)
    _init_state(m_ref, l_ref, acc_ref)

    def chunk(j):
        c = i - j
        off = pl.multiple_of(c * CH, CH)
        kc = k_ref[0, pl.ds(off, CH), :]
        vc = vT_ref[0, c]
        for h in range(HEADS):
            x = _dot(kc, qm_ref[h]) + table_ref[h, j]
            _softmax_update(h, x, vc, m_ref, l_ref, acc_ref)

    chunk(0)
    lax.fori_loop(1, jnp.minimum(i, DIL_SPAN - 1) + 1, lambda j, _: (chunk(j), 0)[1], 0)
    _finish(o_ref, l_ref, acc_ref)


def _dilated(qT, k, vT, slopes):
    bsz, seq, _ = k.shape
    return pl.pallas_call(
        functools.partial(_dilated_kernel, slopes=tuple(float(s) for s in slopes)),
        out_shape=jax.ShapeDtypeStruct((bsz, seq, MIX_W), BF16),
        grid=(bsz, seq // CH),
        in_specs=_attn_specs(1, seq),
        out_specs=pl.BlockSpec((1, CH, MIX_W), lambda b, i: (b, i, 0)),
        scratch_shapes=_attn_scratch() + [pltpu.VMEM((HEADS, DIL_SPAN, CH, CH), F32)],
        compiler_params=_params(2),
        name="dilated",
    )(qT, k, vT)


def _dsa_kernel(qT_ref, k_ref, vT_ref, qiT_ref, misc_ref, miscT_ref, o_ref,
                qm_ref, m_ref, l_ref, acc_ref, bias_ref, qi_ref, key_ref, *, slopes):
    i = pl.program_id(1)
    tq = CH
    _mask_heads(qT_ref, qm_ref)
    _init_state(m_ref, l_ref, acc_ref)
    _key_pos_bias(bias_ref, slopes)

    qi_all = qiT_ref[0]
    zpad = jnp.zeros((MISC_W - IDX_DIM, tq), BF16)
    for h in range(IDX_HEADS):
        qi_ref[h] = jnp.concatenate([qi_all[h * IDX_DIM:(h + 1) * IDX_DIM], zpad], axis=0)
    w_rows = miscT_ref[0, WI_ROW:WI_ROW + IDX_HEADS, :] * float(IDX_HEADS ** -0.5 * IDX_DIM ** -0.5)

    s_i = lax.broadcasted_iota(jnp.int32, (CH, tq), 0)
    t_i = lax.broadcasted_iota(jnp.int32, (CH, tq), 1)

    def score_chunk(c, diag):
        off = pl.multiple_of(c * CH, CH)
        ki = misc_ref[0, pl.ds(off, CH), :].astype(BF16)
        score = jnp.zeros((CH, tq), F32)
        for h in range(IDX_HEADS):
            rel = jnp.maximum(_dot(ki, qi_ref[h]), 0.0)
            score = score + rel * w_rows[h:h + 1, :]
        bits = lax.bitcast_convert_type(score, jnp.int32)
        key = bits ^ ((bits >> 31) & 0x7FFFFFFF)
        key = jnp.where(key == -1, 0, key)
        if diag:
            key = jnp.where(s_i <= t_i, key, INT_MIN)
        key_ref[pl.ds(off, CH), :] = key

    score_chunk(i, True)
    lax.fori_loop(0, i, lambda c, _: (score_chunk(c, False), 0)[1], 0)

    def count(pred):
        def body(c, cnt):
            off = pl.multiple_of(c * CH, CH)
            hit = jnp.where(pred(key_ref[pl.ds(off, CH), :]), 1, 0)
            return cnt + jnp.sum(hit.reshape(CH // 8, 8, tq), axis=0)
        cnt8 = lax.fori_loop(0, i + 1, body, jnp.zeros((8, tq), jnp.int32))
        return jnp.sum(cnt8, axis=0, keepdims=True)

    def bit_step(n, thr):
        inc = jnp.left_shift(jnp.int32(1), 31 - n)
        cand = thr + inc
        cnt = count(lambda key: key >= cand)
        return jnp.where(cnt >= TOPK_KEYS, cand, thr)

    thr = lax.fori_loop(0, 32, bit_step, jnp.full((1, tq), INT_MIN, jnp.int32))
    n_gt = count(lambda key: key > thr)
    n_take = jnp.where(thr == INT_MIN, 0, TOPK_KEYS - n_gt).astype(F32)

    rr = lax.broadcasted_iota(jnp.int32, (CH, CH), 0)
    cc = lax.broadcasted_iota(jnp.int32, (CH, CH), 1)
    tri = jnp.where(cc <= rr, 1.0, 0.0).astype(BF16)

    def attend(c, seen):
        off = pl.multiple_of(c * CH, CH)
        key = key_ref[pl.ds(off, CH), :]
        eq = key == thr
        prefix = _dot(tri, jnp.where(eq, 1.0, 0.0).astype(BF16)) + seen
        take = jnp.logical_or(key > thr, jnp.logical_and(eq, prefix <= n_take))
        neg = jnp.where(take, 0.0, NEG)
        kc = k_ref[0, pl.ds(off, CH), :]
        vc = vT_ref[0, c]
        for h in range(HEADS):
            shift = c.astype(F32) * float(slopes[h] * LOG2E * CH)
            x = _dot(kc, qm_ref[h]) + _lanes(bias_ref[h], tq) + (neg + shift)
            _softmax_update(h, x, vc, m_ref, l_ref, acc_ref)
        return prefix[CH - 1:CH, :]

    lax.fori_loop(0, i + 1, attend, jnp.zeros((1, tq), F32))
    _finish(o_ref, l_ref, acc_ref)


def _dsa(qT, k, vT, qiT, misc, miscT, slopes):
    bsz, seq, _ = k.shape
    nqi = IDX_HEADS * IDX_DIM
    return pl.pallas_call(
        functools.partial(_dsa_kernel, slopes=tuple(float(s) for s in slopes)),
        out_shape=jax.ShapeDtypeStruct((bsz, seq, MIX_W), BF16),
        grid=(bsz, seq // CH),
        in_specs=_attn_specs(0, seq) + [
            pl.BlockSpec((1, nqi, CH), lambda b, i: (b, 0, i)),
            pl.BlockSpec((1, seq, MISC_W), lambda b, i: (b, 0, 0)),
            pl.BlockSpec((1, MISC_W, CH), lambda b, i: (b, 0, i))],
        out_specs=pl.BlockSpec((1, CH, MIX_W), lambda b, i: (b, i, 0)),
        scratch_shapes=_attn_scratch() + [pltpu.VMEM((HEADS, CH, LANES), F32),
                                          pltpu.VMEM((IDX_HEADS, MISC_W, CH), BF16),
                                          pltpu.VMEM((seq, CH), jnp.int32)],
        compiler_params=_params(2),
        name="dsa",
    )(qT, k, vT, qiT, misc, miscT)


def _merge_kernel(x_ref, ada_ref, oa_ref, ob_ref, oc_ref, od_ref, wmg_ref, wbr_ref, wout_ref,
                  g_ref, b_ref, out_ref):
    x = x_ref[0]
    sh = ada_ref[0, 0:1, :]
    sc = ada_ref[0, 1:2, :]
    g1 = ada_ref[0, 2:3, :]
    h = (x * (1.0 + sc) + sh).astype(BF16)
    mixed = None
    for m, o_ref in enumerate((oa_ref, ob_ref, oc_ref, od_ref)):
        gate = jax.nn.sigmoid(_dot(h, wmg_ref[m]))
        term = gate * _dot(o_ref[0], wbr_ref[m])
        mixed = term if mixed is None else mixed + term
    y = _dot(mixed.astype(BF16), wout_ref[...])
    out_ref[0] = _layer_norm(DN_ALPHA * x + g1 * y, g_ref[...], b_ref[...])


def _merge(x, ada, outs, w_mg, w_br, w_out, ln_g, ln_b, tm=512):
    bsz, seq, d = x.shape
    o_spec = pl.BlockSpec((1, tm, MIX_W), lambda b, i: (b, i, 0))
    return pl.pallas_call(
        _merge_kernel,
        out_shape=jax.ShapeDtypeStruct((bsz, seq, d), F32),
        grid=(bsz, seq // tm),
        in_specs=[pl.BlockSpec((1, tm, d), lambda b, i: (b, i, 0)),
                  pl.BlockSpec((1, 6, d), lambda b, i: (b, 0, 0)),
                  o_spec, o_spec, o_spec, o_spec,
                  pl.BlockSpec((N_MIX, d, d), lambda b, i: (0, 0, 0)),
                  pl.BlockSpec((N_MIX, MIX_W, d), lambda b, i: (0, 0, 0)),
                  pl.BlockSpec((d, d), lambda b, i: (0, 0)),
                  pl.BlockSpec((1, d), lambda b, i: (0, 0)),
                  pl.BlockSpec((1, d), lambda b, i: (0, 0))],
        out_specs=pl.BlockSpec((1, tm, d), lambda b, i: (b, i, 0)),
        compiler_params=_params(2),
        name="merge",
    )(x, ada, *outs, w_mg.astype(BF16), w_br.astype(BF16), w_out.astype(BF16),
      ln_g.reshape(1, d), ln_b.reshape(1, d))


def _route(logits_t, bias_col):
    per_group = N_EXPERTS // N_GROUPS
    scores = jax.nn.sigmoid(logits_t)
    biased = scores + bias_col
    s_rows = [scores[e:e + 1] for e in range(N_EXPERTS)]
    b_rows = [biased[e:e + 1] for e in range(N_EXPERTS)]
    best_g = None
    for g in range(N_GROUPS):
        r = b_rows[g * per_group:(g + 1) * per_group]
        gs = None
        for a in range(per_group):
            for b in range(a + 1, per_group):
                pair = r[a] + r[b]
                gs = pair if gs is None else jnp.maximum(gs, pair)
        if best_g is None:
            best_g, best_v = jnp.zeros_like(gs, dtype=jnp.int32), gs
        else:
            better = gs > best_v
            best_g = jnp.where(better, g, best_g)
            best_v = jnp.maximum(best_v, gs)
    masked = [jnp.where(best_g == e // per_group, b_rows[e], -jnp.inf) for e in range(N_EXPERTS)]

    def argmax_first(rows):
        idx, val = jnp.zeros_like(best_g), rows[0]
        for e in range(1, N_EXPERTS):
            better = rows[e] > val
            idx = jnp.where(better, e, idx)
            val = jnp.maximum(val, rows[e])
        return idx

    e1 = argmax_first(masked)
    e2 = argmax_first([jnp.where(e1 == e, -jnp.inf, masked[e]) for e in range(N_EXPERTS)])
    s1 = sum(jnp.where(e1 == e, s_rows[e], 0.0) for e in range(N_EXPERTS))
    s2 = sum(jnp.where(e2 == e, s_rows[e], 0.0) for e in range(N_EXPERTS))
    tot = s1 + s2
    w1, w2 = s1 / tot, s2 / tot
    rows = [jnp.where(e1 == e, w1, 0.0) + jnp.where(e2 == e, w2, 0.0) for e in range(N_EXPERTS)]
    return jnp.concatenate(rows, axis=0)


def _moe_kernel(x_ref, ada_ref, wr_ref, br_ref, wg_ref, wu_ref, wd_ref, g_ref, b_ref, out_ref,
                h_ref, comb_ref, acc_ref, *, tm):
    e = pl.program_id(1)

    @pl.when(e == 0)
    def _router():
        x = x_ref[...]
        h2 = x * (1.0 + ada_ref[0, 4:5, :]) + ada_ref[0, 3:4, :]
        h_ref[...] = h2.astype(BF16)
        logits = _dot_hi(h2, wr_ref[...])
        logits_t = logits.T[0:N_EXPERTS]
        comb_t = _route(logits_t, br_ref[...])
        pad = jnp.zeros((LANES - N_EXPERTS, tm), F32)
        comb_ref[...] = jnp.concatenate([comb_t, pad], axis=0).T
        acc_ref[...] = jnp.zeros(acc_ref.shape, F32)

    h = h_ref[...]
    lane = lax.broadcasted_iota(jnp.int32, (tm, LANES), 1)
    w_e = jnp.sum(jnp.where(lane == e, comb_ref[...], 0.0), axis=1, keepdims=True)
    gate = _dot(h, wg_ref[0, 0])
    up = _dot(h, wu_ref[0, 0])
    hid = (jax.nn.silu(gate) * up * w_e).astype(BF16)
    acc_ref[...] += _dot(hid, wd_ref[0, 0])

    @pl.when(e == N_EXPERTS - 1)
    def _norm():
        z = DN_ALPHA * x_ref[...] + ada_ref[0, 5:6, :] * acc_ref[...]
        out_ref[...] = _layer_norm(z, g_ref[...], b_ref[...])


def _moe(x, ada_rows, w_router, b_router, w_gate, w_up, w_down, ln_g, ln_b, batch_len, tm=1024):
    n, d = x.shape
    per_b = batch_len // tm
    wr = jnp.pad(w_router, ((0, 0), (0, LANES - N_EXPERTS)))
    br = b_router.reshape(N_EXPERTS, 1)
    return pl.pallas_call(
        functools.partial(_moe_kernel, tm=tm),
        out_shape=jax.ShapeDtypeStruct((n, d), F32),
        grid=(n // tm, N_EXPERTS),
        in_specs=[pl.BlockSpec((tm, d), lambda i, e: (i, 0)),
                  pl.BlockSpec((1, 6, d), lambda i, e: (i // per_b, 0, 0)),
                  pl.BlockSpec((d, LANES), lambda i, e: (0, 0)),
                  pl.BlockSpec((N_EXPERTS, 1), lambda i, e: (0, 0)),
                  pl.BlockSpec((1, 1, d, D_EXPERT), lambda i, e: (0, e, 0, 0)),
                  pl.BlockSpec((1, 1, d, D_EXPERT), lambda i, e: (0, e, 0, 0)),
                  pl.BlockSpec((1, 1, D_EXPERT, d), lambda i, e: (0, e, 0, 0)),
                  pl.BlockSpec((1, d), lambda i, e: (0, 0)),
                  pl.BlockSpec((1, d), lambda i, e: (0, 0))],
        out_specs=pl.BlockSpec((tm, d), lambda i, e: (i, 0)),
        scratch_shapes=[pltpu.VMEM((tm, d), BF16),
                        pltpu.VMEM((tm, LANES), F32),
                        pltpu.VMEM((tm, d), F32)],
        compiler_params=_params(2),
        name="moe",
    )(x, ada_rows, wr, br, w_gate[None].astype(BF16), w_up[None].astype(BF16), w_down[None].astype(BF16),
      ln_g.reshape(1, d), ln_b.reshape(1, d))


def kernel(x, c, w_in, b_forget, w_branch, w_merge_gate, w_out, w_ada, b_ada, ln1_g, ln1_b, ln2_g, ln2_b,
           w_router, b_router, w_exp_gate, w_exp_up, w_exp_down):
    bsz, seq, d = x.shape
    slopes = _alibi_slopes()
    ada_all = _ada(c, w_ada, b_ada).reshape(DEPTH, bsz, 6, d)
    for l in range(DEPTH):
        ada = ada_all[l]
        k, kmean, misc, qT, vT, qiT, miscT = _proj(x, ada, w_in[l])
        o_a = _dsa(qT, k, vT, qiT, misc, miscT, slopes[0])
        o_b = _dilated(qT, k, vT, slopes[1])
        o_c = _moba(qT, k, vT, kmean, slopes[2])
        o_d = _fox(qT, k, vT, misc, b_forget[l])
        x = _merge(x, ada, (o_a, o_b, o_c, o_d), w_merge_gate[l], w_branch[l], w_out[l], ln1_g[l], ln1_b[l])
        x = _moe(x.reshape(bsz * seq, d), ada, w_router, b_router, w_exp_gate[l], w_exp_up[l], w_exp_down[l],
                 ln2_g[l], ln2_b[l], seq).reshape(bsz, seq, d)
    return x
```

```python
import functools

import numpy as np
import jax
import jax.numpy as jnp
from jax import lax
from jax.experimental import pallas as pl
from jax.experimental.pallas import tpu as pltpu

D_MODEL = 1024
HEAD_DIM = 64
HEADS = 4
MIX_W = HEADS * HEAD_DIM
N_MIX = 4
IDX_HEADS = 8
IDX_DIM = 64
TOPK_KEYS = 256
MOBA_BLOCK = 256
MOBA_TOPK = 3
N_EXPERTS = 16
N_GROUPS = 4
D_EXPERT = 512
DEPTH = 2
DN_ALPHA = (2 * DEPTH) ** 0.25
LN_EPS = 1e-5
IN_COLS = 3 * N_MIX * MIX_W + IDX_HEADS * IDX_DIM + IDX_DIM + IDX_HEADS + HEADS

CH = 256
LANES = 128
MISC_W = 128
WI_ROW = IDX_DIM
FL_ROW = IDX_DIM + IDX_HEADS
LOG2E = 1.4426950408889634
NEG = -1e30
QSCALE = HEAD_DIM ** -0.5 * LOG2E
INT_MIN = -(2 ** 31)
VMEM_LIMIT = 56 * 1024 * 1024

F32 = jnp.float32
BF16 = jnp.bfloat16
HI = lax.Precision.HIGHEST
NT = (((1,), (1,)), ((), ()))


def _alibi_slopes():
    n = 3 * HEADS
    s = 2.0 ** (-8.0 * np.arange(1, n + 1) / n)
    return s.reshape(HEADS, 3).T


def _dot(a, b):
    return jnp.dot(a, b, preferred_element_type=F32)


def _dot_hi(a, b):
    return jnp.dot(a, b, preferred_element_type=F32, precision=HI)


def _params(n_axes):
    return pltpu.CompilerParams(dimension_semantics=("arbitrary",) * n_axes,
                                vmem_limit_bytes=VMEM_LIMIT)


def _layer_norm(z, g, b):
    mu = jnp.mean(z, axis=-1, keepdims=True)
    var = jnp.mean(jnp.square(z - mu), axis=-1, keepdims=True)
    return (z - mu) * lax.rsqrt(var + LN_EPS) * g + b


def _ada_kernel(c_ref, w_ref, b_ref, o_ref):
    o_ref[0] = _dot_hi(c_ref[...], w_ref[0]) + b_ref[0]


def _ada(c, w_ada, b_ada):
    depth, d, n = w_ada.shape
    bsz = c.shape[0]
    tn = D_MODEL
    return pl.pallas_call(
        _ada_kernel,
        out_shape=jax.ShapeDtypeStruct((depth, bsz, n), F32),
        grid=(depth, n // tn),
        in_specs=[pl.BlockSpec((bsz, d), lambda l, j: (0, 0)),
                  pl.BlockSpec((1, d, tn), lambda l, j: (l, 0, j)),
                  pl.BlockSpec((1, 1, tn), lambda l, j: (l, 0, j))],
        out_specs=pl.BlockSpec((1, bsz, tn), lambda l, j: (l, 0, j)),
        compiler_params=_params(2),
        name="ada",
    )(c, w_ada, b_ada.reshape(depth, 1, n))


def _proj_kernel(x_ref, ada_ref, wk_ref, wm_ref, wt_ref, wmt_ref,
                 k_ref, kmean_ref, misc_ref, qT_ref, vT_ref, qiT_ref, miscT_ref, *, tm):
    x = x_ref[0]
    sh = ada_ref[0, 0:1, :]
    sc = ada_ref[0, 1:2, :]
    h = (x * (1.0 + sc) + sh).astype(BF16)
    kf = _dot(h, wk_ref[...])
    k_ref[0] = kf.astype(BF16)
    for g in range(tm // MOBA_BLOCK):
        kmean_ref[0, 0, g:g + 1, :] = jnp.mean(kf[g * MOBA_BLOCK:(g + 1) * MOBA_BLOCK], axis=0, keepdims=True)
    misc_ref[0] = _dot(h, wm_ref[...])
    t = lax.dot_general(wt_ref[...], h, NT, preferred_element_type=F32)
    nq = N_MIX * MIX_W
    qT_ref[0] = (t[0:nq] * QSCALE).astype(BF16)
    for g in range(tm // CH):
        vT_ref[0, g] = t[nq:2 * nq, g * CH:(g + 1) * CH].astype(BF16)
    qiT_ref[0] = t[2 * nq:].astype(BF16)
    miscT_ref[0] = lax.dot_general(wmt_ref[...], h, NT, preferred_element_type=F32)


def _proj(x, ada, w_in, tm=512):
    bsz, seq, d = x.shape
    nq = N_MIX * MIX_W
    nqi = IDX_HEADS * IDX_DIM
    w = w_in.astype(BF16)
    wk = w[:, nq:2 * nq]
    wm = jnp.pad(w[:, 3 * nq + nqi:], ((0, 0), (0, MISC_W - (IN_COLS - 3 * nq - nqi))))
    wt = jnp.concatenate([w[:, 0:nq], w[:, 2 * nq:3 * nq], w[:, 3 * nq:3 * nq + nqi]], axis=1).T
    wmt = wm.T
    nt = seq // tm
    full = lambda b, i: (0, 0)
    outs = pl.pallas_call(
        functools.partial(_proj_kernel, tm=tm),
        out_shape=(jax.ShapeDtypeStruct((bsz, seq, nq), BF16),
                   jax.ShapeDtypeStruct((bsz, nt, tm // MOBA_BLOCK, nq), F32),
                   jax.ShapeDtypeStruct((bsz, seq, MISC_W), F32),
                   jax.ShapeDtypeStruct((bsz, nq, seq), BF16),
                   jax.ShapeDtypeStruct((bsz, seq // CH, nq, CH), BF16),
                   jax.ShapeDtypeStruct((bsz, nqi, seq), BF16),
                   jax.ShapeDtypeStruct((bsz, MISC_W, seq), F32)),
        grid=(bsz, nt),
        in_specs=[pl.BlockSpec((1, tm, d), lambda b, i: (b, i, 0)),
                  pl.BlockSpec((1, 6, d), lambda b, i: (b, 0, 0)),
                  pl.BlockSpec(wk.shape, full),
                  pl.BlockSpec(wm.shape, full),
                  pl.BlockSpec(wt.shape, full),
                  pl.BlockSpec(wmt.shape, full)],
        out_specs=(pl.BlockSpec((1, tm, nq), lambda b, i: (b, i, 0)),
                   pl.BlockSpec((1, 1, tm // MOBA_BLOCK, nq), lambda b, i: (b, i, 0, 0)),
                   pl.BlockSpec((1, tm, MISC_W), lambda b, i: (b, i, 0)),
                   pl.BlockSpec((1, nq, tm), lambda b, i: (b, 0, i)),
                   pl.BlockSpec((1, tm // CH, nq, CH), lambda b, i: (b, i, 0, 0)),
                   pl.BlockSpec((1, nqi, tm), lambda b, i: (b, 0, i)),
                   pl.BlockSpec((1, MISC_W, tm), lambda b, i: (b, 0, i))),
        compiler_params=_params(2),
        name="proj",
    )(x, ada, wk, wm, wt, wmt)
    k, kmean, misc, qT, vT, qiT, miscT = outs
    return k, kmean.reshape(bsz, seq // MOBA_BLOCK, nq), misc, qT, vT, qiT, miscT


def _mask_heads(qT_ref, qm_ref):
    q = qT_ref[0]
    rowh = lax.broadcasted_iota(jnp.int32, q.shape, 0) // HEAD_DIM
    for h in range(HEADS):
        qm_ref[h] = jnp.where(rowh == h, q, jnp.zeros_like(q))


def _init_state(m_ref, l_ref, acc_ref):
    m_ref[...] = jnp.full(m_ref.shape, NEG, F32)
    l_ref[...] = jnp.zeros(l_ref.shape, F32)
    acc_ref[...] = jnp.zeros(acc_ref.shape, F32)


def _attention(i, first, k_ref, vT_ref, st, logits_fn, chunk_ctx=None):
    qm_ref, m_ref, l_ref, acc_ref, s_ref, mx_ref = st

    def produce(n, slot):
        c = jnp.minimum(first + n, i)
        off = pl.multiple_of(c * CH, CH)
        kc = k_ref[0, pl.ds(off, CH), :]
        ctx = chunk_ctx(c) if chunk_ctx is not None else None
        for h in range(HEADS):
            x = logits_fn(h, c, _dot(kc, qm_ref[h]), ctx)
            s_ref[slot, h] = x
            mx_ref[slot, h] = jnp.max(x, axis=0, keepdims=True)

    def consume(n, slot):
        vc = vT_ref[0, first + n]
        for h in range(HEADS):
            m_old = m_ref[h]
            m_new = jnp.maximum(m_old, mx_ref[slot, h])
            alpha = jnp.exp2(m_old - m_new)
            p = jnp.exp2(s_ref[slot, h] - m_new)
            l_ref[h] = alpha * l_ref[h] + jnp.sum(p, axis=0, keepdims=True)
            pv = _dot(vc[h * HEAD_DIM:(h + 1) * HEAD_DIM, :], p.astype(BF16))
            acc_ref[h] = alpha * acc_ref[h] + pv
            m_ref[h] = m_new

    n_chunks = i - first + 1
    produce(0, 0)

    def pair(p, carry):
        n0 = 2 * p
        produce(n0 + 1, 1)
        consume(n0, 0)

        @pl.when(n0 + 1 < n_chunks)
        def _second():
            produce(n0 + 2, 0)
            consume(n0 + 1, 1)

        return carry

    lax.fori_loop(0, (n_chunks + 1) // 2, pair, 0)


def _finish(o_ref, l_ref, acc_ref):
    oT = jnp.concatenate([acc_ref[h] / l_ref[h] for h in range(HEADS)], axis=0)
    o_ref[0] = oT.T.astype(BF16)


def _lanes(tile, tq):
    return tile if tq == LANES else jnp.concatenate([tile] * (tq // LANES), axis=1)


def _causal_neg(tq):
    s_i = lax.broadcasted_iota(jnp.int32, (CH, tq), 0)
    t_i = lax.broadcasted_iota(jnp.int32, (CH, tq), 1)
    return jnp.where(s_i <= t_i, 0.0, NEG).astype(F32)


def _is_diag(c, i):
    return (c == i).astype(jnp.int32)


def _key_pos_bias(bias_ref, slopes):
    s_i = lax.broadcasted_iota(jnp.int32, (CH, LANES), 0).astype(F32)
    for h in range(HEADS):
        bias_ref[h] = s_i * float(slopes[h] * LOG2E)


def _attn_specs(mixer, seq):
    return [pl.BlockSpec((1, MIX_W, CH), lambda b, i: (b, mixer, i)),
            pl.BlockSpec((1, seq, MIX_W), lambda b, i: (b, 0, mixer)),
            pl.BlockSpec((1, seq // CH, MIX_W, CH), lambda b, i: (b, 0, mixer, 0))]


def _attn_scratch():
    return [pltpu.VMEM((HEADS, MIX_W, CH), BF16),
            pltpu.VMEM((HEADS, 1, CH), F32),
            pltpu.VMEM((HEADS, 1, CH), F32),
            pltpu.VMEM((HEADS, HEAD_DIM, CH), F32),
            pltpu.VMEM((2, HEADS, CH, CH), F32),
            pltpu.VMEM((2, HEADS, 1, CH), F32)]


def _fox_kernel(qT_ref, k_ref, vT_ref, misc_ref, bf_ref, o_ref,
                cum_ref, neg_ref, *st, n_chunks):
    qm_ref, m_ref, l_ref, acc_ref = st[:4]
    i = pl.program_id(1)

    @pl.when(i == 0)
    def _cumulative_gates():
        rr = lax.broadcasted_iota(jnp.int32, (CH, CH), 0)
        cc = lax.broadcasted_iota(jnp.int32, (CH, CH), 1)
        tri = (cc <= rr).astype(F32)
        erow = lax.broadcasted_iota(jnp.int32, (LANES, LANES), 0)

        def body(blk, carry):
            off = pl.multiple_of(blk * CH, CH)
            z = misc_ref[0, pl.ds(off, CH), :] + bf_ref[...]
            ls = jnp.minimum(z, 0.0) - jnp.log1p(jnp.exp(-jnp.abs(z)))
            new = []
            for h in range(HEADS):
                sel = (erow == FL_ROW + h).astype(F32)
                col = _dot_hi(ls, sel)
                cum = _dot_hi(tri, col) + carry[h]
                cum_ref[h, pl.ds(off, CH), :] = cum * LOG2E
                new.append(cum[CH - 1:CH, :])
            return tuple(new)

        lax.fori_loop(0, n_chunks, body, tuple(jnp.zeros((1, LANES), F32) for _ in range(HEADS)))

    _mask_heads(qT_ref, qm_ref)
    _init_state(m_ref, l_ref, acc_ref)

    @pl.when(jnp.logical_and(pl.program_id(0) == 0, i == 0))
    def _mask_table():
        neg_ref[0] = jnp.zeros((CH, CH), F32)
        neg_ref[1] = _causal_neg(CH)

    def logits(h, c, qk, _):
        off = pl.multiple_of(c * CH, CH)
        return qk - _lanes(cum_ref[h, pl.ds(off, CH), :], CH) + neg_ref[_is_diag(c, i)]

    _attention(i, 0, k_ref, vT_ref, st, logits)
    _finish(o_ref, l_ref, acc_ref)


def _fox(qT, k, vT, misc, b_forget_l):
    bsz, seq, _ = k.shape
    n_chunks = seq // CH
    bf = jnp.zeros((1, MISC_W), F32).at[0, FL_ROW:FL_ROW + HEADS].set(b_forget_l)
    return pl.pallas_call(
        functools.partial(_fox_kernel, n_chunks=n_chunks),
        out_shape=jax.ShapeDtypeStruct((bsz, seq, MIX_W), BF16),
        grid=(bsz, n_chunks),
        in_specs=_attn_specs(3, seq) + [
            pl.BlockSpec((1, seq, MISC_W), lambda b, i: (b, 0, 0)),
            pl.BlockSpec((1, MISC_W), lambda b, i: (0, 0))],
        out_specs=pl.BlockSpec((1, CH, MIX_W), lambda b, i: (b, i, 0)),
        scratch_shapes=[pltpu.VMEM((HEADS, seq, LANES), F32),
                        pltpu.VMEM((2, CH, CH), F32)] + _attn_scratch(),
        compiler_params=_params(2),
        name="fox",
    )(qT, k, vT, misc, bf)


def _moba_kernel(qT_ref, k_ref, vT_ref, kmean_ref, o_ref,
                 bias_ref, rowadd_ref, *st, slopes, n_blocks):
    qm_ref, m_ref, l_ref, acc_ref = st[:4]
    i = pl.program_id(1)
    _mask_heads(qT_ref, qm_ref)
    _init_state(m_ref, l_ref, acc_ref)

    @pl.when(jnp.logical_and(pl.program_id(0) == 0, i == 0))
    def _bias_table():
        s_f = lax.broadcasted_iota(jnp.int32, (CH, CH), 0).astype(F32)
        causal = _causal_neg(CH)
        for h in range(HEADS):
            pos = s_f * float(slopes[h] * LOG2E)
            bias_ref[0, h] = pos
            bias_ref[1, h] = pos + causal

    n_i = lax.broadcasted_iota(jnp.int32, (n_blocks, CH), 0)
    n_f = n_i.astype(F32)
    past = n_i < i
    for h in range(HEADS):
        gate = _dot_hi(kmean_ref[0], qm_ref[h].astype(F32))
        gate = jnp.where(past, gate, -jnp.inf)
        chosen = jnp.zeros((n_blocks, CH), F32)
        for _ in range(MOBA_TOPK):
            mx = jnp.max(gate, axis=0, keepdims=True)
            first = jnp.min(jnp.where(gate == mx, n_f, float(n_blocks)), axis=0, keepdims=True)
            pick = n_f == first
            chosen = jnp.where(pick, 1.0, chosen)
            gate = jnp.where(pick, -jnp.inf, gate)
        chosen = jnp.where(past, chosen, 0.0)
        blk_shift = n_f * float(slopes[h] * LOG2E * MOBA_BLOCK)
        rowadd_ref[h] = jnp.where(jnp.logical_or(chosen > 0.5, n_i == i), blk_shift, NEG)

    def logits(h, c, qk, _):
        return qk + bias_ref[_is_diag(c, i), h] + rowadd_ref[h, pl.ds(c, 1), :]

    _attention(i, 0, k_ref, vT_ref, st, logits)
    _finish(o_ref, l_ref, acc_ref)


def _moba(qT, k, vT, kmean, slopes):
    bsz, seq, _ = k.shape
    n_blocks = seq // MOBA_BLOCK
    return pl.pallas_call(
        functools.partial(_moba_kernel, slopes=tuple(float(s) for s in slopes), n_blocks=n_blocks),
        out_shape=jax.ShapeDtypeStruct((bsz, seq, MIX_W), BF16),
        grid=(bsz, seq // CH),
        in_specs=_attn_specs(2, seq) + [
            pl.BlockSpec((1, n_blocks, MIX_W), lambda b, i: (b, 0, 2))],
        out_specs=pl.BlockSpec((1, CH, MIX_W), lambda b, i: (b, i, 0)),
        scratch_shapes=[pltpu.VMEM((2, HEADS, CH, CH), F32),
                        pltpu.VMEM((HEADS, n_blocks, CH), F32)] + _attn_scratch(),
        compiler_params=_params(2),
        name="moba",
    )(qT, k, vT, kmean)


DIL_SPAN = 2048 // CH + 1


def _dilated_kernel(qT_ref, k_ref, vT_ref, o_ref, table_ref, *st, slopes):
    qm_ref, m_ref, l_ref, acc_ref = st[:4]
    b = pl.program_id(0)
    i = pl.program_id(1)

    @pl.when(jnp.logical_and(b == 0, i == 0))
    def _bias_table():
        s_i = lax.broadcasted_iota(jnp.int32, (CH, CH), 0)
        t_i = lax.broadcasted_iota(jnp.int32, (CH, CH), 1)
        for j in range(DIL_SPAN):
            d = t_i - s_i + j * CH
            ok = d >= 0
            mult = (jnp.where(jnp.logical_and(ok, d <= 128), 1.0, 0.0)
                    + jnp.where(jnp.logical_and(ok, jnp.logical_and(d <= 512, (d & 3) == 0)), 1.0, 0.0)
                    + jnp.where(jnp.logical_and(ok, jnp.logical_and(d <= 2048, (d & 15) == 0)), 1.0, 0.0))
            logm = jnp.where(mult > 0.5, jnp.log2(jnp.maximum(mult, 1.0)), NEG)
            df = d.astype(F32)
            for h in range(HEADS):
                table_ref[h, j] = logm - df * float(slopes[h] * LOG2E)

    _mask_heads(qT_ref, qm_ref)
    _init_state(m_ref, l_ref, acc_ref)

    def logits(h, c, qk, _):
        return qk + table_ref[h, i - c]

    _attention(i, jnp.maximum(i - (DIL_SPAN - 1), 0), k_ref, vT_ref, st, logits)
    _finish(o_ref, l_ref, acc_ref)


def _dilated(qT, k, vT, slopes):
    bsz, seq, _ = k.shape
    return pl.pallas_call(
        functools.partial(_dilated_kernel, slopes=tuple(float(s) for s in slopes)),
        out_shape=jax.ShapeDtypeStruct((bsz, seq, MIX_W), BF16),
        grid=(bsz, seq // CH),
        in_specs=_attn_specs(1, seq),
        out_specs=pl.BlockSpec((1, CH, MIX_W), lambda b, i: (b, i, 0)),
        scratch_shapes=[pltpu.VMEM((HEADS, DIL_SPAN, CH, CH), F32)] + _attn_scratch(),
        compiler_params=_params(2),
        name="dilated",
    )(qT, k, vT)


def _dsa_kernel(qT_ref, k_ref, vT_ref, qiT_ref, misc_ref, miscT_ref, o_ref,
                bias_ref, qi_ref, key_ref, seen_ref, *st, slopes):
    qm_ref, m_ref, l_ref, acc_ref = st[:4]
    i = pl.program_id(1)
    tq = CH
    _mask_heads(qT_ref, qm_ref)
    _init_state(m_ref, l_ref, acc_ref)
    _key_pos_bias(bias_ref, slopes)

    qi_all = qiT_ref[0]
    zpad = jnp.zeros((MISC_W - IDX_DIM, tq), BF16)
    for h in range(IDX_HEADS):
        qi_ref[h] = jnp.concatenate([qi_all[h * IDX_DIM:(h + 1) * IDX_DIM], zpad], axis=0)
    w_rows = miscT_ref[0, WI_ROW:WI_ROW + IDX_HEADS, :] * float(IDX_HEADS ** -0.5 * IDX_DIM ** -0.5)

    s_i = lax.broadcasted_iota(jnp.int32, (CH, tq), 0)
    t_i = lax.broadcasted_iota(jnp.int32, (CH, tq), 1)

    def score_chunk(c, diag):
        off = pl.multiple_of(c * CH, CH)
        ki = misc_ref[0, pl.ds(off, CH), :].astype(BF16)
        score = jnp.zeros((CH, tq), F32)
        for h in range(IDX_HEADS):
            rel = jnp.maximum(_dot(ki, qi_ref[h]), 0.0)
            score = score + rel * w_rows[h:h + 1, :]
        bits = lax.bitcast_convert_type(score, jnp.int32)
        key = bits ^ ((bits >> 31) & 0x7FFFFFFF)
        key = jnp.where(key == -1, 0, key)
        if diag:
            key = jnp.where(s_i <= t_i, key, INT_MIN)
        key_ref[pl.ds(off, CH), :] = key

    score_chunk(i, True)
    lax.fori_loop(0, i, lambda c, _: (score_chunk(c, False), 0)[1], 0)

    def count(pred):
        def body(c, cnt):
            off = pl.multiple_of(c * CH, CH)
            hit = jnp.where(pred(key_ref[pl.ds(off, CH), :]), 1, 0)
            return cnt + jnp.sum(hit.reshape(CH // 8, 8, tq), axis=0)
        cnt8 = lax.fori_loop(0, i + 1, body, jnp.zeros((8, tq), jnp.int32))
        return jnp.sum(cnt8, axis=0, keepdims=True)

    def search_cond(state):
        n, _, _, unresolved = state
        return jnp.logical_and(n < 32, unresolved > 0.5)

    def search_step(state):
        n, thr, n_ge, _ = state
        cand = thr + jnp.left_shift(jnp.int32(1), 31 - n)
        cnt = count(lambda key: key >= cand)
        ok = cnt >= TOPK_KEYS
        thr = jnp.where(ok, cand, thr)
        n_ge = jnp.where(ok, cnt, n_ge)
        unresolved = jnp.max(jnp.where(n_ge == TOPK_KEYS, 0.0, 1.0))
        return n + 1, thr, n_ge, unresolved

    _, thr, n_ge, _ = lax.while_loop(
        search_cond, search_step,
        (jnp.int32(0), jnp.full((1, tq), INT_MIN, jnp.int32),
         jnp.zeros((1, tq), jnp.int32) + (i + 1) * CH,
         jnp.where((i + 1) * CH == TOPK_KEYS, 0.0, 1.0).astype(F32)))
    tied = jnp.logical_and(n_ge > TOPK_KEYS, thr != INT_MIN)
    any_tied = jnp.max(jnp.where(tied, 1.0, 0.0)) > 0.5

    def logits(h, c, qk, neg):
        pos = bias_ref[h] + c.astype(F32) * float(slopes[h] * LOG2E * CH)
        return qk + _lanes(pos, tq) + neg

    @pl.when(jnp.logical_not(any_tied))
    def _no_ties():
        thr_lo = jnp.where(thr == INT_MIN, INT_MIN + 1, thr)

        def selection(c):
            off = pl.multiple_of(c * CH, CH)
            return jnp.where(key_ref[pl.ds(off, CH), :] >= thr_lo, 0.0, NEG)

        _attention(i, 0, k_ref, vT_ref, st, logits, selection)

    @pl.when(any_tied)
    def _ties():
        n_gt = count(lambda key: key > thr)
        n_take = jnp.where(thr == INT_MIN, 0, TOPK_KEYS - n_gt).astype(F32)
        rr = lax.broadcasted_iota(jnp.int32, (CH, CH), 0)
        cc = lax.broadcasted_iota(jnp.int32, (CH, CH), 1)
        tri = jnp.where(cc <= rr, 1.0, 0.0).astype(BF16)
        seen_ref[...] = jnp.zeros(seen_ref.shape, F32)

        def selection(c):
            off = pl.multiple_of(c * CH, CH)
            key = key_ref[pl.ds(off, CH), :]
            eq = key == thr
            rank = _dot(tri, jnp.where(eq, 1.0, 0.0).astype(BF16)) + seen_ref[...]
            seen_ref[...] = rank[CH - 1:CH, :]
            take = jnp.logical_or(key > thr, jnp.logical_and(eq, rank <= n_take))
            return jnp.where(take, 0.0, NEG)

        _attention(i, 0, k_ref, vT_ref, st, logits, selection)

    _finish(o_ref, l_ref, acc_ref)


def _dsa(qT, k, vT, qiT, misc, miscT, slopes):
    bsz, seq, _ = k.shape
    nqi = IDX_HEADS * IDX_DIM
    return pl.pallas_call(
        functools.partial(_dsa_kernel, slopes=tuple(float(s) for s in slopes)),
        out_shape=jax.ShapeDtypeStruct((bsz, seq, MIX_W), BF16),
        grid=(bsz, seq // CH),
        in_specs=_attn_specs(0, seq) + [
            pl.BlockSpec((1, nqi, CH), lambda b, i: (b, 0, i)),
            pl.BlockSpec((1, seq, MISC_W), lambda b, i: (b, 0, 0)),
            pl.BlockSpec((1, MISC_W, CH), lambda b, i: (b, 0, i))],
        out_specs=pl.BlockSpec((1, CH, MIX_W), lambda b, i: (b, i, 0)),
        scratch_shapes=[pltpu.VMEM((HEADS, CH, LANES), F32),
                        pltpu.VMEM((IDX_HEADS, MISC_W, CH), BF16),
                        pltpu.VMEM((seq, CH), jnp.int32),
                        pltpu.VMEM((1, CH), F32)] + _attn_scratch(),
        compiler_params=_params(2),
        name="dsa",
    )(qT, k, vT, qiT, misc, miscT)


def _merge_kernel(x_ref, ada_ref, oa_ref, ob_ref, oc_ref, od_ref, wmg_ref, wbr_ref, wout_ref,
                  g_ref, b_ref, out_ref):
    x = x_ref[0]
    sh = ada_ref[0, 0:1, :]
    sc = ada_ref[0, 1:2, :]
    g1 = ada_ref[0, 2:3, :]
    h = (x * (1.0 + sc) + sh).astype(BF16)
    mixed = None
    for m, o_ref in enumerate((oa_ref, ob_ref, oc_ref, od_ref)):
        gate = jax.nn.sigmoid(_dot(h, wmg_ref[m]))
        term = gate * _dot(o_ref[0], wbr_ref[m])
        mixed = term if mixed is None else mixed + term
    y = _dot(mixed.astype(BF16), wout_ref[...])
    out_ref[0] = _layer_norm(DN_ALPHA * x + g1 * y, g_ref[...], b_ref[...])


def _merge(x, ada, outs, w_mg, w_br, w_out, ln_g, ln_b, tm=512):
    bsz, seq, d = x.shape
    o_spec = pl.BlockSpec((1, tm, MIX_W), lambda b, i: (b, i, 0))
    return pl.pallas_call(
        _merge_kernel,
        out_shape=jax.ShapeDtypeStruct((bsz, seq, d), F32),
        grid=(bsz, seq // tm),
        in_specs=[pl.BlockSpec((1, tm, d), lambda b, i: (b, i, 0)),
                  pl.BlockSpec((1, 6, d), lambda b, i: (b, 0, 0)),
                  o_spec, o_spec, o_spec, o_spec,
                  pl.BlockSpec((N_MIX, d, d), lambda b, i: (0, 0, 0)),
                  pl.BlockSpec((N_MIX, MIX_W, d), lambda b, i: (0, 0, 0)),
                  pl.BlockSpec((d, d), lambda b, i: (0, 0)),
                  pl.BlockSpec((1, d), lambda b, i: (0, 0)),
                  pl.BlockSpec((1, d), lambda b, i: (0, 0))],
        out_specs=pl.BlockSpec((1, tm, d), lambda b, i: (b, i, 0)),
        compiler_params=_params(2),
        name="merge",
    )(x, ada, *outs, w_mg.astype(BF16), w_br.astype(BF16), w_out.astype(BF16),
      ln_g.reshape(1, d), ln_b.reshape(1, d))


def _route(logits_t, bias_col):
    per_group = N_EXPERTS // N_GROUPS
    scores = jax.nn.sigmoid(logits_t)
    biased = scores + bias_col
    s_rows = [scores[e:e + 1] for e in range(N_EXPERTS)]
    b_rows = [biased[e:e + 1] for e in range(N_EXPERTS)]
    best_g = None
    for g in range(N_GROUPS):
        r = b_rows[g * per_group:(g + 1) * per_group]
        gs = None
        for a in range(per_group):
            for b in range(a + 1, per_group):
                pair = r[a] + r[b]
                gs = pair if gs is None else jnp.maximum(gs, pair)
        if best_g is None:
            best_g, best_v = jnp.zeros_like(gs, dtype=jnp.int32), gs
        else:
            better = gs > best_v
            best_g = jnp.where(better, g, best_g)
            best_v = jnp.maximum(best_v, gs)
    masked = [jnp.where(best_g == e // per_group, b_rows[e], -jnp.inf) for e in range(N_EXPERTS)]

    def argmax_first(rows):
        idx, val = jnp.zeros_like(best_g), rows[0]
        for e in range(1, N_EXPERTS):
            better = rows[e] > val
            idx = jnp.where(better, e, idx)
            val = jnp.maximum(val, rows[e])
        return idx

    e1 = argmax_first(masked)
    e2 = argmax_first([jnp.where(e1 == e, -jnp.inf, masked[e]) for e in range(N_EXPERTS)])
    s1 = sum(jnp.where(e1 == e, s_rows[e], 0.0) for e in range(N_EXPERTS))
    s2 = sum(jnp.where(e2 == e, s_rows[e], 0.0) for e in range(N_EXPERTS))
    tot = s1 + s2
    w1, w2 = s1 / tot, s2 / tot
    rows = [jnp.where(e1 == e, w1, 0.0) + jnp.where(e2 == e, w2, 0.0) for e in range(N_EXPERTS)]
    return jnp.concatenate(rows, axis=0)


def _moe_kernel(x_ref, ada_ref, wr_ref, br_ref, wg_ref, wu_ref, wd_ref, g_ref, b_ref, out_ref,
                h_ref, comb_ref, acc_ref, *, tm):
    e = pl.program_id(1)

    @pl.when(e == 0)
    def _router():
        x = x_ref[...]
        h2 = x * (1.0 + ada_ref[0, 4:5, :]) + ada_ref[0, 3:4, :]
        h_ref[...] = h2.astype(BF16)
        logits = _dot_hi(h2, wr_ref[...])
        logits_t = logits.T[0:N_EXPERTS]
        comb_t = _route(logits_t, br_ref[...])
        pad = jnp.zeros((LANES - N_EXPERTS, tm), F32)
        comb_ref[...] = jnp.concatenate([comb_t, pad], axis=0).T
        acc_ref[...] = jnp.zeros(acc_ref.shape, F32)

    h = h_ref[...]
    lane = lax.broadcasted_iota(jnp.int32, (tm, LANES), 1)
    w_e = jnp.sum(jnp.where(lane == e, comb_ref[...], 0.0), axis=1, keepdims=True)
    gate = _dot(h, wg_ref[0, 0])
    up = _dot(h, wu_ref[0, 0])
    hid = (jax.nn.silu(gate) * up * w_e).astype(BF16)
    acc_ref[...] += _dot(hid, wd_ref[0, 0])

    @pl.when(e == N_EXPERTS - 1)
    def _norm():
        z = DN_ALPHA * x_ref[...] + ada_ref[0, 5:6, :] * acc_ref[...]
        out_ref[...] = _layer_norm(z, g_ref[...], b_ref[...])


def _moe(x, ada_rows, w_router, b_router, w_gate, w_up, w_down, ln_g, ln_b, batch_len, tm=1024):
    n, d = x.shape
    per_b = batch_len // tm
    wr = jnp.pad(w_router, ((0, 0), (0, LANES - N_EXPERTS)))
    br = b_router.reshape(N_EXPERTS, 1)
    return pl.pallas_call(
        functools.partial(_moe_kernel, tm=tm),
        out_shape=jax.ShapeDtypeStruct((n, d), F32),
        grid=(n // tm, N_EXPERTS),
        in_specs=[pl.BlockSpec((tm, d), lambda i, e: (i, 0)),
                  pl.BlockSpec((1, 6, d), lambda i, e: (i // per_b, 0, 0)),
                  pl.BlockSpec((d, LANES), lambda i, e: (0, 0)),
                  pl.BlockSpec((N_EXPERTS, 1), lambda i, e: (0, 0)),
                  pl.BlockSpec((1, 1, d, D_EXPERT), lambda i, e: (0, e, 0, 0)),
                  pl.BlockSpec((1, 1, d, D_EXPERT), lambda i, e: (0, e, 0, 0)),
                  pl.BlockSpec((1, 1, D_EXPERT, d), lambda i, e: (0, e, 0, 0)),
                  pl.BlockSpec((1, d), lambda i, e: (0, 0)),
                  pl.BlockSpec((1, d), lambda i, e: (0, 0))],
        out_specs=pl.BlockSpec((tm, d), lambda i, e: (i, 0)),
        scratch_shapes=[pltpu.VMEM((tm, d), BF16),
                        pltpu.VMEM((tm, LANES), F32),
                        pltpu.VMEM((tm, d), F32)],
        compiler_params=_params(2),
        name="moe",
    )(x, ada_rows, wr, br, w_gate[None].astype(BF16), w_up[None].astype(BF16), w_down[None].astype(BF16),
      ln_g.reshape(1, d), ln_b.reshape(1, d))


def kernel(x, c, w_in, b_forget, w_branch, w_merge_gate, w_out, w_ada, b_ada, ln1_g, ln1_b, ln2_g, ln2_b,
           w_router, b_router, w_exp_gate, w_exp_up, w_exp_down):
    bsz, seq, d = x.shape
    slopes = _alibi_slopes()
    ada_all = _ada(c, w_ada, b_ada).reshape(DEPTH, bsz, 6, d)
    for l in range(DEPTH):
        ada = ada_all[l]
        k, kmean, misc, qT, vT, qiT, miscT = _proj(x, ada, w_in[l])
        o_a = _dsa(qT, k, vT, qiT, misc, miscT, slopes[0])
        o_b = _dilated(qT, k, vT, slopes[1])
        o_c = _moba(qT, k, vT, kmean, slopes[2])
        o_d = _fox(qT, k, vT, misc, b_forget[l])
        x = _merge(x, ada, (o_a, o_b, o_c, o_d), w_merge_gate[l], w_branch[l], w_out[l], ln1_g[l], ln1_b[l])
        x = _moe(x.reshape(bsz * seq, d), ada, w_router, b_router, w_exp_gate[l], w_exp_up[l], w_exp_down[l],
                 ln2_g[l], ln2_b[l], seq).reshape(bsz, seq, d)
    return x
```

```python
import functools

import numpy as np
import jax
import jax.numpy as jnp
from jax import lax
from jax.experimental import pallas as pl
from jax.experimental.pallas import tpu as pltpu

D_MODEL = 1024
HEAD_DIM = 64
HEADS = 4
MIX_W = HEADS * HEAD_DIM
N_MIX = 4
IDX_HEADS = 8
IDX_DIM = 64
TOPK_KEYS = 256
MOBA_BLOCK = 256
MOBA_TOPK = 3
N_EXPERTS = 16
N_GROUPS = 4
D_EXPERT = 512
DEPTH = 2
DN_ALPHA = (2 * DEPTH) ** 0.25
LN_EPS = 1e-5
IN_COLS = 3 * N_MIX * MIX_W + IDX_HEADS * IDX_DIM + IDX_DIM + IDX_HEADS + HEADS

CH = 256
ACC_ROWS = HEAD_DIM + 16
LANES = 128
MISC_W = 128
WI_ROW = IDX_DIM
FL_ROW = IDX_DIM + IDX_HEADS
LOG2E = 1.4426950408889634
NEG = -1e30
QSCALE = HEAD_DIM ** -0.5 * LOG2E
INT_MIN = -(2 ** 31)
VMEM_LIMIT = 56 * 1024 * 1024

F32 = jnp.float32
BF16 = jnp.bfloat16
HI = lax.Precision.HIGHEST
NT = (((1,), (1,)), ((), ()))


def _alibi_slopes():
    n = 3 * HEADS
    s = 2.0 ** (-8.0 * np.arange(1, n + 1) / n)
    return s.reshape(HEADS, 3).T


def _dot(a, b):
    return jnp.dot(a, b, preferred_element_type=F32)


def _dot_hi(a, b):
    return jnp.dot(a, b, preferred_element_type=F32, precision=HI)


def _split3(x):
    x1 = x.astype(BF16)
    r1 = x - x1.astype(F32)
    x2 = r1.astype(BF16)
    x3 = (r1 - x2.astype(F32)).astype(BF16)
    return x1, x2, x3


def _params(n_axes):
    return pltpu.CompilerParams(dimension_semantics=("arbitrary",) * n_axes,
                                vmem_limit_bytes=VMEM_LIMIT)


def _layer_norm(z, g, b):
    mu = jnp.mean(z, axis=-1, keepdims=True)
    var = jnp.mean(jnp.square(z - mu), axis=-1, keepdims=True)
    return (z - mu) * lax.rsqrt(var + LN_EPS) * g + b


def _ada_kernel(c_ref, w_ref, b_ref, o_ref):
    o_ref[0] = _dot_hi(c_ref[...], w_ref[0]) + b_ref[0]


def _ada(c, w_ada, b_ada):
    depth, d, n = w_ada.shape
    bsz = c.shape[0]
    tn = D_MODEL
    return pl.pallas_call(
        _ada_kernel,
        out_shape=jax.ShapeDtypeStruct((depth, bsz, n), F32),
        grid=(depth, n // tn),
        in_specs=[pl.BlockSpec((bsz, d), lambda l, j: (0, 0)),
                  pl.BlockSpec((1, d, tn), lambda l, j: (l, 0, j)),
                  pl.BlockSpec((1, 1, tn), lambda l, j: (l, 0, j))],
        out_specs=pl.BlockSpec((1, bsz, tn), lambda l, j: (l, 0, j)),
        compiler_params=_params(2),
        name="ada",
    )(c, w_ada, b_ada.reshape(depth, 1, n))


def _proj_kernel(x_ref, ada_ref, wk_ref, wm_ref, wt_ref, wmt_ref,
                 k_ref, kmean_ref, misc_ref, qT_ref, vT_ref, qiT_ref, miscT_ref, *, tm):
    x = x_ref[0]
    sh = ada_ref[0, 0:1, :]
    sc = ada_ref[0, 1:2, :]
    h = (x * (1.0 + sc) + sh).astype(BF16)
    kf = _dot(h, wk_ref[...])
    k_ref[0] = kf.astype(BF16)
    for g in range(tm // MOBA_BLOCK):
        kmean_ref[0, 0, g:g + 1, :] = jnp.mean(kf[g * MOBA_BLOCK:(g + 1) * MOBA_BLOCK], axis=0, keepdims=True)
    misc_ref[0] = _dot(h, wm_ref[...])
    t = lax.dot_general(wt_ref[...], h, NT, preferred_element_type=F32)
    nq = N_MIX * MIX_W
    qT_ref[0] = (t[0:nq] * QSCALE).astype(BF16)
    for g in range(tm // CH):
        vT_ref[0, g] = t[nq:2 * nq, g * CH:(g + 1) * CH].astype(BF16)
    qiT_ref[0] = t[2 * nq:].astype(BF16)
    miscT_ref[0] = lax.dot_general(wmt_ref[...], h, NT, preferred_element_type=F32)


def _proj(x, ada, w_in, tm=512):
    bsz, seq, d = x.shape
    nq = N_MIX * MIX_W
    nqi = IDX_HEADS * IDX_DIM
    w = w_in.astype(BF16)
    wk = w[:, nq:2 * nq]
    wm = jnp.pad(w[:, 3 * nq + nqi:], ((0, 0), (0, MISC_W - (IN_COLS - 3 * nq - nqi))))
    wt = jnp.concatenate([w[:, 0:nq], w[:, 2 * nq:3 * nq], w[:, 3 * nq:3 * nq + nqi]], axis=1).T
    wmt = wm.T
    nt = seq // tm
    full = lambda b, i: (0, 0)
    outs = pl.pallas_call(
        functools.partial(_proj_kernel, tm=tm),
        out_shape=(jax.ShapeDtypeStruct((bsz, seq, nq), BF16),
                   jax.ShapeDtypeStruct((bsz, nt, tm // MOBA_BLOCK, nq), F32),
                   jax.ShapeDtypeStruct((bsz, seq, MISC_W), F32),
                   jax.ShapeDtypeStruct((bsz, nq, seq), BF16),
                   jax.ShapeDtypeStruct((bsz, seq // CH, nq, CH), BF16),
                   jax.ShapeDtypeStruct((bsz, nqi, seq), BF16),
                   jax.ShapeDtypeStruct((bsz, MISC_W, seq), F32)),
        grid=(bsz, nt),
        in_specs=[pl.BlockSpec((1, tm, d), lambda b, i: (b, i, 0)),
                  pl.BlockSpec((1, 6, d), lambda b, i: (b, 0, 0)),
                  pl.BlockSpec(wk.shape, full),
                  pl.BlockSpec(wm.shape, full),
                  pl.BlockSpec(wt.shape, full),
                  pl.BlockSpec(wmt.shape, full)],
        out_specs=(pl.BlockSpec((1, tm, nq), lambda b, i: (b, i, 0)),
                   pl.BlockSpec((1, 1, tm // MOBA_BLOCK, nq), lambda b, i: (b, i, 0, 0)),
                   pl.BlockSpec((1, tm, MISC_W), lambda b, i: (b, i, 0)),
                   pl.BlockSpec((1, nq, tm), lambda b, i: (b, 0, i)),
                   pl.BlockSpec((1, tm // CH, nq, CH), lambda b, i: (b, i, 0, 0)),
                   pl.BlockSpec((1, nqi, tm), lambda b, i: (b, 0, i)),
                   pl.BlockSpec((1, MISC_W, tm), lambda b, i: (b, 0, i))),
        compiler_params=_params(2),
        name="proj",
    )(x, ada, wk, wm, wt, wmt)
    k, kmean, misc, qT, vT, qiT, miscT = outs
    return k, kmean.reshape(bsz, seq // MOBA_BLOCK, nq), misc, qT, vT, qiT, miscT


def _mask_heads(qT_ref, qm_ref):
    q = qT_ref[0]
    rowh = lax.broadcasted_iota(jnp.int32, q.shape, 0) // HEAD_DIM
    for h in range(HEADS):
        qm_ref[h] = jnp.where(rowh == h, q, jnp.zeros_like(q))


def _init_state(m_ref, acc_ref):
    m_ref[...] = jnp.full(m_ref.shape, NEG, F32)
    acc_ref[...] = jnp.zeros(acc_ref.shape, F32)


def _attention(i, first, k_ref, vT_ref, st, logits_fn, chunk_ctx=None, diag_first=True):
    qm_ref, m_ref, acc_ref, s_ref, mx_ref, p_ref, alpha_ref = st
    n_chunks = i - first + 1
    ones = jnp.ones((ACC_ROWS - HEAD_DIM, CH), BF16)

    def chunk_of(n):
        n = jnp.clip(n, 0, n_chunks - 1)
        if diag_first:
            return jnp.where(n == 0, i, first + n - 1)
        return first + n

    def logits_head(h, slot, c, kc, ctx, diag, pen):
        x = logits_fn(h, c, _dot(kc, qm_ref[h]), ctx, diag, pen)
        s_ref[slot, h] = x
        mx_ref[slot, h] = jnp.max(x, axis=0, keepdims=True)

    def softmax_head(h, slot):
        m_old = m_ref[h]
        m_new = jnp.maximum(m_old, mx_ref[slot, h])
        alpha_ref[slot, h] = jnp.exp2(m_old - m_new)
        p_ref[slot, h] = jnp.exp2(s_ref[slot, h] - m_new).astype(BF16)
        m_ref[h] = m_new

    def pv_head(h, slot, vc):
        v1 = jnp.concatenate([vc[h * HEAD_DIM:(h + 1) * HEAD_DIM, :], ones], axis=0)
        acc_ref[h] = alpha_ref[slot, h] * acc_ref[h] + _dot(v1, p_ref[slot, h])

    def step(n, a, diag):
        c = chunk_of(n + 1)
        pen = jnp.where(n + 1 < n_chunks, 0.0, NEG)
        kc = k_ref[0, pl.ds(pl.multiple_of(c * CH, CH), CH), :]
        vc = vT_ref[0, chunk_of(n - 1)]
        ctx = chunk_ctx(c) if chunk_ctx is not None else None
        for h in range(HEADS):
            softmax_head(h, a)
        for h in range(HEADS):
            pv_head(h, 1 - a, vc)
        for h in range(HEADS):
            logits_head(h, 1 - a, c, kc, ctx, diag, pen)

    p_ref[1] = jnp.zeros(p_ref.shape[1:], BF16)
    alpha_ref[1] = jnp.ones(alpha_ref.shape[1:], F32)
    c0 = jnp.asarray(i if diag_first else first, jnp.int32)
    k0 = k_ref[0, pl.ds(pl.multiple_of(c0 * CH, CH), CH), :]
    ctx0 = chunk_ctx(c0) if chunk_ctx is not None else None
    for h in range(HEADS):
        logits_head(h, 0, c0, k0, ctx0, True if diag_first else None, 0.0)
    later = False if diag_first else None

    def pair(t, carry):
        step(2 * t, 0, later)
        step(2 * t + 1, 1, later)
        return carry

    n_trips = (n_chunks + 1) // 2
    lax.fori_loop(0, n_trips, pair, 0)
    v_last = vT_ref[0, chunk_of(2 * n_trips - 1)]
    for h in range(HEADS):
        pv_head(h, 1, v_last)


def _finish(o_ref, acc_ref):
    parts = [acc_ref[h, 0:HEAD_DIM, :] / acc_ref[h, HEAD_DIM:HEAD_DIM + 1, :] for h in range(HEADS)]
    o_ref[0] = jnp.concatenate(parts, axis=0).T.astype(BF16)


def _lanes(tile, tq):
    return tile if tq == LANES else jnp.concatenate([tile] * (tq // LANES), axis=1)


def _causal_neg(tq):
    s_i = lax.broadcasted_iota(jnp.int32, (CH, tq), 0)
    t_i = lax.broadcasted_iota(jnp.int32, (CH, tq), 1)
    return jnp.where(s_i <= t_i, 0.0, NEG).astype(F32)


def _key_pos_bias(bias_ref, slopes):
    s_i = lax.broadcasted_iota(jnp.int32, (CH, LANES), 0).astype(F32)
    for h in range(HEADS):
        bias_ref[h] = s_i * float(slopes[h] * LOG2E)


def _attn_specs(mixer, seq):
    return [pl.BlockSpec((1, MIX_W, CH), lambda b, i: (b, mixer, i)),
            pl.BlockSpec((1, seq, MIX_W), lambda b, i: (b, 0, mixer)),
            pl.BlockSpec((1, seq // CH, MIX_W, CH), lambda b, i: (b, 0, mixer, 0))]


def _attn_scratch():
    return [pltpu.VMEM((HEADS, MIX_W, CH), BF16),
            pltpu.VMEM((HEADS, 1, CH), F32),
            pltpu.VMEM((HEADS, ACC_ROWS, CH), F32),
            pltpu.VMEM((2, HEADS, CH, CH), F32),
            pltpu.VMEM((2, HEADS, 1, CH), F32),
            pltpu.VMEM((2, HEADS, CH, CH), BF16),
            pltpu.VMEM((2, HEADS, 1, CH), F32)]


def _fox_kernel(qT_ref, k_ref, vT_ref, misc_ref, bf_ref, o_ref,
                cum_ref, *st, n_chunks):
    qm_ref, m_ref, acc_ref = st[:3]
    i = pl.program_id(1)

    @pl.when(i == 0)
    def _cumulative_gates():
        rr = lax.broadcasted_iota(jnp.int32, (CH, CH), 0)
        cc = lax.broadcasted_iota(jnp.int32, (CH, CH), 1)
        tri = jnp.where(cc <= rr, 1.0, 0.0).astype(BF16)
        erow = lax.broadcasted_iota(jnp.int32, (LANES, LANES), 0)

        def body(blk, carry):
            off = pl.multiple_of(blk * CH, CH)
            z = misc_ref[0, pl.ds(off, CH), :] + bf_ref[...]
            ls = jnp.minimum(z, 0.0) - jnp.log1p(jnp.exp(-jnp.abs(z)))
            csum = sum(_dot(tri, part) for part in _split3(ls))
            parts = _split3(csum)
            new = []
            for h in range(HEADS):
                sel = jnp.where(erow == FL_ROW + h, 1.0, 0.0).astype(BF16)
                cum = sum(_dot(part, sel) for part in parts) + carry[h]
                cum_ref[h, pl.ds(off, CH), :] = cum * LOG2E
                new.append(cum[CH - 1:CH, :])
            return tuple(new)

        lax.fori_loop(0, n_chunks, body, tuple(jnp.zeros((1, LANES), F32) for _ in range(HEADS)))

    _mask_heads(qT_ref, qm_ref)
    _init_state(m_ref, acc_ref)

    def logits(h, c, qk, _, diag, pen):
        off = pl.multiple_of(c * CH, CH)
        cum = cum_ref[h, pl.ds(off, CH), :]
        if diag:
            return qk - _lanes(cum, CH) + _causal_neg(CH)
        return qk - _lanes(cum - pen, CH)

    _attention(i, 0, k_ref, vT_ref, st, logits)
    _finish(o_ref, acc_ref)


def _fox(qT, k, vT, misc, b_forget_l):
    bsz, seq, _ = k.shape
    n_chunks = seq // CH
    bf = jnp.zeros((1, MISC_W), F32).at[0, FL_ROW:FL_ROW + HEADS].set(b_forget_l)
    return pl.pallas_call(
        functools.partial(_fox_kernel, n_chunks=n_chunks),
        out_shape=jax.ShapeDtypeStruct((bsz, seq, MIX_W), BF16),
        grid=(bsz, n_chunks),
        in_specs=_attn_specs(3, seq) + [
            pl.BlockSpec((1, seq, MISC_W), lambda b, i: (b, 0, 0)),
            pl.BlockSpec((1, MISC_W), lambda b, i: (0, 0))],
        out_specs=pl.BlockSpec((1, CH, MIX_W), lambda b, i: (b, i, 0)),
        scratch_shapes=[pltpu.VMEM((HEADS, seq, LANES), F32)] + _attn_scratch(),
        compiler_params=_params(2),
        name="fox",
    )(qT, k, vT, misc, bf)


def _moba_kernel(qT_ref, k_ref, vT_ref, kmean_ref, o_ref,
                 bias_ref, rowadd_ref, *st, slopes, n_blocks):
    qm_ref, m_ref, acc_ref = st[:3]
    i = pl.program_id(1)
    _mask_heads(qT_ref, qm_ref)
    _init_state(m_ref, acc_ref)

    @pl.when(jnp.logical_and(pl.program_id(0) == 0, i == 0))
    def _bias_table():
        s_f = lax.broadcasted_iota(jnp.int32, (CH, CH), 0).astype(F32)
        causal = _causal_neg(CH)
        for h in range(HEADS):
            pos = s_f * float(slopes[h] * LOG2E)
            bias_ref[0, h] = pos
            bias_ref[1, h] = pos + causal

    n_i = lax.broadcasted_iota(jnp.int32, (n_blocks, CH), 0)
    n_f = n_i.astype(F32)
    past = n_i < i
    for h in range(HEADS):
        gate = _dot_hi(kmean_ref[0], qm_ref[h].astype(F32))
        gate = jnp.where(past, gate, -jnp.inf)
        chosen = jnp.zeros((n_blocks, CH), F32)
        for _ in range(MOBA_TOPK):
            mx = jnp.max(gate, axis=0, keepdims=True)
            first = jnp.min(jnp.where(gate == mx, n_f, float(n_blocks)), axis=0, keepdims=True)
            pick = n_f == first
            chosen = jnp.where(pick, 1.0, chosen)
            gate = jnp.where(pick, -jnp.inf, gate)
        chosen = jnp.where(past, chosen, 0.0)
        blk_shift = n_f * float(slopes[h] * LOG2E * MOBA_BLOCK)
        rowadd_ref[h] = jnp.where(jnp.logical_or(chosen > 0.5, n_i == i), blk_shift, NEG)

    def logits(h, c, qk, _, diag, pen):
        row = rowadd_ref[h, pl.ds(c, 1), :]
        if diag:
            return qk + bias_ref[1, h] + row
        return qk + bias_ref[0, h] + (row + pen)

    _attention(i, 0, k_ref, vT_ref, st, logits)
    _finish(o_ref, acc_ref)


def _moba(qT, k, vT, kmean, slopes):
    bsz, seq, _ = k.shape
    n_blocks = seq // MOBA_BLOCK
    return pl.pallas_call(
        functools.partial(_moba_kernel, slopes=tuple(float(s) for s in slopes), n_blocks=n_blocks),
        out_shape=jax.ShapeDtypeStruct((bsz, seq, MIX_W), BF16),
        grid=(bsz, seq // CH),
        in_specs=_attn_specs(2, seq) + [
            pl.BlockSpec((1, n_blocks, MIX_W), lambda b, i: (b, 0, 2))],
        out_specs=pl.BlockSpec((1, CH, MIX_W), lambda b, i: (b, i, 0)),
        scratch_shapes=[pltpu.VMEM((2, HEADS, CH, CH), F32),
                        pltpu.VMEM((HEADS, n_blocks, CH), F32)] + _attn_scratch(),
        compiler_params=_params(2),
        name="moba",
    )(qT, k, vT, kmean)


DIL_SPAN = 2048 // CH + 1


def _dilated_kernel(qT_ref, k_ref, vT_ref, o_ref, table_ref, *st, slopes):
    qm_ref, m_ref, acc_ref = st[:3]
    b = pl.program_id(0)
    i = pl.program_id(1)

    @pl.when(jnp.logical_and(b == 0, i == 0))
    def _bias_table():
        s_i = lax.broadcasted_iota(jnp.int32, (CH, CH), 0)
        t_i = lax.broadcasted_iota(jnp.int32, (CH, CH), 1)
        for j in range(DIL_SPAN):
            d = t_i - s_i + j * CH
            ok = d >= 0
            mult = (jnp.where(jnp.logical_and(ok, d <= 128), 1.0, 0.0)
                    + jnp.where(jnp.logical_and(ok, jnp.logical_and(d <= 512, (d & 3) == 0)), 1.0, 0.0)
                    + jnp.where(jnp.logical_and(ok, jnp.logical_and(d <= 2048, (d & 15) == 0)), 1.0, 0.0))
            logm = jnp.where(mult > 0.5, jnp.log2(jnp.maximum(mult, 1.0)), NEG)
            df = d.astype(F32)
            for h in range(HEADS):
                table_ref[h, j] = logm - df * float(slopes[h] * LOG2E)
        for h in range(HEADS):
            table_ref[h, DIL_SPAN] = jnp.full((CH, CH), NEG, F32)

    _mask_heads(qT_ref, qm_ref)
    _init_state(m_ref, acc_ref)

    def logits(h, c, qk, _, diag, pen):
        if diag:
            return qk + table_ref[h, 0]
        return qk + table_ref[h, jnp.where(pen < 0.0, DIL_SPAN, i - c)]

    _attention(i, jnp.maximum(i - (DIL_SPAN - 1), 0), k_ref, vT_ref, st, logits)
    _finish(o_ref, acc_ref)


def _dilated(qT, k, vT, slopes):
    bsz, seq, _ = k.shape
    return pl.pallas_call(
        functools.partial(_dilated_kernel, slopes=tuple(float(s) for s in slopes)),
        out_shape=jax.ShapeDtypeStruct((bsz, seq, MIX_W), BF16),
        grid=(bsz, seq // CH),
        in_specs=_attn_specs(1, seq),
        out_specs=pl.BlockSpec((1, CH, MIX_W), lambda b, i: (b, i, 0)),
        scratch_shapes=[pltpu.VMEM((HEADS, DIL_SPAN + 1, CH, CH), F32)] + _attn_scratch(),
        compiler_params=_params(2),
        name="dilated",
    )(qT, k, vT)


def _dsa_kernel(qT_ref, k_ref, vT_ref, qiT_ref, misc_ref, miscT_ref, o_ref,
                bias_ref, qi_ref, key_ref, half_ref, seen_ref, *st, slopes):
    qm_ref, m_ref, acc_ref = st[:3]
    i = pl.program_id(1)
    tq = CH
    _mask_heads(qT_ref, qm_ref)
    _init_state(m_ref, acc_ref)
    _key_pos_bias(bias_ref, slopes)

    qi_all = qiT_ref[0]
    zpad = jnp.zeros((MISC_W - IDX_DIM, tq), BF16)
    for h in range(IDX_HEADS):
        qi_ref[h] = jnp.concatenate([qi_all[h * IDX_DIM:(h + 1) * IDX_DIM], zpad], axis=0)
    w_rows = miscT_ref[0, WI_ROW:WI_ROW + IDX_HEADS, :] * float(IDX_HEADS ** -0.5 * IDX_DIM ** -0.5)

    s_i = lax.broadcasted_iota(jnp.int32, (CH, tq), 0)
    t_i = lax.broadcasted_iota(jnp.int32, (CH, tq), 1)

    def score_chunk(c, diag):
        off = pl.multiple_of(c * CH, CH)
        ki = misc_ref[0, pl.ds(off, CH), :].astype(BF16)
        score = jnp.zeros((CH, tq), F32)
        for h in range(IDX_HEADS):
            rel = jnp.maximum(_dot(ki, qi_ref[h]), 0.0)
            score = score + rel * w_rows[h:h + 1, :]
        bits = lax.bitcast_convert_type(score, jnp.int32)
        key = bits ^ ((bits >> 31) & 0x7FFFFFFF)
        key = jnp.where(key == -1, 0, key)
        if diag:
            key = jnp.where(s_i <= t_i, key, INT_MIN)
        key_ref[pl.ds(off, CH), :] = key
        half_ref[pl.ds(off, CH), :] = (key >> 16).astype(jnp.int16)

    score_chunk(i, True)
    lax.fori_loop(0, i, lambda c, _: (score_chunk(c, False), 0)[1], 0)

    def count(ref, pred):
        def body(c, cnt):
            off = pl.multiple_of(c * CH, CH)
            hit = jnp.where(pred(ref[pl.ds(off, CH), :]), 1, 0)
            return cnt + jnp.sum(hit.reshape(CH // 8, 8, tq), axis=0)
        cnt8 = lax.fori_loop(0, i + 1, body, jnp.zeros((8, tq), jnp.int32))
        return jnp.sum(cnt8, axis=0, keepdims=True)

    def count_half(cand):
        cand16 = cand.astype(jnp.int16)
        rows = 16

        def body(c, cnt):
            off = pl.multiple_of(c * CH, CH)
            hit = jnp.where(half_ref[pl.ds(off, CH), :] >= cand16, jnp.int16(1), jnp.int16(0))
            hit = hit.reshape(CH // rows, rows, tq)
            for r in range(CH // rows):
                cnt = cnt + hit[r]
            return cnt

        cnt16 = lax.fori_loop(0, i + 1, body, jnp.zeros((rows, tq), jnp.int16))
        return jnp.sum(cnt16.astype(jnp.int32), axis=0, keepdims=True)

    def search_half(need, n_all):
        def bit_step(n, state):
            v, n_ge = state
            cand = v + jnp.left_shift(jnp.int32(1), 15 - n)
            cnt = count_half(cand)
            ok = cnt >= need
            return jnp.where(ok, cand, v), jnp.where(ok, cnt, n_ge)

        return lax.fori_loop(0, 16, bit_step, (jnp.full((1, tq), -32768, jnp.int32), n_all))

    n_all = jnp.zeros((1, tq), jnp.int32) + (i + 1) * CH
    hi, n_ge_hi = search_half(TOPK_KEYS, n_all)
    n_gt_hi = jnp.where(hi == 32767, 0, count_half(jnp.minimum(hi + 1, 32767)))
    need_lo = TOPK_KEYS - n_gt_hi

    def lower_halves(c, _):
        off = pl.multiple_of(c * CH, CH)
        key = key_ref[pl.ds(off, CH), :]
        lo = (key & 0xFFFF) - 32768
        half_ref[pl.ds(off, CH), :] = jnp.where((key >> 16) == hi, lo, -32768).astype(jnp.int16)
        return 0

    lax.fori_loop(0, i + 1, lower_halves, 0)
    lo, n_ge_lo = search_half(need_lo, n_ge_hi - n_gt_hi)
    thr = hi * 65536 + (lo + 32768)
    n_ge = n_gt_hi + n_ge_lo
    tied = jnp.logical_and(n_ge > TOPK_KEYS, thr != INT_MIN)
    any_tied = jnp.max(jnp.where(tied, 1.0, 0.0)) > 0.5

    def logits(h, c, qk, neg, diag, pen):
        pos = bias_ref[h] + (c.astype(F32) * float(slopes[h] * LOG2E * CH) + pen)
        return qk + _lanes(pos, tq) + neg

    @pl.when(jnp.logical_not(any_tied))
    def _no_ties():
        thr_lo = jnp.where(thr == INT_MIN, INT_MIN + 1, thr)

        def selection(c):
            off = pl.multiple_of(c * CH, CH)
            return jnp.where(key_ref[pl.ds(off, CH), :] >= thr_lo, 0.0, NEG)

        _attention(i, 0, k_ref, vT_ref, st, logits, selection, diag_first=False)

    @pl.when(any_tied)
    def _ties():
        n_gt = count(key_ref, lambda key: key > thr)
        n_take = jnp.where(thr == INT_MIN, 0, TOPK_KEYS - n_gt).astype(F32)
        rr = lax.broadcasted_iota(jnp.int32, (CH, CH), 0)
        cc = lax.broadcasted_iota(jnp.int32, (CH, CH), 1)
        tri = jnp.where(cc <= rr, 1.0, 0.0).astype(BF16)
        seen_ref[...] = jnp.zeros(seen_ref.shape, F32)

        def selection(c):
            off = pl.multiple_of(c * CH, CH)
            key = key_ref[pl.ds(off, CH), :]
            eq = key == thr
            rank = _dot(tri, jnp.where(eq, 1.0, 0.0).astype(BF16)) + seen_ref[...]
            seen_ref[...] = rank[CH - 1:CH, :]
            take = jnp.logical_or(key > thr, jnp.logical_and(eq, rank <= n_take))
            return jnp.where(take, 0.0, NEG)

        _attention(i, 0, k_ref, vT_ref, st, logits, selection, diag_first=False)

    _finish(o_ref, acc_ref)


def _dsa(qT, k, vT, qiT, misc, miscT, slopes):
    bsz, seq, _ = k.shape
    nqi = IDX_HEADS * IDX_DIM
    return pl.pallas_call(
        functools.partial(_dsa_kernel, slopes=tuple(float(s) for s in slopes)),
        out_shape=jax.ShapeDtypeStruct((bsz, seq, MIX_W), BF16),
        grid=(bsz, seq // CH),
        in_specs=_attn_specs(0, seq) + [
            pl.BlockSpec((1, nqi, CH), lambda b, i: (b, 0, i)),
            pl.BlockSpec((1, seq, MISC_W), lambda b, i: (b, 0, 0)),
            pl.BlockSpec((1, MISC_W, CH), lambda b, i: (b, 0, i))],
        out_specs=pl.BlockSpec((1, CH, MIX_W), lambda b, i: (b, i, 0)),
        scratch_shapes=[pltpu.VMEM((HEADS, CH, LANES), F32),
                        pltpu.VMEM((IDX_HEADS, MISC_W, CH), BF16),
                        pltpu.VMEM((seq, CH), jnp.int32),
                        pltpu.VMEM((seq, CH), jnp.int16),
                        pltpu.VMEM((1, CH), F32)] + _attn_scratch(),
        compiler_params=_params(2),
        name="dsa",
    )(qT, k, vT, qiT, misc, miscT)


def _merge_kernel(x_ref, ada_ref, oa_ref, ob_ref, oc_ref, od_ref, wmg_ref, wbr_ref, wout_ref,
                  g_ref, b_ref, out_ref):
    x = x_ref[0]
    sh = ada_ref[0, 0:1, :]
    sc = ada_ref[0, 1:2, :]
    g1 = ada_ref[0, 2:3, :]
    h = (x * (1.0 + sc) + sh).astype(BF16)
    mixed = None
    for m, o_ref in enumerate((oa_ref, ob_ref, oc_ref, od_ref)):
        gate = jax.nn.sigmoid(_dot(h, wmg_ref[m]))
        term = gate * _dot(o_ref[0], wbr_ref[m])
        mixed = term if mixed is None else mixed + term
    y = _dot(mixed.astype(BF16), wout_ref[...])
    out_ref[0] = _layer_norm(DN_ALPHA * x + g1 * y, g_ref[...], b_ref[...])


def _merge(x, ada, outs, w_mg, w_br, w_out, ln_g, ln_b, tm=512):
    bsz, seq, d = x.shape
    o_spec = pl.BlockSpec((1, tm, MIX_W), lambda b, i: (b, i, 0))
    return pl.pallas_call(
        _merge_kernel,
        out_shape=jax.ShapeDtypeStruct((bsz, seq, d), F32),
        grid=(bsz, seq // tm),
        in_specs=[pl.BlockSpec((1, tm, d), lambda b, i: (b, i, 0)),
                  pl.BlockSpec((1, 6, d), lambda b, i: (b, 0, 0)),
                  o_spec, o_spec, o_spec, o_spec,
                  pl.BlockSpec((N_MIX, d, d), lambda b, i: (0, 0, 0)),
                  pl.BlockSpec((N_MIX, MIX_W, d), lambda b, i: (0, 0, 0)),
                  pl.BlockSpec((d, d), lambda b, i: (0, 0)),
                  pl.BlockSpec((1, d), lambda b, i: (0, 0)),
                  pl.BlockSpec((1, d), lambda b, i: (0, 0))],
        out_specs=pl.BlockSpec((1, tm, d), lambda b, i: (b, i, 0)),
        compiler_params=_params(2),
        name="merge",
    )(x, ada, *outs, w_mg.astype(BF16), w_br.astype(BF16), w_out.astype(BF16),
      ln_g.reshape(1, d), ln_b.reshape(1, d))


def _route(logits_t, bias_col):
    per_group = N_EXPERTS // N_GROUPS
    scores = jax.nn.sigmoid(logits_t)
    biased = scores + bias_col
    s_rows = [scores[e:e + 1] for e in range(N_EXPERTS)]
    b_rows = [biased[e:e + 1] for e in range(N_EXPERTS)]
    best_g = None
    for g in range(N_GROUPS):
        r = b_rows[g * per_group:(g + 1) * per_group]
        gs = None
        for a in range(per_group):
            for b in range(a + 1, per_group):
                pair = r[a] + r[b]
                gs = pair if gs is None else jnp.maximum(gs, pair)
        if best_g is None:
            best_g, best_v = jnp.zeros_like(gs, dtype=jnp.int32), gs
        else:
            better = gs > best_v
            best_g = jnp.where(better, g, best_g)
            best_v = jnp.maximum(best_v, gs)
    masked = [jnp.where(best_g == e // per_group, b_rows[e], -jnp.inf) for e in range(N_EXPERTS)]

    def argmax_first(rows):
        idx, val = jnp.zeros_like(best_g), rows[0]
        for e in range(1, N_EXPERTS):
            better = rows[e] > val
            idx = jnp.where(better, e, idx)
            val = jnp.maximum(val, rows[e])
        return idx

    e1 = argmax_first(masked)
    e2 = argmax_first([jnp.where(e1 == e, -jnp.inf, masked[e]) for e in range(N_EXPERTS)])
    s1 = sum(jnp.where(e1 == e, s_rows[e], 0.0) for e in range(N_EXPERTS))
    s2 = sum(jnp.where(e2 == e, s_rows[e], 0.0) for e in range(N_EXPERTS))
    tot = s1 + s2
    w1, w2 = s1 / tot, s2 / tot
    rows = [jnp.where(e1 == e, w1, 0.0) + jnp.where(e2 == e, w2, 0.0) for e in range(N_EXPERTS)]
    return jnp.concatenate(rows, axis=0)


def _moe_kernel(x_ref, ada_ref, wr_ref, br_ref, wg_ref, wu_ref, wd_ref, g_ref, b_ref, out_ref,
                h_ref, comb_ref, acc_ref, *, tm):
    e = pl.program_id(1)

    @pl.when(e == 0)
    def _router():
        x = x_ref[...]
        h2 = x * (1.0 + ada_ref[0, 4:5, :]) + ada_ref[0, 3:4, :]
        h_ref[...] = h2.astype(BF16)
        logits = _dot_hi(h2, wr_ref[...])
        logits_t = logits.T[0:N_EXPERTS]
        comb_t = _route(logits_t, br_ref[...])
        pad = jnp.zeros((LANES - N_EXPERTS, tm), F32)
        comb_ref[...] = jnp.concatenate([comb_t, pad], axis=0).T
        acc_ref[...] = jnp.zeros(acc_ref.shape, F32)

    h = h_ref[...]
    lane = lax.broadcasted_iota(jnp.int32, (tm, LANES), 1)
    w_e = jnp.sum(jnp.where(lane == e, comb_ref[...], 0.0), axis=1, keepdims=True)
    gate = _dot(h, wg_ref[0, 0])
    up = _dot(h, wu_ref[0, 0])
    hid = (jax.nn.silu(gate) * up * w_e).astype(BF16)
    acc_ref[...] += _dot(hid, wd_ref[0, 0])

    @pl.when(e == N_EXPERTS - 1)
    def _norm():
        z = DN_ALPHA * x_ref[...] + ada_ref[0, 5:6, :] * acc_ref[...]
        out_ref[...] = _layer_norm(z, g_ref[...], b_ref[...])


def _moe(x, ada_rows, w_router, b_router, w_gate, w_up, w_down, ln_g, ln_b, batch_len, tm=1024):
    n, d = x.shape
    per_b = batch_len // tm
    wr = jnp.pad(w_router, ((0, 0), (0, LANES - N_EXPERTS)))
    br = b_router.reshape(N_EXPERTS, 1)
    return pl.pallas_call(
        functools.partial(_moe_kernel, tm=tm),
        out_shape=jax.ShapeDtypeStruct((n, d), F32),
        grid=(n // tm, N_EXPERTS),
        in_specs=[pl.BlockSpec((tm, d), lambda i, e: (i, 0)),
                  pl.BlockSpec((1, 6, d), lambda i, e: (i // per_b, 0, 0)),
                  pl.BlockSpec((d, LANES), lambda i, e: (0, 0)),
                  pl.BlockSpec((N_EXPERTS, 1), lambda i, e: (0, 0)),
                  pl.BlockSpec((1, 1, d, D_EXPERT), lambda i, e: (0, e, 0, 0)),
                  pl.BlockSpec((1, 1, d, D_EXPERT), lambda i, e: (0, e, 0, 0)),
                  pl.BlockSpec((1, 1, D_EXPERT, d), lambda i, e: (0, e, 0, 0)),
                  pl.BlockSpec((1, d), lambda i, e: (0, 0)),
                  pl.BlockSpec((1, d), lambda i, e: (0, 0))],
        out_specs=pl.BlockSpec((tm, d), lambda i, e: (i, 0)),
        scratch_shapes=[pltpu.VMEM((tm, d), BF16),
                        pltpu.VMEM((tm, LANES), F32),
                        pltpu.VMEM((tm, d), F32)],
        compiler_params=_params(2),
        name="moe",
    )(x, ada_rows, wr, br, w_gate[None].astype(BF16), w_up[None].astype(BF16), w_down[None].astype(BF16),
      ln_g.reshape(1, d), ln_b.reshape(1, d))


def kernel(x, c, w_in, b_forget, w_branch, w_merge_gate, w_out, w_ada, b_ada, ln1_g, ln1_b, ln2_g, ln2_b,
           w_router, b_router, w_exp_gate, w_exp_up, w_exp_down):
    bsz, seq, d = x.shape
    slopes = _alibi_slopes()
    ada_all = _ada(c, w_ada, b_ada).reshape(DEPTH, bsz, 6, d)
    for l in range(DEPTH):
        ada = ada_all[l]
        k, kmean, misc, qT, vT, qiT, miscT = _proj(x, ada, w_in[l])
        o_a = _dsa(qT, k, vT, qiT, misc, miscT, slopes[0])
        o_b = _dilated(qT, k, vT, slopes[1])
        o_c = _moba(qT, k, vT, kmean, slopes[2])
        o_d = _fox(qT, k, vT, misc, b_forget[l])
        x = _merge(x, ada, (o_a, o_b, o_c, o_d), w_merge_gate[l], w_branch[l], w_out[l], ln1_g[l], ln1_b[l])
        x = _moe(x.reshape(bsz * seq, d), ada, w_router, b_router, w_exp_gate[l], w_exp_up[l], w_exp_down[l],
                 ln2_g[l], ln2_b[l], seq).reshape(bsz, seq, d)
    return x
```

```python
import functools

import numpy as np
import jax
import jax.numpy as jnp
from jax import lax
from jax.experimental import pallas as pl
from jax.experimental.pallas import tpu as pltpu

D_MODEL = 1024
HEAD_DIM = 64
HEADS = 4
MIX_W = HEADS * HEAD_DIM
N_MIX = 4
IDX_HEADS = 8
IDX_DIM = 64
TOPK_KEYS = 256
MOBA_BLOCK = 256
MOBA_TOPK = 3
N_EXPERTS = 16
N_GROUPS = 4
D_EXPERT = 512
DEPTH = 2
DN_ALPHA = (2 * DEPTH) ** 0.25
LN_EPS = 1e-5
IN_COLS = 3 * N_MIX * MIX_W + IDX_HEADS * IDX_DIM + IDX_DIM + IDX_HEADS + HEADS

CH = 256
ACC_ROWS = HEAD_DIM + 16
MOE_SUB = 128
LANES = 128
MISC_W = 128
WI_ROW = IDX_DIM
FL_ROW = IDX_DIM + IDX_HEADS
LOG2E = 1.4426950408889634
NEG = -1e30
QSCALE = HEAD_DIM ** -0.5 * LOG2E
INT_MIN = -(2 ** 31)
VMEM_LIMIT = 56 * 1024 * 1024

F32 = jnp.float32
BF16 = jnp.bfloat16
HI = lax.Precision.HIGHEST
NT = (((1,), (1,)), ((), ()))


def _alibi_slopes():
    n = 3 * HEADS
    s = 2.0 ** (-8.0 * np.arange(1, n + 1) / n)
    return s.reshape(HEADS, 3).T


def _dot(a, b):
    return jnp.dot(a, b, preferred_element_type=F32)


def _dot_hi(a, b):
    return jnp.dot(a, b, preferred_element_type=F32, precision=HI)


def _split3(x):
    x1 = x.astype(BF16)
    r1 = x - x1.astype(F32)
    x2 = r1.astype(BF16)
    x3 = (r1 - x2.astype(F32)).astype(BF16)
    return x1, x2, x3


def _params(n_axes):
    return pltpu.CompilerParams(dimension_semantics=("arbitrary",) * n_axes,
                                vmem_limit_bytes=VMEM_LIMIT)


def _layer_norm(z, g, b):
    mu = jnp.mean(z, axis=-1, keepdims=True)
    var = jnp.mean(jnp.square(z - mu), axis=-1, keepdims=True)
    return (z - mu) * lax.rsqrt(var + LN_EPS) * g + b


def _ada_kernel(c_ref, w_ref, b_ref, o_ref):
    o_ref[0] = _dot_hi(c_ref[...], w_ref[0]) + b_ref[0]


def _ada(c, w_ada, b_ada):
    depth, d, n = w_ada.shape
    bsz = c.shape[0]
    tn = D_MODEL
    return pl.pallas_call(
        _ada_kernel,
        out_shape=jax.ShapeDtypeStruct((depth, bsz, n), F32),
        grid=(depth, n // tn),
        in_specs=[pl.BlockSpec((bsz, d), lambda l, j: (0, 0)),
                  pl.BlockSpec((1, d, tn), lambda l, j: (l, 0, j)),
                  pl.BlockSpec((1, 1, tn), lambda l, j: (l, 0, j))],
        out_specs=pl.BlockSpec((1, bsz, tn), lambda l, j: (l, 0, j)),
        compiler_params=_params(2),
        name="ada",
    )(c, w_ada, b_ada.reshape(depth, 1, n))


def _proj_kernel(x_ref, ada_ref, wk_ref, wm_ref, wt_ref, wmt_ref,
                 k_ref, kmean_ref, misc_ref, qT_ref, vT_ref, qiT_ref, miscT_ref, *, tm):
    x = x_ref[0]
    sh = ada_ref[0, 0:1, :]
    sc = ada_ref[0, 1:2, :]
    h = (x * (1.0 + sc) + sh).astype(BF16)
    kf = _dot(h, wk_ref[...])
    k_ref[0] = kf.astype(BF16)
    for g in range(tm // MOBA_BLOCK):
        kmean_ref[0, 0, g:g + 1, :] = jnp.mean(kf[g * MOBA_BLOCK:(g + 1) * MOBA_BLOCK], axis=0, keepdims=True)
    misc_ref[0] = _dot(h, wm_ref[...])
    t = lax.dot_general(wt_ref[...], h, NT, preferred_element_type=F32)
    nq = N_MIX * MIX_W
    qT_ref[0] = (t[0:nq] * QSCALE).astype(BF16)
    for g in range(tm // CH):
        vT_ref[0, g] = t[nq:2 * nq, g * CH:(g + 1) * CH].astype(BF16)
    qiT_ref[0] = t[2 * nq:].astype(BF16)
    miscT_ref[0] = lax.dot_general(wmt_ref[...], h, NT, preferred_element_type=F32)


def _proj(x, ada, w_in, tm=512):
    bsz, seq, d = x.shape
    nq = N_MIX * MIX_W
    nqi = IDX_HEADS * IDX_DIM
    w = w_in.astype(BF16)
    wk = w[:, nq:2 * nq]
    wm = jnp.pad(w[:, 3 * nq + nqi:], ((0, 0), (0, MISC_W - (IN_COLS - 3 * nq - nqi))))
    wt = jnp.concatenate([w[:, 0:nq], w[:, 2 * nq:3 * nq], w[:, 3 * nq:3 * nq + nqi]], axis=1).T
    wmt = wm.T
    nt = seq // tm
    full = lambda b, i: (0, 0)
    outs = pl.pallas_call(
        functools.partial(_proj_kernel, tm=tm),
        out_shape=(jax.ShapeDtypeStruct((bsz, seq, nq), BF16),
                   jax.ShapeDtypeStruct((bsz, nt, tm // MOBA_BLOCK, nq), F32),
                   jax.ShapeDtypeStruct((bsz, seq, MISC_W), F32),
                   jax.ShapeDtypeStruct((bsz, nq, seq), BF16),
                   jax.ShapeDtypeStruct((bsz, seq // CH, nq, CH), BF16),
                   jax.ShapeDtypeStruct((bsz, nqi, seq), BF16),
                   jax.ShapeDtypeStruct((bsz, MISC_W, seq), F32)),
        grid=(bsz, nt),
        in_specs=[pl.BlockSpec((1, tm, d), lambda b, i: (b, i, 0)),
                  pl.BlockSpec((1, 6, d), lambda b, i: (b, 0, 0)),
                  pl.BlockSpec(wk.shape, full),
                  pl.BlockSpec(wm.shape, full),
                  pl.BlockSpec(wt.shape, full),
                  pl.BlockSpec(wmt.shape, full)],
        out_specs=(pl.BlockSpec((1, tm, nq), lambda b, i: (b, i, 0)),
                   pl.BlockSpec((1, 1, tm // MOBA_BLOCK, nq), lambda b, i: (b, i, 0, 0)),
                   pl.BlockSpec((1, tm, MISC_W), lambda b, i: (b, i, 0)),
                   pl.BlockSpec((1, nq, tm), lambda b, i: (b, 0, i)),
                   pl.BlockSpec((1, tm // CH, nq, CH), lambda b, i: (b, i, 0, 0)),
                   pl.BlockSpec((1, nqi, tm), lambda b, i: (b, 0, i)),
                   pl.BlockSpec((1, MISC_W, tm), lambda b, i: (b, 0, i))),
        compiler_params=_params(2),
        name="proj",
    )(x, ada, wk, wm, wt, wmt)
    k, kmean, misc, qT, vT, qiT, miscT = outs
    return k, kmean.reshape(bsz, seq // MOBA_BLOCK, nq), misc, qT, vT, qiT, miscT


def _mask_heads(qT_ref, qm_ref):
    q = qT_ref[0]
    rowh = lax.broadcasted_iota(jnp.int32, q.shape, 0) // HEAD_DIM
    for h in range(HEADS):
        qm_ref[h] = jnp.where(rowh == h, q, jnp.zeros_like(q))


def _init_state(m_ref, acc_ref):
    m_ref[...] = jnp.full(m_ref.shape, NEG, F32)
    acc_ref[...] = jnp.zeros(acc_ref.shape, F32)


def _attention(i, first, k_ref, vT_ref, st, logits_fn, chunk_ctx=None, diag_first=True):
    qm_ref, m_ref, acc_ref, s_ref, mx_ref, p_ref, alpha_ref = st
    n_chunks = i - first + 1
    ones = jnp.ones((ACC_ROWS - HEAD_DIM, CH), BF16)

    def chunk_of(n):
        n = jnp.clip(n, 0, n_chunks - 1)
        if diag_first:
            return jnp.where(n == 0, i, first + n - 1)
        return first + n

    def logits_head(h, slot, c, kc, ctx, diag, pen):
        x = logits_fn(h, c, _dot(kc, qm_ref[h]), ctx, diag, pen)
        s_ref[slot, h] = x
        mx_ref[slot, h] = jnp.max(x, axis=0, keepdims=True)

    def softmax_head(h, slot):
        m_old = m_ref[h]
        m_new = jnp.maximum(m_old, mx_ref[slot, h])
        alpha_ref[slot, h] = jnp.exp2(m_old - m_new)
        p_ref[slot, h] = jnp.exp2(s_ref[slot, h] - m_new).astype(BF16)
        m_ref[h] = m_new

    def pv_head(h, slot, vc):
        v1 = jnp.concatenate([vc[h * HEAD_DIM:(h + 1) * HEAD_DIM, :], ones], axis=0)
        acc_ref[h] = alpha_ref[slot, h] * acc_ref[h] + _dot(v1, p_ref[slot, h])

    def step(n, a, diag):
        c = chunk_of(n + 1)
        pen = jnp.where(n + 1 < n_chunks, 0.0, NEG)
        kc = k_ref[0, pl.ds(pl.multiple_of(c * CH, CH), CH), :]
        vc = vT_ref[0, chunk_of(n - 1)]
        ctx = chunk_ctx(c) if chunk_ctx is not None else None
        for h in range(HEADS):
            softmax_head(h, a)
        for h in range(HEADS):
            pv_head(h, 1 - a, vc)
        for h in range(HEADS):
            logits_head(h, 1 - a, c, kc, ctx, diag, pen)

    p_ref[1] = jnp.zeros(p_ref.shape[1:], BF16)
    alpha_ref[1] = jnp.ones(alpha_ref.shape[1:], F32)
    c0 = jnp.asarray(i if diag_first else first, jnp.int32)
    k0 = k_ref[0, pl.ds(pl.multiple_of(c0 * CH, CH), CH), :]
    ctx0 = chunk_ctx(c0) if chunk_ctx is not None else None
    for h in range(HEADS):
        logits_head(h, 0, c0, k0, ctx0, True if diag_first else None, 0.0)
    later = False if diag_first else None

    def pair(t, carry):
        step(2 * t, 0, later)
        step(2 * t + 1, 1, later)
        return carry

    n_trips = (n_chunks + 1) // 2
    lax.fori_loop(0, n_trips, pair, 0)
    v_last = vT_ref[0, chunk_of(2 * n_trips - 1)]
    for h in range(HEADS):
        pv_head(h, 1, v_last)


def _finish(o_ref, acc_ref):
    parts = [acc_ref[h, 0:HEAD_DIM, :] / acc_ref[h, HEAD_DIM:HEAD_DIM + 1, :] for h in range(HEADS)]
    o_ref[0] = jnp.concatenate(parts, axis=0).T.astype(BF16)


def _lanes(tile, tq):
    return tile if tq == LANES else jnp.concatenate([tile] * (tq // LANES), axis=1)


def _causal_neg(tq):
    s_i = lax.broadcasted_iota(jnp.int32, (CH, tq), 0)
    t_i = lax.broadcasted_iota(jnp.int32, (CH, tq), 1)
    return jnp.where(s_i <= t_i, 0.0, NEG).astype(F32)


def _key_pos_bias(bias_ref, slopes):
    s_i = lax.broadcasted_iota(jnp.int32, (CH, LANES), 0).astype(F32)
    for h in range(HEADS):
        bias_ref[h] = s_i * float(slopes[h] * LOG2E)


def _attn_specs(mixer, seq):
    return [pl.BlockSpec((1, MIX_W, CH), lambda b, i: (b, mixer, i)),
            pl.BlockSpec((1, seq, MIX_W), lambda b, i: (b, 0, mixer)),
            pl.BlockSpec((1, seq // CH, MIX_W, CH), lambda b, i: (b, 0, mixer, 0))]


def _attn_scratch():
    return [pltpu.VMEM((HEADS, MIX_W, CH), BF16),
            pltpu.VMEM((HEADS, 1, CH), F32),
            pltpu.VMEM((HEADS, ACC_ROWS, CH), F32),
            pltpu.VMEM((2, HEADS, CH, CH), F32),
            pltpu.VMEM((2, HEADS, 1, CH), F32),
            pltpu.VMEM((2, HEADS, CH, CH), BF16),
            pltpu.VMEM((2, HEADS, 1, CH), F32)]


def _fox_kernel(qT_ref, k_ref, vT_ref, misc_ref, bf_ref, o_ref,
                cum_ref, *st, n_chunks):
    qm_ref, m_ref, acc_ref = st[:3]
    i = pl.program_id(1)

    @pl.when(i == 0)
    def _cumulative_gates():
        rr = lax.broadcasted_iota(jnp.int32, (CH, CH), 0)
        cc = lax.broadcasted_iota(jnp.int32, (CH, CH), 1)
        tri = jnp.where(cc <= rr, 1.0, 0.0).astype(BF16)
        erow = lax.broadcasted_iota(jnp.int32, (LANES, LANES), 0)

        def body(blk, carry):
            off = pl.multiple_of(blk * CH, CH)
            z = misc_ref[0, pl.ds(off, CH), :] + bf_ref[...]
            ls = jnp.minimum(z, 0.0) - jnp.log1p(jnp.exp(-jnp.abs(z)))
            csum = sum(_dot(tri, part) for part in _split3(ls))
            parts = _split3(csum)
            new = []
            for h in range(HEADS):
                sel = jnp.where(erow == FL_ROW + h, 1.0, 0.0).astype(BF16)
                cum = sum(_dot(part, sel) for part in parts) + carry[h]
                cum_ref[h, pl.ds(off, CH), :] = cum * LOG2E
                new.append(cum[CH - 1:CH, :])
            return tuple(new)

        lax.fori_loop(0, n_chunks, body, tuple(jnp.zeros((1, LANES), F32) for _ in range(HEADS)))

    _mask_heads(qT_ref, qm_ref)
    _init_state(m_ref, acc_ref)

    def logits(h, c, qk, _, diag, pen):
        off = pl.multiple_of(c * CH, CH)
        cum = cum_ref[h, pl.ds(off, CH), :]
        if diag:
            return qk - _lanes(cum, CH) + _causal_neg(CH)
        return qk - _lanes(cum - pen, CH)

    _attention(i, 0, k_ref, vT_ref, st, logits)
    _finish(o_ref, acc_ref)


def _fox(qT, k, vT, misc, b_forget_l):
    bsz, seq, _ = k.shape
    n_chunks = seq // CH
    bf = jnp.zeros((1, MISC_W), F32).at[0, FL_ROW:FL_ROW + HEADS].set(b_forget_l)
    return pl.pallas_call(
        functools.partial(_fox_kernel, n_chunks=n_chunks),
        out_shape=jax.ShapeDtypeStruct((bsz, seq, MIX_W), BF16),
        grid=(bsz, n_chunks),
        in_specs=_attn_specs(3, seq) + [
            pl.BlockSpec((1, seq, MISC_W), lambda b, i: (b, 0, 0)),
            pl.BlockSpec((1, MISC_W), lambda b, i: (0, 0))],
        out_specs=pl.BlockSpec((1, CH, MIX_W), lambda b, i: (b, i, 0)),
        scratch_shapes=[pltpu.VMEM((HEADS, seq, LANES), F32)] + _attn_scratch(),
        compiler_params=_params(2),
        name="fox",
    )(qT, k, vT, misc, bf)


def _moba_kernel(qT_ref, k_ref, vT_ref, kmean_ref, o_ref,
                 bias_ref, rowadd_ref, *st, slopes, n_blocks):
    qm_ref, m_ref, acc_ref = st[:3]
    i = pl.program_id(1)
    _mask_heads(qT_ref, qm_ref)
    _init_state(m_ref, acc_ref)

    @pl.when(jnp.logical_and(pl.program_id(0) == 0, i == 0))
    def _bias_table():
        s_f = lax.broadcasted_iota(jnp.int32, (CH, CH), 0).astype(F32)
        causal = _causal_neg(CH)
        for h in range(HEADS):
            pos = s_f * float(slopes[h] * LOG2E)
            bias_ref[0, h] = pos
            bias_ref[1, h] = pos + causal

    n_i = lax.broadcasted_iota(jnp.int32, (n_blocks, CH), 0)
    n_f = n_i.astype(F32)
    past = n_i < i
    for h in range(HEADS):
        gate = _dot_hi(kmean_ref[0], qm_ref[h].astype(F32))
        gate = jnp.where(past, gate, -jnp.inf)
        chosen = jnp.zeros((n_blocks, CH), F32)
        for _ in range(MOBA_TOPK):
            mx = jnp.max(gate, axis=0, keepdims=True)
            first = jnp.min(jnp.where(gate == mx, n_f, float(n_blocks)), axis=0, keepdims=True)
            pick = n_f == first
            chosen = jnp.where(pick, 1.0, chosen)
            gate = jnp.where(pick, -jnp.inf, gate)
        chosen = jnp.where(past, chosen, 0.0)
        blk_shift = n_f * float(slopes[h] * LOG2E * MOBA_BLOCK)
        rowadd_ref[h] = jnp.where(jnp.logical_or(chosen > 0.5, n_i == i), blk_shift, NEG)

    def logits(h, c, qk, _, diag, pen):
        row = rowadd_ref[h, pl.ds(c, 1), :]
        if diag:
            return qk + bias_ref[1, h] + row
        return qk + bias_ref[0, h] + (row + pen)

    _attention(i, 0, k_ref, vT_ref, st, logits)
    _finish(o_ref, acc_ref)


def _moba(qT, k, vT, kmean, slopes):
    bsz, seq, _ = k.shape
    n_blocks = seq // MOBA_BLOCK
    return pl.pallas_call(
        functools.partial(_moba_kernel, slopes=tuple(float(s) for s in slopes), n_blocks=n_blocks),
        out_shape=jax.ShapeDtypeStruct((bsz, seq, MIX_W), BF16),
        grid=(bsz, seq // CH),
        in_specs=_attn_specs(2, seq) + [
            pl.BlockSpec((1, n_blocks, MIX_W), lambda b, i: (b, 0, 2))],
        out_specs=pl.BlockSpec((1, CH, MIX_W), lambda b, i: (b, i, 0)),
        scratch_shapes=[pltpu.VMEM((2, HEADS, CH, CH), F32),
                        pltpu.VMEM((HEADS, n_blocks, CH), F32)] + _attn_scratch(),
        compiler_params=_params(2),
        name="moba",
    )(qT, k, vT, kmean)


DIL_SPAN = 2048 // CH + 1


def _dilated_kernel(qT_ref, k_ref, vT_ref, o_ref, table_ref, *st, slopes):
    qm_ref, m_ref, acc_ref = st[:3]
    b = pl.program_id(0)
    i = pl.program_id(1)

    @pl.when(jnp.logical_and(b == 0, i == 0))
    def _bias_table():
        s_i = lax.broadcasted_iota(jnp.int32, (CH, CH), 0)
        t_i = lax.broadcasted_iota(jnp.int32, (CH, CH), 1)
        for j in range(DIL_SPAN):
            d = t_i - s_i + j * CH
            ok = d >= 0
            mult = (jnp.where(jnp.logical_and(ok, d <= 128), 1.0, 0.0)
                    + jnp.where(jnp.logical_and(ok, jnp.logical_and(d <= 512, (d & 3) == 0)), 1.0, 0.0)
                    + jnp.where(jnp.logical_and(ok, jnp.logical_and(d <= 2048, (d & 15) == 0)), 1.0, 0.0))
            logm = jnp.where(mult > 0.5, jnp.log2(jnp.maximum(mult, 1.0)), NEG)
            df = d.astype(F32)
            for h in range(HEADS):
                table_ref[h, j] = logm - df * float(slopes[h] * LOG2E)
        for h in range(HEADS):
            table_ref[h, DIL_SPAN] = jnp.full((CH, CH), NEG, F32)

    _mask_heads(qT_ref, qm_ref)
    _init_state(m_ref, acc_ref)

    def logits(h, c, qk, _, diag, pen):
        if diag:
            return qk + table_ref[h, 0]
        return qk + table_ref[h, jnp.where(pen < 0.0, DIL_SPAN, i - c)]

    _attention(i, jnp.maximum(i - (DIL_SPAN - 1), 0), k_ref, vT_ref, st, logits)
    _finish(o_ref, acc_ref)


def _dilated(qT, k, vT, slopes):
    bsz, seq, _ = k.shape
    return pl.pallas_call(
        functools.partial(_dilated_kernel, slopes=tuple(float(s) for s in slopes)),
        out_shape=jax.ShapeDtypeStruct((bsz, seq, MIX_W), BF16),
        grid=(bsz, seq // CH),
        in_specs=_attn_specs(1, seq),
        out_specs=pl.BlockSpec((1, CH, MIX_W), lambda b, i: (b, i, 0)),
        scratch_shapes=[pltpu.VMEM((HEADS, DIL_SPAN + 1, CH, CH), F32)] + _attn_scratch(),
        compiler_params=_params(2),
        name="dilated",
    )(qT, k, vT)


def _dsa_kernel(qT_ref, k_ref, vT_ref, qiT_ref, misc_ref, miscT_ref, o_ref,
                bias_ref, qi_ref, key_ref, half_ref, seen_ref, *st, slopes):
    qm_ref, m_ref, acc_ref = st[:3]
    i = pl.program_id(1)
    tq = CH
    _mask_heads(qT_ref, qm_ref)
    _init_state(m_ref, acc_ref)
    _key_pos_bias(bias_ref, slopes)

    qi_all = qiT_ref[0]
    zpad = jnp.zeros((MISC_W - IDX_DIM, tq), BF16)
    for h in range(IDX_HEADS):
        qi_ref[h] = jnp.concatenate([qi_all[h * IDX_DIM:(h + 1) * IDX_DIM], zpad], axis=0)
    w_rows = miscT_ref[0, WI_ROW:WI_ROW + IDX_HEADS, :] * float(IDX_HEADS ** -0.5 * IDX_DIM ** -0.5)

    s_i = lax.broadcasted_iota(jnp.int32, (CH, tq), 0)
    t_i = lax.broadcasted_iota(jnp.int32, (CH, tq), 1)

    def score_chunk(c, diag):
        off = pl.multiple_of(c * CH, CH)
        ki = misc_ref[0, pl.ds(off, CH), :].astype(BF16)
        score = jnp.zeros((CH, tq), F32)
        for h in range(IDX_HEADS):
            rel = jnp.maximum(_dot(ki, qi_ref[h]), 0.0)
            score = score + rel * w_rows[h:h + 1, :]
        bits = lax.bitcast_convert_type(score, jnp.int32)
        key = bits ^ ((bits >> 31) & 0x7FFFFFFF)
        key = jnp.where(key == -1, 0, key)
        if diag:
            key = jnp.where(s_i <= t_i, key, INT_MIN)
        key_ref[pl.ds(off, CH), :] = key
        half_ref[pl.ds(off, CH), :] = (key >> 16).astype(jnp.int16)

    score_chunk(i, True)
    lax.fori_loop(0, i, lambda c, _: (score_chunk(c, False), 0)[1], 0)
    half_ref[pl.ds(pl.multiple_of((i + 1) * CH, CH), CH), :] = jnp.full((CH, tq), -32768, jnp.int16)

    def count(ref, pred):
        def body(c, cnt):
            off = pl.multiple_of(c * CH, CH)
            hit = jnp.where(pred(ref[pl.ds(off, CH), :]), 1, 0)
            return cnt + jnp.sum(hit.reshape(CH // 8, 8, tq), axis=0)
        cnt8 = lax.fori_loop(0, i + 1, body, jnp.zeros((8, tq), jnp.int32))
        return jnp.sum(cnt8, axis=0, keepdims=True)

    def count_half(cand):
        cand16 = cand.astype(jnp.int16)
        rows = 16

        def body(c2, cnt):
            off = pl.multiple_of(c2 * (2 * CH), 2 * CH)
            hit = jnp.where(half_ref[pl.ds(off, 2 * CH), :] >= cand16, jnp.int16(1), jnp.int16(0))
            hit = hit.reshape(2 * CH // rows, rows, tq)
            for r in range(2 * CH // rows):
                cnt = cnt + hit[r]
            return cnt

        cnt16 = lax.fori_loop(0, i // 2 + 1, body, jnp.zeros((rows, tq), jnp.int16))
        return jnp.sum(cnt16.astype(jnp.int32), axis=0, keepdims=True)

    def search_half(need, n_all):
        def bit_step(n, state):
            v, n_ge = state
            cand = v + jnp.left_shift(jnp.int32(1), 15 - n)
            cnt = count_half(cand)
            ok = cnt >= need
            return jnp.where(ok, cand, v), jnp.where(ok, cnt, n_ge)

        return lax.fori_loop(0, 16, bit_step, (jnp.full((1, tq), -32768, jnp.int32), n_all))

    n_all = jnp.zeros((1, tq), jnp.int32) + (i + 1) * CH
    hi, n_ge_hi = search_half(TOPK_KEYS, n_all)
    n_gt_hi = jnp.where(hi == 32767, 0, count_half(jnp.minimum(hi + 1, 32767)))
    need_lo = TOPK_KEYS - n_gt_hi

    def lower_halves(c, _):
        off = pl.multiple_of(c * CH, CH)
        key = key_ref[pl.ds(off, CH), :]
        lo = (key & 0xFFFF) - 32768
        half_ref[pl.ds(off, CH), :] = jnp.where((key >> 16) == hi, lo, -32768).astype(jnp.int16)
        return 0

    lax.fori_loop(0, i + 1, lower_halves, 0)
    lo, n_ge_lo = search_half(need_lo, n_ge_hi - n_gt_hi)
    thr = hi * 65536 + (lo + 32768)
    n_ge = n_gt_hi + n_ge_lo
    tied = jnp.logical_and(n_ge > TOPK_KEYS, thr != INT_MIN)
    any_tied = jnp.max(jnp.where(tied, 1.0, 0.0)) > 0.5

    def logits(h, c, qk, neg, diag, pen):
        pos = bias_ref[h] + (c.astype(F32) * float(slopes[h] * LOG2E * CH) + pen)
        return qk + _lanes(pos, tq) + neg

    @pl.when(jnp.logical_not(any_tied))
    def _no_ties():
        thr_lo = jnp.where(thr == INT_MIN, INT_MIN + 1, thr)

        def selection(c):
            off = pl.multiple_of(c * CH, CH)
            return jnp.where(key_ref[pl.ds(off, CH), :] >= thr_lo, 0.0, NEG)

        _attention(i, 0, k_ref, vT_ref, st, logits, selection, diag_first=False)

    @pl.when(any_tied)
    def _ties():
        n_gt = count(key_ref, lambda key: key > thr)
        n_take = jnp.where(thr == INT_MIN, 0, TOPK_KEYS - n_gt).astype(F32)
        rr = lax.broadcasted_iota(jnp.int32, (CH, CH), 0)
        cc = lax.broadcasted_iota(jnp.int32, (CH, CH), 1)
        tri = jnp.where(cc <= rr, 1.0, 0.0).astype(BF16)
        seen_ref[...] = jnp.zeros(seen_ref.shape, F32)

        def selection(c):
            off = pl.multiple_of(c * CH, CH)
            key = key_ref[pl.ds(off, CH), :]
            eq = key == thr
            rank = _dot(tri, jnp.where(eq, 1.0, 0.0).astype(BF16)) + seen_ref[...]
            seen_ref[...] = rank[CH - 1:CH, :]
            take = jnp.logical_or(key > thr, jnp.logical_and(eq, rank <= n_take))
            return jnp.where(take, 0.0, NEG)

        _attention(i, 0, k_ref, vT_ref, st, logits, selection, diag_first=False)

    _finish(o_ref, acc_ref)


def _dsa(qT, k, vT, qiT, misc, miscT, slopes):
    bsz, seq, _ = k.shape
    nqi = IDX_HEADS * IDX_DIM
    return pl.pallas_call(
        functools.partial(_dsa_kernel, slopes=tuple(float(s) for s in slopes)),
        out_shape=jax.ShapeDtypeStruct((bsz, seq, MIX_W), BF16),
        grid=(bsz, seq // CH),
        in_specs=_attn_specs(0, seq) + [
            pl.BlockSpec((1, nqi, CH), lambda b, i: (b, 0, i)),
            pl.BlockSpec((1, seq, MISC_W), lambda b, i: (b, 0, 0)),
            pl.BlockSpec((1, MISC_W, CH), lambda b, i: (b, 0, i))],
        out_specs=pl.BlockSpec((1, CH, MIX_W), lambda b, i: (b, i, 0)),
        scratch_shapes=[pltpu.VMEM((HEADS, CH, LANES), F32),
                        pltpu.VMEM((IDX_HEADS, MISC_W, CH), BF16),
                        pltpu.VMEM((seq, CH), jnp.int32),
                        pltpu.VMEM((seq + CH, CH), jnp.int16),
                        pltpu.VMEM((1, CH), F32)] + _attn_scratch(),
        compiler_params=_params(2),
        name="dsa",
    )(qT, k, vT, qiT, misc, miscT)


def _merge_kernel(x_ref, ada_ref, oa_ref, ob_ref, oc_ref, od_ref, wmg_ref, wbr_ref, wout_ref,
                  g_ref, b_ref, out_ref):
    x = x_ref[0]
    sh = ada_ref[0, 0:1, :]
    sc = ada_ref[0, 1:2, :]
    g1 = ada_ref[0, 2:3, :]
    h = (x * (1.0 + sc) + sh).astype(BF16)
    mixed = None
    for m, o_ref in enumerate((oa_ref, ob_ref, oc_ref, od_ref)):
        gate = jax.nn.sigmoid(_dot(h, wmg_ref[m]))
        term = gate * _dot(o_ref[0], wbr_ref[m])
        mixed = term if mixed is None else mixed + term
    y = _dot(mixed.astype(BF16), wout_ref[...])
    out_ref[0] = _layer_norm(DN_ALPHA * x + g1 * y, g_ref[...], b_ref[...])


def _merge(x, ada, outs, w_mg, w_br, w_out, ln_g, ln_b, tm=512):
    bsz, seq, d = x.shape
    o_spec = pl.BlockSpec((1, tm, MIX_W), lambda b, i: (b, i, 0))
    return pl.pallas_call(
        _merge_kernel,
        out_shape=jax.ShapeDtypeStruct((bsz, seq, d), F32),
        grid=(bsz, seq // tm),
        in_specs=[pl.BlockSpec((1, tm, d), lambda b, i: (b, i, 0)),
                  pl.BlockSpec((1, 6, d), lambda b, i: (b, 0, 0)),
                  o_spec, o_spec, o_spec, o_spec,
                  pl.BlockSpec((N_MIX, d, d), lambda b, i: (0, 0, 0)),
                  pl.BlockSpec((N_MIX, MIX_W, d), lambda b, i: (0, 0, 0)),
                  pl.BlockSpec((d, d), lambda b, i: (0, 0)),
                  pl.BlockSpec((1, d), lambda b, i: (0, 0)),
                  pl.BlockSpec((1, d), lambda b, i: (0, 0))],
        out_specs=pl.BlockSpec((1, tm, d), lambda b, i: (b, i, 0)),
        compiler_params=_params(2),
        name="merge",
    )(x, ada, *outs, w_mg.astype(BF16), w_br.astype(BF16), w_out.astype(BF16),
      ln_g.reshape(1, d), ln_b.reshape(1, d))


def _route(logits_t, bias_col):
    per_group = N_EXPERTS // N_GROUPS
    scores = jax.nn.sigmoid(logits_t)
    biased = scores + bias_col
    s_rows = [scores[e:e + 1] for e in range(N_EXPERTS)]
    b_rows = [biased[e:e + 1] for e in range(N_EXPERTS)]
    best_g = None
    for g in range(N_GROUPS):
        r = b_rows[g * per_group:(g + 1) * per_group]
        gs = None
        for a in range(per_group):
            for b in range(a + 1, per_group):
                pair = r[a] + r[b]
                gs = pair if gs is None else jnp.maximum(gs, pair)
        if best_g is None:
            best_g, best_v = jnp.zeros_like(gs, dtype=jnp.int32), gs
        else:
            better = gs > best_v
            best_g = jnp.where(better, g, best_g)
            best_v = jnp.maximum(best_v, gs)
    masked = [jnp.where(best_g == e // per_group, b_rows[e], -jnp.inf) for e in range(N_EXPERTS)]

    def argmax_first(rows):
        idx, val = jnp.zeros_like(best_g), rows[0]
        for e in range(1, N_EXPERTS):
            better = rows[e] > val
            idx = jnp.where(better, e, idx)
            val = jnp.maximum(val, rows[e])
        return idx

    e1 = argmax_first(masked)
    e2 = argmax_first([jnp.where(e1 == e, -jnp.inf, masked[e]) for e in range(N_EXPERTS)])
    s1 = sum(jnp.where(e1 == e, s_rows[e], 0.0) for e in range(N_EXPERTS))
    s2 = sum(jnp.where(e2 == e, s_rows[e], 0.0) for e in range(N_EXPERTS))
    tot = s1 + s2
    w1, w2 = s1 / tot, s2 / tot
    rows = [jnp.where(e1 == e, w1, 0.0) + jnp.where(e2 == e, w2, 0.0) for e in range(N_EXPERTS)]
    return jnp.concatenate(rows, axis=0), best_g


def _moe_kernel(x_ref, ada_ref, wr_ref, br_ref, wg_ref, wu_ref, wd_ref, g_ref, b_ref, out_ref,
                perm_ref, xs_ref, combs_ref, ys_ref, sub_ref, *, tm, rows):
    e = pl.program_id(1)
    per_group = N_EXPERTS // N_GROUPS

    @pl.when(e == 0)
    def _route_and_sort():
        x = x_ref[...]
        h2 = x * (1.0 + ada_ref[0, 4:5, :]) + ada_ref[0, 3:4, :]
        logits = _dot_hi(h2, wr_ref[...])
        logits_t = logits.T[0:N_EXPERTS]
        comb_t, best_g = _route(logits_t, br_ref[...])
        member = [jnp.where(best_g == g, 1.0, 0.0) for g in range(N_GROUPS)]
        grp = jnp.concatenate(member + [jnp.zeros((8 - N_GROUPS, tm), F32)], axis=0).astype(BF16)
        s_i = lax.broadcasted_iota(jnp.int32, (tm, tm), 0)
        t_i = lax.broadcasted_iota(jnp.int32, (tm, tm), 1)
        rank = _dot(grp, jnp.where(s_i <= t_i, 1.0, 0.0).astype(BF16))
        pos = jnp.zeros((1, tm), F32)
        start = jnp.zeros((1, 1), F32)
        for g in range(N_GROUPS):
            cap = jnp.ceil(rank[g:g + 1, tm - 1:tm] * (1.0 / MOE_SUB)) * MOE_SUB
            pos = pos + member[g] * (start + rank[g:g + 1] - 1.0)
            sub_ref[2 * g] = (jnp.sum(start) * (1.0 / MOE_SUB)).astype(jnp.int32)
            sub_ref[2 * g + 1] = (jnp.sum(cap) * (1.0 / MOE_SUB)).astype(jnp.int32)
            start = start + cap
        r_f = lax.broadcasted_iota(jnp.int32, (rows, tm), 0).astype(F32)
        perm = jnp.where(r_f == pos, 1.0, 0.0).astype(BF16)
        perm_ref[...] = perm
        xs_ref[...] = _dot(perm, h2.astype(BF16)).astype(BF16)
        comb = jnp.concatenate([comb_t, jnp.zeros((LANES - N_EXPERTS, tm), F32)], axis=0).T
        c_hi = comb.astype(BF16)
        c_lo = (comb - c_hi.astype(F32)).astype(BF16)
        combs_ref[...] = _dot(perm, c_hi) + _dot(perm, c_lo)
        ys_ref[...] = jnp.zeros(ys_ref.shape, F32)

    g = e // per_group
    lane = lax.broadcasted_iota(jnp.int32, (MOE_SUB, LANES), 1)

    def sub_tile(j, carry):
        r0 = pl.multiple_of((sub_ref[2 * g] + j) * MOE_SUB, MOE_SUB)
        xj = xs_ref[pl.ds(r0, MOE_SUB), :]
        w_e = jnp.sum(jnp.where(lane == e, combs_ref[pl.ds(r0, MOE_SUB), :], 0.0), axis=1, keepdims=True)
        hid = (jax.nn.silu(_dot(xj, wg_ref[0, 0])) * _dot(xj, wu_ref[0, 0]) * w_e).astype(BF16)
        ys_ref[pl.ds(r0, MOE_SUB), :] += _dot(hid, wd_ref[0, 0])
        return carry

    lax.fori_loop(0, sub_ref[2 * g + 1], sub_tile, 0)

    @pl.when(e == N_EXPERTS - 1)
    def _unsort_and_norm():
        y = lax.dot_general(perm_ref[...], ys_ref[...].astype(BF16), (((0,), (0,)), ((), ())),
                            preferred_element_type=F32)
        z = DN_ALPHA * x_ref[...] + ada_ref[0, 5:6, :] * y
        out_ref[...] = _layer_norm(z, g_ref[...], b_ref[...])


def _moe(x, ada_rows, w_router, b_router, w_gate, w_up, w_down, ln_g, ln_b, batch_len, tm=1024):
    n, d = x.shape
    per_b = batch_len // tm
    rows = tm + N_GROUPS * MOE_SUB
    wr = jnp.pad(w_router, ((0, 0), (0, LANES - N_EXPERTS)))
    br = b_router.reshape(N_EXPERTS, 1)
    return pl.pallas_call(
        functools.partial(_moe_kernel, tm=tm, rows=rows),
        out_shape=jax.ShapeDtypeStruct((n, d), F32),
        grid=(n // tm, N_EXPERTS),
        in_specs=[pl.BlockSpec((tm, d), lambda i, e: (i, 0)),
                  pl.BlockSpec((1, 6, d), lambda i, e: (i // per_b, 0, 0)),
                  pl.BlockSpec((d, LANES), lambda i, e: (0, 0)),
                  pl.BlockSpec((N_EXPERTS, 1), lambda i, e: (0, 0)),
                  pl.BlockSpec((1, 1, d, D_EXPERT), lambda i, e: (0, e, 0, 0)),
                  pl.BlockSpec((1, 1, d, D_EXPERT), lambda i, e: (0, e, 0, 0)),
                  pl.BlockSpec((1, 1, D_EXPERT, d), lambda i, e: (0, e, 0, 0)),
                  pl.BlockSpec((1, d), lambda i, e: (0, 0)),
                  pl.BlockSpec((1, d), lambda i, e: (0, 0))],
        out_specs=pl.BlockSpec((tm, d), lambda i, e: (i, 0)),
        scratch_shapes=[pltpu.VMEM((rows, tm), BF16),
                        pltpu.VMEM((rows, d), BF16),
                        pltpu.VMEM((rows, LANES), F32),
                        pltpu.VMEM((rows, d), F32),
                        pltpu.SMEM((2 * N_GROUPS,), jnp.int32)],
        compiler_params=_params(2),
        name="moe",
    )(x, ada_rows, wr, br, w_gate[None].astype(BF16), w_up[None].astype(BF16), w_down[None].astype(BF16),
      ln_g.reshape(1, d), ln_b.reshape(1, d))


def kernel(x, c, w_in, b_forget, w_branch, w_merge_gate, w_out, w_ada, b_ada, ln1_g, ln1_b, ln2_g, ln2_b,
           w_router, b_router, w_exp_gate, w_exp_up, w_exp_down):
    bsz, seq, d = x.shape
    slopes = _alibi_slopes()
    ada_all = _ada(c, w_ada, b_ada).reshape(DEPTH, bsz, 6, d)
    for l in range(DEPTH):
        ada = ada_all[l]
        k, kmean, misc, qT, vT, qiT, miscT = _proj(x, ada, w_in[l])
        o_a = _dsa(qT, k, vT, qiT, misc, miscT, slopes[0])
        o_b = _dilated(qT, k, vT, slopes[1])
        o_c = _moba(qT, k, vT, kmean, slopes[2])
        o_d = _fox(qT, k, vT, misc, b_forget[l])
        x = _merge(x, ada, (o_a, o_b, o_c, o_d), w_merge_gate[l], w_branch[l], w_out[l], ln1_g[l], ln1_b[l])
        x = _moe(x.reshape(bsz * seq, d), ada, w_router, b_router, w_exp_gate[l], w_exp_up[l], w_exp_down[l],
                 ln2_g[l], ln2_b[l], seq).reshape(bsz, seq, d)
    return x
```

```python
import functools

import numpy as np
import jax
import jax.numpy as jnp
from jax import lax
from jax.experimental import pallas as pl
from jax.experimental.pallas import tpu as pltpu

D_MODEL = 1024
HEAD_DIM = 64
HEADS = 4
MIX_W = HEADS * HEAD_DIM
N_MIX = 4
IDX_HEADS = 8
IDX_DIM = 64
TOPK_KEYS = 256
MOBA_BLOCK = 256
MOBA_TOPK = 3
N_EXPERTS = 16
N_GROUPS = 4
D_EXPERT = 512
DEPTH = 2
DN_ALPHA = (2 * DEPTH) ** 0.25
LN_EPS = 1e-5
IN_COLS = 3 * N_MIX * MIX_W + IDX_HEADS * IDX_DIM + IDX_DIM + IDX_HEADS + HEADS

CH = 256
ACC_ROWS = HEAD_DIM + 16
MOE_SUB = 128
LANES = 128
MISC_W = 128
WI_ROW = IDX_DIM
FL_ROW = IDX_DIM + IDX_HEADS
LOG2E = 1.4426950408889634
NEG = -1e30
QSCALE = HEAD_DIM ** -0.5 * LOG2E
INT_MIN = -(2 ** 31)
VMEM_LIMIT = 56 * 1024 * 1024

F32 = jnp.float32
BF16 = jnp.bfloat16
HI = lax.Precision.HIGHEST
NT = (((1,), (1,)), ((), ()))


def _alibi_slopes():
    n = 3 * HEADS
    s = 2.0 ** (-8.0 * np.arange(1, n + 1) / n)
    return s.reshape(HEADS, 3).T


def _dot(a, b):
    return jnp.dot(a, b, preferred_element_type=F32)


def _dot_hi(a, b):
    return jnp.dot(a, b, preferred_element_type=F32, precision=HI)


def _split3(x):
    x1 = x.astype(BF16)
    r1 = x - x1.astype(F32)
    x2 = r1.astype(BF16)
    x3 = (r1 - x2.astype(F32)).astype(BF16)
    return x1, x2, x3


def _params(n_axes):
    return pltpu.CompilerParams(dimension_semantics=("arbitrary",) * n_axes,
                                vmem_limit_bytes=VMEM_LIMIT)


def _layer_norm(z, g, b):
    mu = jnp.mean(z, axis=-1, keepdims=True)
    var = jnp.mean(jnp.square(z - mu), axis=-1, keepdims=True)
    return (z - mu) * lax.rsqrt(var + LN_EPS) * g + b


def _ada_kernel(c_ref, w_ref, b_ref, o_ref):
    o_ref[0] = _dot_hi(c_ref[...], w_ref[0]) + b_ref[0]


def _ada(c, w_ada, b_ada):
    depth, d, n = w_ada.shape
    bsz = c.shape[0]
    tn = D_MODEL
    return pl.pallas_call(
        _ada_kernel,
        out_shape=jax.ShapeDtypeStruct((depth, bsz, n), F32),
        grid=(depth, n // tn),
        in_specs=[pl.BlockSpec((bsz, d), lambda l, j: (0, 0)),
                  pl.BlockSpec((1, d, tn), lambda l, j: (l, 0, j)),
                  pl.BlockSpec((1, 1, tn), lambda l, j: (l, 0, j))],
        out_specs=pl.BlockSpec((1, bsz, tn), lambda l, j: (l, 0, j)),
        compiler_params=_params(2),
        name="ada",
    )(c, w_ada, b_ada.reshape(depth, 1, n))


def _proj_kernel(x_ref, ada_ref, wk_ref, wm_ref, wt_ref, wmt_ref,
                 k_ref, kmean_ref, misc_ref, qT_ref, vT_ref, qiT_ref, miscT_ref, *, tm):
    x = x_ref[0]
    sh = ada_ref[0, 0:1, :]
    sc = ada_ref[0, 1:2, :]
    h = (x * (1.0 + sc) + sh).astype(BF16)
    kf = _dot(h, wk_ref[...])
    k_ref[0] = kf.astype(BF16)
    for g in range(tm // MOBA_BLOCK):
        kmean_ref[0, 0, g:g + 1, :] = jnp.mean(kf[g * MOBA_BLOCK:(g + 1) * MOBA_BLOCK], axis=0, keepdims=True)
    misc_ref[0] = _dot(h, wm_ref[...])
    t = lax.dot_general(wt_ref[...], h, NT, preferred_element_type=F32)
    nq = N_MIX * MIX_W
    qT_ref[0] = (t[0:nq] * QSCALE).astype(BF16)
    for g in range(tm // CH):
        vT_ref[0, g] = t[nq:2 * nq, g * CH:(g + 1) * CH].astype(BF16)
    qiT_ref[0] = t[2 * nq:].astype(BF16)
    miscT_ref[0] = lax.dot_general(wmt_ref[...], h, NT, preferred_element_type=F32)


def _proj(x, ada, w_in, tm=512):
    bsz, seq, d = x.shape
    nq = N_MIX * MIX_W
    nqi = IDX_HEADS * IDX_DIM
    w = w_in.astype(BF16)
    wk = w[:, nq:2 * nq]
    wm = jnp.pad(w[:, 3 * nq + nqi:], ((0, 0), (0, MISC_W - (IN_COLS - 3 * nq - nqi))))
    wt = jnp.concatenate([w[:, 0:nq], w[:, 2 * nq:3 * nq], w[:, 3 * nq:3 * nq + nqi]], axis=1).T
    wmt = wm.T
    nt = seq // tm
    full = lambda b, i: (0, 0)
    outs = pl.pallas_call(
        functools.partial(_proj_kernel, tm=tm),
        out_shape=(jax.ShapeDtypeStruct((bsz, seq, nq), BF16),
                   jax.ShapeDtypeStruct((bsz, nt, tm // MOBA_BLOCK, nq), F32),
                   jax.ShapeDtypeStruct((bsz, seq, MISC_W), F32),
                   jax.ShapeDtypeStruct((bsz, nq, seq), BF16),
                   jax.ShapeDtypeStruct((bsz, seq // CH, nq, CH), BF16),
                   jax.ShapeDtypeStruct((bsz, nqi, seq), BF16),
                   jax.ShapeDtypeStruct((bsz, MISC_W, seq), F32)),
        grid=(bsz, nt),
        in_specs=[pl.BlockSpec((1, tm, d), lambda b, i: (b, i, 0)),
                  pl.BlockSpec((1, 6, d), lambda b, i: (b, 0, 0)),
                  pl.BlockSpec(wk.shape, full),
                  pl.BlockSpec(wm.shape, full),
                  pl.BlockSpec(wt.shape, full),
                  pl.BlockSpec(wmt.shape, full)],
        out_specs=(pl.BlockSpec((1, tm, nq), lambda b, i: (b, i, 0)),
                   pl.BlockSpec((1, 1, tm // MOBA_BLOCK, nq), lambda b, i: (b, i, 0, 0)),
                   pl.BlockSpec((1, tm, MISC_W), lambda b, i: (b, i, 0)),
                   pl.BlockSpec((1, nq, tm), lambda b, i: (b, 0, i)),
                   pl.BlockSpec((1, tm // CH, nq, CH), lambda b, i: (b, i, 0, 0)),
                   pl.BlockSpec((1, nqi, tm), lambda b, i: (b, 0, i)),
                   pl.BlockSpec((1, MISC_W, tm), lambda b, i: (b, 0, i))),
        compiler_params=_params(2),
        name="proj",
    )(x, ada, wk, wm, wt, wmt)
    k, kmean, misc, qT, vT, qiT, miscT = outs
    return k, kmean.reshape(bsz, seq // MOBA_BLOCK, nq), misc, qT, vT, qiT, miscT


def _mask_heads(qT_ref, qm_ref):
    q = qT_ref[0]
    rowh = lax.broadcasted_iota(jnp.int32, q.shape, 0) // HEAD_DIM
    for h in range(HEADS):
        qm_ref[h] = jnp.where(rowh == h, q, jnp.zeros_like(q))


def _init_state(m_ref, acc_ref):
    m_ref[...] = jnp.full(m_ref.shape, NEG, F32)
    acc_ref[...] = jnp.zeros(acc_ref.shape, F32)


def _attention(i, first, k_ref, vT_ref, st, logits_fn, chunk_ctx=None, diag_first=True):
    qm_ref, m_ref, acc_ref, s_ref, mx_ref, p_ref, alpha_ref = st
    n_chunks = i - first + 1
    ones = jnp.ones((ACC_ROWS - HEAD_DIM, CH), BF16)

    def chunk_of(n):
        n = jnp.clip(n, 0, n_chunks - 1)
        if diag_first:
            return jnp.where(n == 0, i, first + n - 1)
        return first + n

    def logits_head(h, slot, c, kc, ctx, diag, pen):
        x = logits_fn(h, c, _dot(kc, qm_ref[h]), ctx, diag, pen)
        s_ref[slot, h] = x
        mx_ref[slot, h] = jnp.max(x, axis=0, keepdims=True)

    def softmax_head(h, slot):
        m_old = m_ref[h]
        m_new = jnp.maximum(m_old, mx_ref[slot, h])
        alpha_ref[slot, h] = jnp.exp2(m_old - m_new)
        p_ref[slot, h] = jnp.exp2(s_ref[slot, h] - m_new).astype(BF16)
        m_ref[h] = m_new

    def pv_head(h, slot, vc):
        v1 = jnp.concatenate([vc[h * HEAD_DIM:(h + 1) * HEAD_DIM, :], ones], axis=0)
        acc_ref[h] = alpha_ref[slot, h] * acc_ref[h] + _dot(v1, p_ref[slot, h])

    def step(n, a, diag):
        c = chunk_of(n + 1)
        pen = jnp.where(n + 1 < n_chunks, 0.0, NEG)
        kc = k_ref[0, pl.ds(pl.multiple_of(c * CH, CH), CH), :]
        vc = vT_ref[0, chunk_of(n - 1)]
        ctx = chunk_ctx(c) if chunk_ctx is not None else None
        for h in range(HEADS):
            softmax_head(h, a)
        for h in range(HEADS):
            pv_head(h, 1 - a, vc)
        for h in range(HEADS):
            logits_head(h, 1 - a, c, kc, ctx, diag, pen)

    p_ref[1] = jnp.zeros(p_ref.shape[1:], BF16)
    alpha_ref[1] = jnp.ones(alpha_ref.shape[1:], F32)
    c0 = jnp.asarray(i if diag_first else first, jnp.int32)
    k0 = k_ref[0, pl.ds(pl.multiple_of(c0 * CH, CH), CH), :]
    ctx0 = chunk_ctx(c0) if chunk_ctx is not None else None
    for h in range(HEADS):
        logits_head(h, 0, c0, k0, ctx0, True if diag_first else None, 0.0)
    later = False if diag_first else None

    def pair(t, carry):
        step(2 * t, 0, later)
        step(2 * t + 1, 1, later)
        return carry

    n_trips = (n_chunks + 1) // 2
    lax.fori_loop(0, n_trips, pair, 0)
    v_last = vT_ref[0, chunk_of(2 * n_trips - 1)]
    for h in range(HEADS):
        pv_head(h, 1, v_last)


def _finish(o_ref, acc_ref):
    parts = [acc_ref[h, 0:HEAD_DIM, :] / acc_ref[h, HEAD_DIM:HEAD_DIM + 1, :] for h in range(HEADS)]
    o_ref[0] = jnp.concatenate(parts, axis=0).T.astype(BF16)


def _lanes(tile, tq):
    return tile if tq == LANES else jnp.concatenate([tile] * (tq // LANES), axis=1)


def _causal_neg(tq):
    s_i = lax.broadcasted_iota(jnp.int32, (CH, tq), 0)
    t_i = lax.broadcasted_iota(jnp.int32, (CH, tq), 1)
    return jnp.where(s_i <= t_i, 0.0, NEG).astype(F32)


def _key_pos_bias(bias_ref, slopes):
    s_i = lax.broadcasted_iota(jnp.int32, (CH, LANES), 0).astype(F32)
    for h in range(HEADS):
        bias_ref[h] = s_i * float(slopes[h] * LOG2E)


def _attn_specs(mixer, seq):
    return [pl.BlockSpec((1, MIX_W, CH), lambda b, i: (b, mixer, i)),
            pl.BlockSpec((1, seq, MIX_W), lambda b, i: (b, 0, mixer)),
            pl.BlockSpec((1, seq // CH, MIX_W, CH), lambda b, i: (b, 0, mixer, 0))]


def _attn_scratch():
    return [pltpu.VMEM((HEADS, MIX_W, CH), BF16),
            pltpu.VMEM((HEADS, 1, CH), F32),
            pltpu.VMEM((HEADS, ACC_ROWS, CH), F32),
            pltpu.VMEM((2, HEADS, CH, CH), F32),
            pltpu.VMEM((2, HEADS, 1, CH), F32),
            pltpu.VMEM((2, HEADS, CH, CH), BF16),
            pltpu.VMEM((2, HEADS, 1, CH), F32)]


def _fox_kernel(qT_ref, k_ref, vT_ref, misc_ref, bf_ref, o_ref,
                cum_ref, *st, n_chunks):
    qm_ref, m_ref, acc_ref = st[:3]
    i = pl.program_id(1)

    @pl.when(i == 0)
    def _cumulative_gates():
        rr = lax.broadcasted_iota(jnp.int32, (CH, CH), 0)
        cc = lax.broadcasted_iota(jnp.int32, (CH, CH), 1)
        tri = jnp.where(cc <= rr, 1.0, 0.0).astype(BF16)
        erow = lax.broadcasted_iota(jnp.int32, (LANES, LANES), 0)

        def body(blk, carry):
            off = pl.multiple_of(blk * CH, CH)
            z = misc_ref[0, pl.ds(off, CH), :] + bf_ref[...]
            ls = jnp.minimum(z, 0.0) - jnp.log1p(jnp.exp(-jnp.abs(z)))
            csum = sum(_dot(tri, part) for part in _split3(ls))
            parts = _split3(csum)
            new = []
            for h in range(HEADS):
                sel = jnp.where(erow == FL_ROW + h, 1.0, 0.0).astype(BF16)
                cum = sum(_dot(part, sel) for part in parts) + carry[h]
                cum_ref[h, pl.ds(off, CH), :] = cum * LOG2E
                new.append(cum[CH - 1:CH, :])
            return tuple(new)

        lax.fori_loop(0, n_chunks, body, tuple(jnp.zeros((1, LANES), F32) for _ in range(HEADS)))

    _mask_heads(qT_ref, qm_ref)
    _init_state(m_ref, acc_ref)

    def logits(h, c, qk, _, diag, pen):
        off = pl.multiple_of(c * CH, CH)
        cum = cum_ref[h, pl.ds(off, CH), :]
        if diag:
            return qk - _lanes(cum, CH) + _causal_neg(CH)
        return qk - _lanes(cum - pen, CH)

    _attention(i, 0, k_ref, vT_ref, st, logits)
    _finish(o_ref, acc_ref)


def _fox(qT, k, vT, misc, b_forget_l):
    bsz, seq, _ = k.shape
    n_chunks = seq // CH
    bf = jnp.zeros((1, MISC_W), F32).at[0, FL_ROW:FL_ROW + HEADS].set(b_forget_l)
    return pl.pallas_call(
        functools.partial(_fox_kernel, n_chunks=n_chunks),
        out_shape=jax.ShapeDtypeStruct((bsz, seq, MIX_W), BF16),
        grid=(bsz, n_chunks),
        in_specs=_attn_specs(3, seq) + [
            pl.BlockSpec((1, seq, MISC_W), lambda b, i: (b, 0, 0)),
            pl.BlockSpec((1, MISC_W), lambda b, i: (0, 0))],
        out_specs=pl.BlockSpec((1, CH, MIX_W), lambda b, i: (b, i, 0)),
        scratch_shapes=[pltpu.VMEM((HEADS, seq, LANES), F32)] + _attn_scratch(),
        compiler_params=_params(2),
        name="fox",
    )(qT, k, vT, misc, bf)


def _moba_kernel(qT_ref, k_ref, vT_ref, kmean_ref, o_ref,
                 bias_ref, rowadd_ref, *st, slopes, n_blocks):
    qm_ref, m_ref, acc_ref = st[:3]
    i = pl.program_id(1)
    _mask_heads(qT_ref, qm_ref)
    _init_state(m_ref, acc_ref)

    @pl.when(jnp.logical_and(pl.program_id(0) == 0, i == 0))
    def _bias_table():
        s_f = lax.broadcasted_iota(jnp.int32, (CH, CH), 0).astype(F32)
        causal = _causal_neg(CH)
        for h in range(HEADS):
            pos = s_f * float(slopes[h] * LOG2E)
            bias_ref[0, h] = pos
            bias_ref[1, h] = pos + causal

    n_i = lax.broadcasted_iota(jnp.int32, (n_blocks, CH), 0)
    n_f = n_i.astype(F32)
    past = n_i < i
    kmean_parts = _split3(kmean_ref[0])
    for h in range(HEADS):
        gate = sum(_dot(part, qm_ref[h]) for part in kmean_parts)
        gate = jnp.where(past, gate, -jnp.inf)
        chosen = jnp.zeros((n_blocks, CH), F32)
        for _ in range(MOBA_TOPK):
            mx = jnp.max(gate, axis=0, keepdims=True)
            first = jnp.min(jnp.where(gate == mx, n_f, float(n_blocks)), axis=0, keepdims=True)
            pick = n_f == first
            chosen = jnp.where(pick, 1.0, chosen)
            gate = jnp.where(pick, -jnp.inf, gate)
        chosen = jnp.where(past, chosen, 0.0)
        blk_shift = n_f * float(slopes[h] * LOG2E * MOBA_BLOCK)
        rowadd_ref[h] = jnp.where(jnp.logical_or(chosen > 0.5, n_i == i), blk_shift, NEG)

    def logits(h, c, qk, _, diag, pen):
        row = rowadd_ref[h, pl.ds(c, 1), :]
        if diag:
            return qk + bias_ref[1, h] + row
        return qk + bias_ref[0, h] + (row + pen)

    _attention(i, 0, k_ref, vT_ref, st, logits)
    _finish(o_ref, acc_ref)


def _moba(qT, k, vT, kmean, slopes):
    bsz, seq, _ = k.shape
    n_blocks = seq // MOBA_BLOCK
    return pl.pallas_call(
        functools.partial(_moba_kernel, slopes=tuple(float(s) for s in slopes), n_blocks=n_blocks),
        out_shape=jax.ShapeDtypeStruct((bsz, seq, MIX_W), BF16),
        grid=(bsz, seq // CH),
        in_specs=_attn_specs(2, seq) + [
            pl.BlockSpec((1, n_blocks, MIX_W), lambda b, i: (b, 0, 2))],
        out_specs=pl.BlockSpec((1, CH, MIX_W), lambda b, i: (b, i, 0)),
        scratch_shapes=[pltpu.VMEM((2, HEADS, CH, CH), F32),
                        pltpu.VMEM((HEADS, n_blocks, CH), F32)] + _attn_scratch(),
        compiler_params=_params(2),
        name="moba",
    )(qT, k, vT, kmean)


DIL_SPAN = 2048 // CH + 1


def _dilated_kernel(qT_ref, k_ref, vT_ref, o_ref, table_ref, *st, slopes):
    qm_ref, m_ref, acc_ref = st[:3]
    b = pl.program_id(0)
    i = pl.program_id(1)

    @pl.when(jnp.logical_and(b == 0, i == 0))
    def _bias_table():
        s_i = lax.broadcasted_iota(jnp.int32, (CH, CH), 0)
        t_i = lax.broadcasted_iota(jnp.int32, (CH, CH), 1)
        for j in range(DIL_SPAN):
            d = t_i - s_i + j * CH
            ok = d >= 0
            mult = (jnp.where(jnp.logical_and(ok, d <= 128), 1.0, 0.0)
                    + jnp.where(jnp.logical_and(ok, jnp.logical_and(d <= 512, (d & 3) == 0)), 1.0, 0.0)
                    + jnp.where(jnp.logical_and(ok, jnp.logical_and(d <= 2048, (d & 15) == 0)), 1.0, 0.0))
            logm = jnp.where(mult > 0.5, jnp.log2(jnp.maximum(mult, 1.0)), NEG)
            df = d.astype(F32)
            for h in range(HEADS):
                table_ref[h, j] = logm - df * float(slopes[h] * LOG2E)
        for h in range(HEADS):
            table_ref[h, DIL_SPAN] = jnp.full((CH, CH), NEG, F32)

    _mask_heads(qT_ref, qm_ref)
    _init_state(m_ref, acc_ref)

    def logits(h, c, qk, _, diag, pen):
        if diag:
            return qk + table_ref[h, 0]
        return qk + table_ref[h, jnp.where(pen < 0.0, DIL_SPAN, i - c)]

    _attention(i, jnp.maximum(i - (DIL_SPAN - 1), 0), k_ref, vT_ref, st, logits)
    _finish(o_ref, acc_ref)


def _dilated(qT, k, vT, slopes):
    bsz, seq, _ = k.shape
    return pl.pallas_call(
        functools.partial(_dilated_kernel, slopes=tuple(float(s) for s in slopes)),
        out_shape=jax.ShapeDtypeStruct((bsz, seq, MIX_W), BF16),
        grid=(bsz, seq // CH),
        in_specs=_attn_specs(1, seq),
        out_specs=pl.BlockSpec((1, CH, MIX_W), lambda b, i: (b, i, 0)),
        scratch_shapes=[pltpu.VMEM((HEADS, DIL_SPAN + 1, CH, CH), F32)] + _attn_scratch(),
        compiler_params=_params(2),
        name="dilated",
    )(qT, k, vT)


def _dsa_kernel(qT_ref, k_ref, vT_ref, qiT_ref, misc_ref, miscT_ref, o_ref,
                bias_ref, qi_ref, key_ref, half_ref, seen_ref, *st, slopes):
    qm_ref, m_ref, acc_ref = st[:3]
    i = pl.program_id(1)
    tq = CH
    _mask_heads(qT_ref, qm_ref)
    _init_state(m_ref, acc_ref)
    _key_pos_bias(bias_ref, slopes)

    qi_all = qiT_ref[0]
    zpad = jnp.zeros((MISC_W - IDX_DIM, tq), BF16)
    for h in range(IDX_HEADS):
        qi_ref[h] = jnp.concatenate([qi_all[h * IDX_DIM:(h + 1) * IDX_DIM], zpad], axis=0)
    w_rows = miscT_ref[0, WI_ROW:WI_ROW + IDX_HEADS, :] * float(IDX_HEADS ** -0.5 * IDX_DIM ** -0.5)

    s_i = lax.broadcasted_iota(jnp.int32, (CH, tq), 0)
    t_i = lax.broadcasted_iota(jnp.int32, (CH, tq), 1)

    def score_chunk(c, diag):
        off = pl.multiple_of(c * CH, CH)
        ki = misc_ref[0, pl.ds(off, CH), :].astype(BF16)
        score = jnp.zeros((CH, tq), F32)
        for h in range(IDX_HEADS):
            rel = jnp.maximum(_dot(ki, qi_ref[h]), 0.0)
            score = score + rel * w_rows[h:h + 1, :]
        bits = lax.bitcast_convert_type(score, jnp.int32)
        key = bits ^ ((bits >> 31) & 0x7FFFFFFF)
        key = jnp.where(key == -1, 0, key)
        if diag:
            key = jnp.where(s_i <= t_i, key, INT_MIN)
        key_ref[pl.ds(off, CH), :] = key
        half_ref[pl.ds(off, CH), :] = (key >> 16).astype(jnp.int16)

    score_chunk(i, True)

    def score_pair(t, carry):
        score_chunk(2 * t, False)
        score_chunk(jnp.minimum(2 * t + 1, i - 1), False)
        return carry

    lax.fori_loop(0, (i + 1) // 2, score_pair, 0)
    half_ref[pl.ds(pl.multiple_of((i + 1) * CH, CH), CH), :] = jnp.full((CH, tq), -32768, jnp.int16)

    def count(ref, pred):
        def body(c, cnt):
            off = pl.multiple_of(c * CH, CH)
            hit = jnp.where(pred(ref[pl.ds(off, CH), :]), 1, 0)
            return cnt + jnp.sum(hit.reshape(CH // 8, 8, tq), axis=0)
        cnt8 = lax.fori_loop(0, i + 1, body, jnp.zeros((8, tq), jnp.int32))
        return jnp.sum(cnt8, axis=0, keepdims=True)

    def count_half(cand):
        cand16 = cand.astype(jnp.int16)
        rows = 16

        def body(c2, cnt):
            off = pl.multiple_of(c2 * (2 * CH), 2 * CH)
            hit = jnp.where(half_ref[pl.ds(off, 2 * CH), :] >= cand16, jnp.int16(1), jnp.int16(0))
            hit = hit.reshape(2 * CH // rows, rows, tq)
            for r in range(2 * CH // rows):
                cnt = cnt + hit[r]
            return cnt

        cnt16 = lax.fori_loop(0, i // 2 + 1, body, jnp.zeros((rows, tq), jnp.int16))
        return jnp.sum(cnt16.astype(jnp.int32), axis=0, keepdims=True)

    def search_half(need, n_all):
        def bit_step(n, state):
            v, n_ge = state
            cand = v + jnp.left_shift(jnp.int32(1), 15 - n)
            cnt = count_half(cand)
            ok = cnt >= need
            return jnp.where(ok, cand, v), jnp.where(ok, cnt, n_ge)

        return lax.fori_loop(0, 16, bit_step, (jnp.full((1, tq), -32768, jnp.int32), n_all))

    n_all = jnp.zeros((1, tq), jnp.int32) + (i + 1) * CH
    hi, n_ge_hi = search_half(TOPK_KEYS, n_all)
    n_gt_hi = jnp.where(hi == 32767, 0, count_half(jnp.minimum(hi + 1, 32767)))
    need_lo = TOPK_KEYS - n_gt_hi

    def lower_halves(c, _):
        off = pl.multiple_of(c * CH, CH)
        key = key_ref[pl.ds(off, CH), :]
        lo = (key & 0xFFFF) - 32768
        half_ref[pl.ds(off, CH), :] = jnp.where((key >> 16) == hi, lo, -32768).astype(jnp.int16)
        return 0

    lax.fori_loop(0, i + 1, lower_halves, 0)
    lo, n_ge_lo = search_half(need_lo, n_ge_hi - n_gt_hi)
    thr = hi * 65536 + (lo + 32768)
    n_ge = n_gt_hi + n_ge_lo
    tied = jnp.logical_and(n_ge > TOPK_KEYS, thr != INT_MIN)
    any_tied = jnp.max(jnp.where(tied, 1.0, 0.0)) > 0.5

    def logits(h, c, qk, neg, diag, pen):
        pos = bias_ref[h] + (c.astype(F32) * float(slopes[h] * LOG2E * CH) + pen)
        return qk + _lanes(pos, tq) + neg

    @pl.when(jnp.logical_not(any_tied))
    def _no_ties():
        thr_lo = jnp.where(thr == INT_MIN, INT_MIN + 1, thr)

        def selection(c):
            off = pl.multiple_of(c * CH, CH)
            return jnp.where(key_ref[pl.ds(off, CH), :] >= thr_lo, 0.0, NEG)

        _attention(i, 0, k_ref, vT_ref, st, logits, selection, diag_first=False)

    @pl.when(any_tied)
    def _ties():
        n_gt = count(key_ref, lambda key: key > thr)
        n_take = jnp.where(thr == INT_MIN, 0, TOPK_KEYS - n_gt).astype(F32)
        rr = lax.broadcasted_iota(jnp.int32, (CH, CH), 0)
        cc = lax.broadcasted_iota(jnp.int32, (CH, CH), 1)
        tri = jnp.where(cc <= rr, 1.0, 0.0).astype(BF16)
        seen_ref[...] = jnp.zeros(seen_ref.shape, F32)

        def selection(c):
            off = pl.multiple_of(c * CH, CH)
            key = key_ref[pl.ds(off, CH), :]
            eq = key == thr
            rank = _dot(tri, jnp.where(eq, 1.0, 0.0).astype(BF16)) + seen_ref[...]
            seen_ref[...] = rank[CH - 1:CH, :]
            take = jnp.logical_or(key > thr, jnp.logical_and(eq, rank <= n_take))
            return jnp.where(take, 0.0, NEG)

        _attention(i, 0, k_ref, vT_ref, st, logits, selection, diag_first=False)

    _finish(o_ref, acc_ref)


def _dsa(qT, k, vT, qiT, misc, miscT, slopes):
    bsz, seq, _ = k.shape
    nqi = IDX_HEADS * IDX_DIM
    return pl.pallas_call(
        functools.partial(_dsa_kernel, slopes=tuple(float(s) for s in slopes)),
        out_shape=jax.ShapeDtypeStruct((bsz, seq, MIX_W), BF16),
        grid=(bsz, seq // CH),
        in_specs=_attn_specs(0, seq) + [
            pl.BlockSpec((1, nqi, CH), lambda b, i: (b, 0, i)),
            pl.BlockSpec((1, seq, MISC_W), lambda b, i: (b, 0, 0)),
            pl.BlockSpec((1, MISC_W, CH), lambda b, i: (b, 0, i))],
        out_specs=pl.BlockSpec((1, CH, MIX_W), lambda b, i: (b, i, 0)),
        scratch_shapes=[pltpu.VMEM((HEADS, CH, LANES), F32),
                        pltpu.VMEM((IDX_HEADS, MISC_W, CH), BF16),
                        pltpu.VMEM((seq, CH), jnp.int32),
                        pltpu.VMEM((seq + CH, CH), jnp.int16),
                        pltpu.VMEM((1, CH), F32)] + _attn_scratch(),
        compiler_params=_params(2),
        name="dsa",
    )(qT, k, vT, qiT, misc, miscT)


def _merge_kernel(x_ref, ada_ref, oa_ref, ob_ref, oc_ref, od_ref, wmg_ref, wbr_ref, wout_ref,
                  g_ref, b_ref, out_ref):
    x = x_ref[0]
    sh = ada_ref[0, 0:1, :]
    sc = ada_ref[0, 1:2, :]
    g1 = ada_ref[0, 2:3, :]
    h = (x * (1.0 + sc) + sh).astype(BF16)
    mixed = None
    for m, o_ref in enumerate((oa_ref, ob_ref, oc_ref, od_ref)):
        gate = jax.nn.sigmoid(_dot(h, wmg_ref[m]))
        term = gate * _dot(o_ref[0], wbr_ref[m])
        mixed = term if mixed is None else mixed + term
    y = _dot(mixed.astype(BF16), wout_ref[...])
    out_ref[0] = _layer_norm(DN_ALPHA * x + g1 * y, g_ref[...], b_ref[...])


def _merge(x, ada, outs, w_mg, w_br, w_out, ln_g, ln_b, tm=512):
    bsz, seq, d = x.shape
    o_spec = pl.BlockSpec((1, tm, MIX_W), lambda b, i: (b, i, 0))
    return pl.pallas_call(
        _merge_kernel,
        out_shape=jax.ShapeDtypeStruct((bsz, seq, d), F32),
        grid=(bsz, seq // tm),
        in_specs=[pl.BlockSpec((1, tm, d), lambda b, i: (b, i, 0)),
                  pl.BlockSpec((1, 6, d), lambda b, i: (b, 0, 0)),
                  o_spec, o_spec, o_spec, o_spec,
                  pl.BlockSpec((N_MIX, d, d), lambda b, i: (0, 0, 0)),
                  pl.BlockSpec((N_MIX, MIX_W, d), lambda b, i: (0, 0, 0)),
                  pl.BlockSpec((d, d), lambda b, i: (0, 0)),
                  pl.BlockSpec((1, d), lambda b, i: (0, 0)),
                  pl.BlockSpec((1, d), lambda b, i: (0, 0))],
        out_specs=pl.BlockSpec((1, tm, d), lambda b, i: (b, i, 0)),
        compiler_params=_params(2),
        name="merge",
    )(x, ada, *outs, w_mg.astype(BF16), w_br.astype(BF16), w_out.astype(BF16),
      ln_g.reshape(1, d), ln_b.reshape(1, d))


def _route(logits_t, bias_col):
    per_group = N_EXPERTS // N_GROUPS
    scores = jax.nn.sigmoid(logits_t)
    biased = scores + bias_col
    s_rows = [scores[e:e + 1] for e in range(N_EXPERTS)]
    b_rows = [biased[e:e + 1] for e in range(N_EXPERTS)]
    best_g = None
    for g in range(N_GROUPS):
        r = b_rows[g * per_group:(g + 1) * per_group]
        gs = None
        for a in range(per_group):
            for b in range(a + 1, per_group):
                pair = r[a] + r[b]
                gs = pair if gs is None else jnp.maximum(gs, pair)
        if best_g is None:
            best_g, best_v = jnp.zeros_like(gs, dtype=jnp.int32), gs
        else:
            better = gs > best_v
            best_g = jnp.where(better, g, best_g)
            best_v = jnp.maximum(best_v, gs)
    masked = [jnp.where(best_g == e // per_group, b_rows[e], -jnp.inf) for e in range(N_EXPERTS)]

    def argmax_first(rows):
        idx, val = jnp.zeros_like(best_g), rows[0]
        for e in range(1, N_EXPERTS):
            better = rows[e] > val
            idx = jnp.where(better, e, idx)
            val = jnp.maximum(val, rows[e])
        return idx

    e1 = argmax_first(masked)
    e2 = argmax_first([jnp.where(e1 == e, -jnp.inf, masked[e]) for e in range(N_EXPERTS)])
    s1 = sum(jnp.where(e1 == e, s_rows[e], 0.0) for e in range(N_EXPERTS))
    s2 = sum(jnp.where(e2 == e, s_rows[e], 0.0) for e in range(N_EXPERTS))
    tot = s1 + s2
    w1, w2 = s1 / tot, s2 / tot
    rows = [jnp.where(e1 == e, w1, 0.0) + jnp.where(e2 == e, w2, 0.0) for e in range(N_EXPERTS)]
    return jnp.concatenate(rows, axis=0), best_g


def _moe_kernel(x_ref, ada_ref, wr_ref, br_ref, wg_ref, wu_ref, wd_ref, g_ref, b_ref, out_ref,
                perm_ref, xs_ref, combs_ref, ys_ref, sub_ref, *, tm, rows):
    e = pl.program_id(1)
    per_group = N_EXPERTS // N_GROUPS

    @pl.when(e == 0)
    def _route_and_sort():
        x = x_ref[...]
        h2 = x * (1.0 + ada_ref[0, 4:5, :]) + ada_ref[0, 3:4, :]
        h_hi = h2.astype(BF16)
        h_lo = (h2 - h_hi.astype(F32)).astype(BF16)
        first = _dot(h_hi, wr_ref[...])
        logits = first[:, 0:LANES] + first[:, LANES:] + _dot(h_lo, wr_ref[:, 0:LANES])
        logits_t = logits.T[0:N_EXPERTS]
        comb_t, best_g = _route(logits_t, br_ref[...])
        member = [jnp.where(best_g == g, 1.0, 0.0) for g in range(N_GROUPS)]
        grp = jnp.concatenate(member + [jnp.zeros((8 - N_GROUPS, tm), F32)], axis=0).astype(BF16)
        s_i = lax.broadcasted_iota(jnp.int32, (tm, tm), 0)
        t_i = lax.broadcasted_iota(jnp.int32, (tm, tm), 1)
        rank = _dot(grp, jnp.where(s_i <= t_i, 1.0, 0.0).astype(BF16))
        pos = jnp.zeros((1, tm), F32)
        start = jnp.zeros((1, 1), F32)
        for g in range(N_GROUPS):
            cap = jnp.ceil(rank[g:g + 1, tm - 1:tm] * (1.0 / MOE_SUB)) * MOE_SUB
            pos = pos + member[g] * (start + rank[g:g + 1] - 1.0)
            sub_ref[2 * g] = (jnp.sum(start) * (1.0 / MOE_SUB)).astype(jnp.int32)
            sub_ref[2 * g + 1] = (jnp.sum(cap) * (1.0 / MOE_SUB)).astype(jnp.int32)
            start = start + cap
        r_f = lax.broadcasted_iota(jnp.int32, (rows, tm), 0).astype(F32)
        perm = jnp.where(r_f == pos, 1.0, 0.0).astype(BF16)
        perm_ref[...] = perm
        xs_ref[...] = _dot(perm, h_hi).astype(BF16)
        comb = jnp.concatenate([comb_t, jnp.zeros((LANES - N_EXPERTS, tm), F32)], axis=0).T
        c_hi = comb.astype(BF16)
        c_lo = (comb - c_hi.astype(F32)).astype(BF16)
        combs_ref[...] = _dot(perm, c_hi) + _dot(perm, c_lo)
        ys_ref[...] = jnp.zeros(ys_ref.shape, F32)

    g = e // per_group
    lane = lax.broadcasted_iota(jnp.int32, (MOE_SUB, LANES), 1)

    def sub_tile(j, carry):
        r0 = pl.multiple_of((sub_ref[2 * g] + j) * MOE_SUB, MOE_SUB)
        xj = xs_ref[pl.ds(r0, MOE_SUB), :]
        w_e = jnp.sum(jnp.where(lane == e, combs_ref[pl.ds(r0, MOE_SUB), :], 0.0), axis=1, keepdims=True)
        hid = (jax.nn.silu(_dot(xj, wg_ref[0, 0])) * _dot(xj, wu_ref[0, 0]) * w_e).astype(BF16)
        ys_ref[pl.ds(r0, MOE_SUB), :] += _dot(hid, wd_ref[0, 0])
        return carry

    lax.fori_loop(0, sub_ref[2 * g + 1], sub_tile, 0)

    @pl.when(e == N_EXPERTS - 1)
    def _unsort_and_norm():
        y = lax.dot_general(perm_ref[...], ys_ref[...].astype(BF16), (((0,), (0,)), ((), ())),
                            preferred_element_type=F32)
        z = DN_ALPHA * x_ref[...] + ada_ref[0, 5:6, :] * y
        out_ref[...] = _layer_norm(z, g_ref[...], b_ref[...])


def _moe(x, ada_rows, w_router, b_router, w_gate, w_up, w_down, ln_g, ln_b, batch_len, tm=1024):
    n, d = x.shape
    per_b = batch_len // tm
    rows = tm + N_GROUPS * MOE_SUB
    wr32 = jnp.pad(w_router, ((0, 0), (0, LANES - N_EXPERTS)))
    wr_hi = wr32.astype(BF16)
    wr = jnp.concatenate([wr_hi, (wr32 - wr_hi.astype(F32)).astype(BF16)], axis=1)
    br = b_router.reshape(N_EXPERTS, 1)
    return pl.pallas_call(
        functools.partial(_moe_kernel, tm=tm, rows=rows),
        out_shape=jax.ShapeDtypeStruct((n, d), F32),
        grid=(n // tm, N_EXPERTS),
        in_specs=[pl.BlockSpec((tm, d), lambda i, e: (i, 0)),
                  pl.BlockSpec((1, 6, d), lambda i, e: (i // per_b, 0, 0)),
                  pl.BlockSpec((d, 2 * LANES), lambda i, e: (0, 0)),
                  pl.BlockSpec((N_EXPERTS, 1), lambda i, e: (0, 0)),
                  pl.BlockSpec((1, 1, d, D_EXPERT), lambda i, e: (0, e, 0, 0)),
                  pl.BlockSpec((1, 1, d, D_EXPERT), lambda i, e: (0, e, 0, 0)),
                  pl.BlockSpec((1, 1, D_EXPERT, d), lambda i, e: (0, e, 0, 0)),
                  pl.BlockSpec((1, d), lambda i, e: (0, 0)),
                  pl.BlockSpec((1, d), lambda i, e: (0, 0))],
        out_specs=pl.BlockSpec((tm, d), lambda i, e: (i, 0)),
        scratch_shapes=[pltpu.VMEM((rows, tm), BF16),
                        pltpu.VMEM((rows, d), BF16),
                        pltpu.VMEM((rows, LANES), F32),
                        pltpu.VMEM((rows, d), F32),
                        pltpu.SMEM((2 * N_GROUPS,), jnp.int32)],
        compiler_params=_params(2),
        name="moe",
    )(x, ada_rows, wr, br, w_gate[None].astype(BF16), w_up[None].astype(BF16), w_down[None].astype(BF16),
      ln_g.reshape(1, d), ln_b.reshape(1, d))


def kernel(x, c, w_in, b_forget, w_branch, w_merge_gate, w_out, w_ada, b_ada, ln1_g, ln1_b, ln2_g, ln2_b,
           w_router, b_router, w_exp_gate, w_exp_up, w_exp_down):
    bsz, seq, d = x.shape
    slopes = _alibi_slopes()
    ada_all = _ada(c, w_ada, b_ada).reshape(DEPTH, bsz, 6, d)
    for l in range(DEPTH):
        ada = ada_all[l]
        k, kmean, misc, qT, vT, qiT, miscT = _proj(x, ada, w_in[l])
        o_a = _dsa(qT, k, vT, qiT, misc, miscT, slopes[0])
        o_b = _dilated(qT, k, vT, slopes[1])
        o_c = _moba(qT, k, vT, kmean, slopes[2])
        o_d = _fox(qT, k, vT, misc, b_forget[l])
        x = _merge(x, ada, (o_a, o_b, o_c, o_d), w_merge_gate[l], w_branch[l], w_out[l], ln1_g[l], ln1_b[l])
        x = _moe(x.reshape(bsz * seq, d), ada, w_router, b_router, w_exp_gate[l], w_exp_up[l], w_exp_down[l],
                 ln2_g[l], ln2_b[l], seq).reshape(bsz, seq, d)
    return x
```

```python
import functools

import numpy as np
import jax
import jax.numpy as jnp
from jax import lax
from jax.experimental import pallas as pl
from jax.experimental.pallas import tpu as pltpu

D_MODEL = 1024
HEAD_DIM = 64
HEADS = 4
MIX_W = HEADS * HEAD_DIM
N_MIX = 4
IDX_HEADS = 8
IDX_DIM = 64
TOPK_KEYS = 256
MOBA_BLOCK = 256
MOBA_TOPK = 3
N_EXPERTS = 16
N_GROUPS = 4
D_EXPERT = 512
DEPTH = 2
DN_ALPHA = (2 * DEPTH) ** 0.25
LN_EPS = 1e-5
IN_COLS = 3 * N_MIX * MIX_W + IDX_HEADS * IDX_DIM + IDX_DIM + IDX_HEADS + HEADS

CH = 256
ACC_ROWS = HEAD_DIM + 16
MOE_SUB = 128
LANES = 128
MISC_W = 128
WI_ROW = IDX_DIM
FL_ROW = IDX_DIM + IDX_HEADS
LOG2E = 1.4426950408889634
NEG = -1e30
QSCALE = HEAD_DIM ** -0.5 * LOG2E
INT_MIN = -(2 ** 31)
VMEM_LIMIT = 56 * 1024 * 1024

F32 = jnp.float32
BF16 = jnp.bfloat16
HI = lax.Precision.HIGHEST
NT = (((1,), (1,)), ((), ()))


def _alibi_slopes():
    n = 3 * HEADS
    s = 2.0 ** (-8.0 * np.arange(1, n + 1) / n)
    return s.reshape(HEADS, 3).T


def _dot(a, b):
    return jnp.dot(a, b, preferred_element_type=F32)


def _dot_hi(a, b):
    return jnp.dot(a, b, preferred_element_type=F32, precision=HI)


def _bit_transpose32(words):
    a = list(words)
    j, m = 16, 0x0000FFFF
    while j:
        k = 0
        while k < 32:
            t = (a[k] ^ (a[k + j] >> j)) & m
            a[k] = a[k] ^ t
            a[k + j] = a[k + j] ^ (t << j)
            k = (k + j + 1) & ~j
        j >>= 1
        m = (m ^ (m << j)) & 0xFFFFFFFF
    return a


def _split3(x):
    x1 = x.astype(BF16)
    r1 = x - x1.astype(F32)
    x2 = r1.astype(BF16)
    x3 = (r1 - x2.astype(F32)).astype(BF16)
    return x1, x2, x3


def _params(n_axes):
    return pltpu.CompilerParams(dimension_semantics=("arbitrary",) * n_axes,
                                vmem_limit_bytes=VMEM_LIMIT)


def _layer_norm(z, g, b):
    mu = jnp.mean(z, axis=-1, keepdims=True)
    var = jnp.mean(jnp.square(z - mu), axis=-1, keepdims=True)
    return (z - mu) * lax.rsqrt(var + LN_EPS) * g + b


def _ada_kernel(c_ref, w_ref, b_ref, o_ref):
    o_ref[0] = _dot_hi(c_ref[...], w_ref[0]) + b_ref[0]


def _ada(c, w_ada, b_ada):
    depth, d, n = w_ada.shape
    bsz = c.shape[0]
    tn = D_MODEL
    return pl.pallas_call(
        _ada_kernel,
        out_shape=jax.ShapeDtypeStruct((depth, bsz, n), F32),
        grid=(depth, n // tn),
        in_specs=[pl.BlockSpec((bsz, d), lambda l, j: (0, 0)),
                  pl.BlockSpec((1, d, tn), lambda l, j: (l, 0, j)),
                  pl.BlockSpec((1, 1, tn), lambda l, j: (l, 0, j))],
        out_specs=pl.BlockSpec((1, bsz, tn), lambda l, j: (l, 0, j)),
        compiler_params=_params(2),
        name="ada",
    )(c, w_ada, b_ada.reshape(depth, 1, n))


def _proj_kernel(x_ref, ada_ref, wk_ref, wm_ref, wt_ref, wmt_ref,
                 k_ref, kmean_ref, misc_ref, qT_ref, vT_ref, qiT_ref, miscT_ref, *, tm):
    x = x_ref[0]
    sh = ada_ref[0, 0:1, :]
    sc = ada_ref[0, 1:2, :]
    h = (x * (1.0 + sc) + sh).astype(BF16)
    kf = _dot(h, wk_ref[...])
    k_ref[0] = kf.astype(BF16)
    for g in range(tm // MOBA_BLOCK):
        kmean_ref[0, 0, g:g + 1, :] = jnp.mean(kf[g * MOBA_BLOCK:(g + 1) * MOBA_BLOCK], axis=0, keepdims=True)
    misc_ref[0] = _dot(h, wm_ref[...])
    t = lax.dot_general(wt_ref[...], h, NT, preferred_element_type=F32)
    nq = N_MIX * MIX_W
    qT_ref[0] = (t[0:nq] * QSCALE).astype(BF16)
    for g in range(tm // CH):
        vT_ref[0, g] = t[nq:2 * nq, g * CH:(g + 1) * CH].astype(BF16)
    qiT_ref[0] = t[2 * nq:].astype(BF16)
    miscT_ref[0] = lax.dot_general(wmt_ref[...], h, NT, preferred_element_type=F32)


def _proj(x, ada, w_in, tm=512):
    bsz, seq, d = x.shape
    nq = N_MIX * MIX_W
    nqi = IDX_HEADS * IDX_DIM
    w = w_in.astype(BF16)
    wk = w[:, nq:2 * nq]
    wm = jnp.pad(w[:, 3 * nq + nqi:], ((0, 0), (0, MISC_W - (IN_COLS - 3 * nq - nqi))))
    wt = jnp.concatenate([w[:, 0:nq], w[:, 2 * nq:3 * nq], w[:, 3 * nq:3 * nq + nqi]], axis=1).T
    wmt = wm.T
    nt = seq // tm
    full = lambda b, i: (0, 0)
    outs = pl.pallas_call(
        functools.partial(_proj_kernel, tm=tm),
        out_shape=(jax.ShapeDtypeStruct((bsz, seq, nq), BF16),
                   jax.ShapeDtypeStruct((bsz, nt, tm // MOBA_BLOCK, nq), F32),
                   jax.ShapeDtypeStruct((bsz, seq, MISC_W), F32),
                   jax.ShapeDtypeStruct((bsz, nq, seq), BF16),
                   jax.ShapeDtypeStruct((bsz, seq // CH, nq, CH), BF16),
                   jax.ShapeDtypeStruct((bsz, nqi, seq), BF16),
                   jax.ShapeDtypeStruct((bsz, MISC_W, seq), F32)),
        grid=(bsz, nt),
        in_specs=[pl.BlockSpec((1, tm, d), lambda b, i: (b, i, 0)),
                  pl.BlockSpec((1, 6, d), lambda b, i: (b, 0, 0)),
                  pl.BlockSpec(wk.shape, full),
                  pl.BlockSpec(wm.shape, full),
                  pl.BlockSpec(wt.shape, full),
                  pl.BlockSpec(wmt.shape, full)],
        out_specs=(pl.BlockSpec((1, tm, nq), lambda b, i: (b, i, 0)),
                   pl.BlockSpec((1, 1, tm // MOBA_BLOCK, nq), lambda b, i: (b, i, 0, 0)),
                   pl.BlockSpec((1, tm, MISC_W), lambda b, i: (b, i, 0)),
                   pl.BlockSpec((1, nq, tm), lambda b, i: (b, 0, i)),
                   pl.BlockSpec((1, tm // CH, nq, CH), lambda b, i: (b, i, 0, 0)),
                   pl.BlockSpec((1, nqi, tm), lambda b, i: (b, 0, i)),
                   pl.BlockSpec((1, MISC_W, tm), lambda b, i: (b, 0, i))),
        compiler_params=_params(2),
        name="proj",
    )(x, ada, wk, wm, wt, wmt)
    k, kmean, misc, qT, vT, qiT, miscT = outs
    return k, kmean.reshape(bsz, seq // MOBA_BLOCK, nq), misc, qT, vT, qiT, miscT


def _mask_heads(qT_ref, qm_ref):
    q = qT_ref[0]
    rowh = lax.broadcasted_iota(jnp.int32, q.shape, 0) // HEAD_DIM
    for h in range(HEADS):
        qm_ref[h] = jnp.where(rowh == h, q, jnp.zeros_like(q))


def _init_state(m_ref, acc_ref):
    m_ref[...] = jnp.full(m_ref.shape, NEG, F32)
    acc_ref[...] = jnp.zeros(acc_ref.shape, F32)


def _attention(i, first, k_ref, vT_ref, st, logits_fn, chunk_ctx=None, diag_first=True):
    qm_ref, m_ref, acc_ref, s_ref, mx_ref, p_ref, alpha_ref = st
    n_chunks = i - first + 1
    ones = jnp.ones((ACC_ROWS - HEAD_DIM, CH), BF16)

    def chunk_of(n):
        n = jnp.clip(n, 0, n_chunks - 1)
        if diag_first:
            return jnp.where(n == 0, i, first + n - 1)
        return first + n

    def logits_head(h, slot, c, kc, ctx, diag, pen):
        x = logits_fn(h, c, _dot(kc, qm_ref[h]), ctx, diag, pen)
        s_ref[slot, h] = x
        mx_ref[slot, h] = jnp.max(x, axis=0, keepdims=True)

    def softmax_head(h, slot):
        m_old = m_ref[h]
        m_new = jnp.maximum(m_old, mx_ref[slot, h])
        alpha_ref[slot, h] = jnp.exp2(m_old - m_new)
        p_ref[slot, h] = jnp.exp2(s_ref[slot, h] - m_new).astype(BF16)
        m_ref[h] = m_new

    def pv_head(h, slot, vc):
        v1 = jnp.concatenate([vc[h * HEAD_DIM:(h + 1) * HEAD_DIM, :], ones], axis=0)
        acc_ref[h] = alpha_ref[slot, h] * acc_ref[h] + _dot(v1, p_ref[slot, h])

    def step(n, a, diag):
        c = chunk_of(n + 1)
        pen = jnp.where(n + 1 < n_chunks, 0.0, NEG)
        kc = k_ref[0, pl.ds(pl.multiple_of(c * CH, CH), CH), :]
        vc = vT_ref[0, chunk_of(n - 1)]
        ctx = chunk_ctx(c) if chunk_ctx is not None else None
        for h in range(HEADS):
            softmax_head(h, a)
        for h in range(HEADS):
            pv_head(h, 1 - a, vc)
        for h in range(HEADS):
            logits_head(h, 1 - a, c, kc, ctx, diag, pen)

    p_ref[1] = jnp.zeros(p_ref.shape[1:], BF16)
    alpha_ref[1] = jnp.ones(alpha_ref.shape[1:], F32)
    c0 = jnp.asarray(i if diag_first else first, jnp.int32)
    k0 = k_ref[0, pl.ds(pl.multiple_of(c0 * CH, CH), CH), :]
    ctx0 = chunk_ctx(c0) if chunk_ctx is not None else None
    for h in range(HEADS):
        logits_head(h, 0, c0, k0, ctx0, True if diag_first else None, 0.0)
    later = False if diag_first else None

    def pair(t, carry):
        step(2 * t, 0, later)
        step(2 * t + 1, 1, later)
        return carry

    n_trips = (n_chunks + 1) // 2
    lax.fori_loop(0, n_trips, pair, 0)
    v_last = vT_ref[0, chunk_of(2 * n_trips - 1)]
    for h in range(HEADS):
        pv_head(h, 1, v_last)


def _finish(o_ref, acc_ref):
    parts = [acc_ref[h, 0:HEAD_DIM, :] / acc_ref[h, HEAD_DIM:HEAD_DIM + 1, :] for h in range(HEADS)]
    o_ref[0] = jnp.concatenate(parts, axis=0).T.astype(BF16)


def _lanes(tile, tq):
    return tile if tq == LANES else jnp.concatenate([tile] * (tq // LANES), axis=1)


def _causal_neg(tq):
    s_i = lax.broadcasted_iota(jnp.int32, (CH, tq), 0)
    t_i = lax.broadcasted_iota(jnp.int32, (CH, tq), 1)
    return jnp.where(s_i <= t_i, 0.0, NEG).astype(F32)


def _key_pos_bias(bias_ref, slopes):
    s_i = lax.broadcasted_iota(jnp.int32, (CH, LANES), 0).astype(F32)
    for h in range(HEADS):
        bias_ref[h] = s_i * float(slopes[h] * LOG2E)


def _attn_specs(mixer, seq):
    return [pl.BlockSpec((1, MIX_W, CH), lambda b, i: (b, mixer, i)),
            pl.BlockSpec((1, seq, MIX_W), lambda b, i: (b, 0, mixer)),
            pl.BlockSpec((1, seq // CH, MIX_W, CH), lambda b, i: (b, 0, mixer, 0))]


def _attn_scratch():
    return [pltpu.VMEM((HEADS, MIX_W, CH), BF16),
            pltpu.VMEM((HEADS, 1, CH), F32),
            pltpu.VMEM((HEADS, ACC_ROWS, CH), F32),
            pltpu.VMEM((2, HEADS, CH, CH), F32),
            pltpu.VMEM((2, HEADS, 1, CH), F32),
            pltpu.VMEM((2, HEADS, CH, CH), BF16),
            pltpu.VMEM((2, HEADS, 1, CH), F32)]


def _fox_kernel(qT_ref, k_ref, vT_ref, misc_ref, bf_ref, o_ref,
                cum_ref, *st, n_chunks):
    qm_ref, m_ref, acc_ref = st[:3]
    i = pl.program_id(1)

    @pl.when(i == 0)
    def _cumulative_gates():
        rr = lax.broadcasted_iota(jnp.int32, (CH, CH), 0)
        cc = lax.broadcasted_iota(jnp.int32, (CH, CH), 1)
        tri = jnp.where(cc <= rr, 1.0, 0.0).astype(BF16)
        erow = lax.broadcasted_iota(jnp.int32, (LANES, LANES), 0)

        def body(blk, carry):
            off = pl.multiple_of(blk * CH, CH)
            z = misc_ref[0, pl.ds(off, CH), :] + bf_ref[...]
            ls = jnp.minimum(z, 0.0) - jnp.log1p(jnp.exp(-jnp.abs(z)))
            csum = sum(_dot(tri, part) for part in _split3(ls))
            parts = _split3(csum)
            new = []
            for h in range(HEADS):
                sel = jnp.where(erow == FL_ROW + h, 1.0, 0.0).astype(BF16)
                cum = sum(_dot(part, sel) for part in parts) + carry[h]
                cum_ref[h, pl.ds(off, CH), :] = cum * LOG2E
                new.append(cum[CH - 1:CH, :])
            return tuple(new)

        lax.fori_loop(0, n_chunks, body, tuple(jnp.zeros((1, LANES), F32) for _ in range(HEADS)))

    _mask_heads(qT_ref, qm_ref)
    _init_state(m_ref, acc_ref)

    def logits(h, c, qk, _, diag, pen):
        off = pl.multiple_of(c * CH, CH)
        cum = cum_ref[h, pl.ds(off, CH), :]
        if diag:
            return qk - _lanes(cum, CH) + _causal_neg(CH)
        return qk - _lanes(cum - pen, CH)

    _attention(i, 0, k_ref, vT_ref, st, logits)
    _finish(o_ref, acc_ref)


def _fox(qT, k, vT, misc, b_forget_l):
    bsz, seq, _ = k.shape
    n_chunks = seq // CH
    bf = jnp.zeros((1, MISC_W), F32).at[0, FL_ROW:FL_ROW + HEADS].set(b_forget_l)
    return pl.pallas_call(
        functools.partial(_fox_kernel, n_chunks=n_chunks),
        out_shape=jax.ShapeDtypeStruct((bsz, seq, MIX_W), BF16),
        grid=(bsz, n_chunks),
        in_specs=_attn_specs(3, seq) + [
            pl.BlockSpec((1, seq, MISC_W), lambda b, i: (b, 0, 0)),
            pl.BlockSpec((1, MISC_W), lambda b, i: (0, 0))],
        out_specs=pl.BlockSpec((1, CH, MIX_W), lambda b, i: (b, i, 0)),
        scratch_shapes=[pltpu.VMEM((HEADS, seq, LANES), F32)] + _attn_scratch(),
        compiler_params=_params(2),
        name="fox",
    )(qT, k, vT, misc, bf)


def _moba_kernel(qT_ref, k_ref, vT_ref, kmean_ref, o_ref,
                 bias_ref, rowadd_ref, *st, slopes, n_blocks):
    qm_ref, m_ref, acc_ref = st[:3]
    i = pl.program_id(1)
    _mask_heads(qT_ref, qm_ref)
    _init_state(m_ref, acc_ref)

    @pl.when(jnp.logical_and(pl.program_id(0) == 0, i == 0))
    def _bias_table():
        s_f = lax.broadcasted_iota(jnp.int32, (CH, CH), 0).astype(F32)
        causal = _causal_neg(CH)
        for h in range(HEADS):
            pos = s_f * float(slopes[h] * LOG2E)
            bias_ref[0, h] = pos
            bias_ref[1, h] = pos + causal

    n_i = lax.broadcasted_iota(jnp.int32, (n_blocks, CH), 0)
    n_f = n_i.astype(F32)
    past = n_i < i
    kmean_parts = _split3(kmean_ref[0])
    for h in range(HEADS):
        gate = sum(_dot(part, qm_ref[h]) for part in kmean_parts)
        gate = jnp.where(past, gate, -jnp.inf)
        chosen = jnp.zeros((n_blocks, CH), F32)
        for _ in range(MOBA_TOPK):
            mx = jnp.max(gate, axis=0, keepdims=True)
            first = jnp.min(jnp.where(gate == mx, n_f, float(n_blocks)), axis=0, keepdims=True)
            pick = n_f == first
            chosen = jnp.where(pick, 1.0, chosen)
            gate = jnp.where(pick, -jnp.inf, gate)
        chosen = jnp.where(past, chosen, 0.0)
        blk_shift = n_f * float(slopes[h] * LOG2E * MOBA_BLOCK)
        rowadd_ref[h] = jnp.where(jnp.logical_or(chosen > 0.5, n_i == i), blk_shift, NEG)

    def logits(h, c, qk, _, diag, pen):
        row = rowadd_ref[h, pl.ds(c, 1), :]
        if diag:
            return qk + bias_ref[1, h] + row
        return qk + bias_ref[0, h] + (row + pen)

    _attention(i, 0, k_ref, vT_ref, st, logits)
    _finish(o_ref, acc_ref)


def _moba(qT, k, vT, kmean, slopes):
    bsz, seq, _ = k.shape
    n_blocks = seq // MOBA_BLOCK
    return pl.pallas_call(
        functools.partial(_moba_kernel, slopes=tuple(float(s) for s in slopes), n_blocks=n_blocks),
        out_shape=jax.ShapeDtypeStruct((bsz, seq, MIX_W), BF16),
        grid=(bsz, seq // CH),
        in_specs=_attn_specs(2, seq) + [
            pl.BlockSpec((1, n_blocks, MIX_W), lambda b, i: (b, 0, 2))],
        out_specs=pl.BlockSpec((1, CH, MIX_W), lambda b, i: (b, i, 0)),
        scratch_shapes=[pltpu.VMEM((2, HEADS, CH, CH), F32),
                        pltpu.VMEM((HEADS, n_blocks, CH), F32)] + _attn_scratch(),
        compiler_params=_params(2),
        name="moba",
    )(qT, k, vT, kmean)


DIL_SPAN = 2048 // CH + 1


def _dilated_kernel(qT_ref, k_ref, vT_ref, o_ref, table_ref, *st, slopes):
    qm_ref, m_ref, acc_ref = st[:3]
    b = pl.program_id(0)
    i = pl.program_id(1)

    @pl.when(jnp.logical_and(b == 0, i == 0))
    def _bias_table():
        s_i = lax.broadcasted_iota(jnp.int32, (CH, CH), 0)
        t_i = lax.broadcasted_iota(jnp.int32, (CH, CH), 1)
        for j in range(DIL_SPAN):
            d = t_i - s_i + j * CH
            ok = d >= 0
            mult = (jnp.where(jnp.logical_and(ok, d <= 128), 1.0, 0.0)
                    + jnp.where(jnp.logical_and(ok, jnp.logical_and(d <= 512, (d & 3) == 0)), 1.0, 0.0)
                    + jnp.where(jnp.logical_and(ok, jnp.logical_and(d <= 2048, (d & 15) == 0)), 1.0, 0.0))
            logm = jnp.where(mult > 0.5, jnp.log2(jnp.maximum(mult, 1.0)), NEG)
            df = d.astype(F32)
            for h in range(HEADS):
                table_ref[h, j] = logm - df * float(slopes[h] * LOG2E)
        for h in range(HEADS):
            table_ref[h, DIL_SPAN] = jnp.full((CH, CH), NEG, F32)

    _mask_heads(qT_ref, qm_ref)
    _init_state(m_ref, acc_ref)

    def logits(h, c, qk, _, diag, pen):
        if diag:
            return qk + table_ref[h, 0]
        return qk + table_ref[h, jnp.where(pen < 0.0, DIL_SPAN, i - c)]

    _attention(i, jnp.maximum(i - (DIL_SPAN - 1), 0), k_ref, vT_ref, st, logits)
    _finish(o_ref, acc_ref)


def _dilated(qT, k, vT, slopes):
    bsz, seq, _ = k.shape
    return pl.pallas_call(
        functools.partial(_dilated_kernel, slopes=tuple(float(s) for s in slopes)),
        out_shape=jax.ShapeDtypeStruct((bsz, seq, MIX_W), BF16),
        grid=(bsz, seq // CH),
        in_specs=_attn_specs(1, seq),
        out_specs=pl.BlockSpec((1, CH, MIX_W), lambda b, i: (b, i, 0)),
        scratch_shapes=[pltpu.VMEM((HEADS, DIL_SPAN + 1, CH, CH), F32)] + _attn_scratch(),
        compiler_params=_params(2),
        name="dilated",
    )(qT, k, vT)


def _dsa_kernel(qT_ref, k_ref, vT_ref, qiT_ref, misc_ref, miscT_ref, o_ref,
                bias_ref, qi_ref, key_ref, planes_ref, alive_ref, seen_ref, *st, slopes):
    qm_ref, m_ref, acc_ref = st[:3]
    i = pl.program_id(1)
    tq = CH

    @pl.when(jnp.logical_and(pl.program_id(0) == 0, i == 0))
    def _clear_planes():
        planes_ref[...] = jnp.zeros(planes_ref.shape, jnp.int32)
    _mask_heads(qT_ref, qm_ref)
    _init_state(m_ref, acc_ref)
    _key_pos_bias(bias_ref, slopes)

    qi_all = qiT_ref[0]
    zpad = jnp.zeros((MISC_W - IDX_DIM, tq), BF16)
    for h in range(IDX_HEADS):
        qi_ref[h] = jnp.concatenate([qi_all[h * IDX_DIM:(h + 1) * IDX_DIM], zpad], axis=0)
    w_rows = miscT_ref[0, WI_ROW:WI_ROW + IDX_HEADS, :] * float(IDX_HEADS ** -0.5 * IDX_DIM ** -0.5)

    s_i = lax.broadcasted_iota(jnp.int32, (CH, tq), 0)
    t_i = lax.broadcasted_iota(jnp.int32, (CH, tq), 1)

    def score_chunk(c, diag):
        off = pl.multiple_of(c * CH, CH)
        ki = misc_ref[0, pl.ds(off, CH), :].astype(BF16)
        score = jnp.zeros((CH, tq), F32)
        for h in range(IDX_HEADS):
            rel = jnp.maximum(_dot(ki, qi_ref[h]), 0.0)
            score = score + rel * w_rows[h:h + 1, :]
        bits = lax.bitcast_convert_type(score, jnp.int32)
        key = bits ^ ((bits >> 31) & 0x7FFFFFFF)
        key = jnp.where(key == -1, 0, key)
        if diag:
            key = jnp.where(s_i <= t_i, key, INT_MIN)
        key_ref[pl.ds(off, CH), :] = key
        ukey = key ^ INT_MIN
        words = _bit_transpose32([ukey[8 * j:8 * j + 8, :] for j in range(32)])
        row0 = pl.multiple_of(c * 8, 8)
        for b in range(32):
            planes_ref[b, pl.ds(row0, 8), :] = words[31 - b]

    score_chunk(i, True)

    def score_pair(t, carry):
        score_chunk(2 * t, False)
        score_chunk(jnp.minimum(2 * t + 1, i - 1), False)
        return carry

    lax.fori_loop(0, (i + 1) // 2, score_pair, 0)
    group = 4 * 8
    n_groups = i // 4 + 1
    alive_ref[...] = jnp.where(lax.broadcasted_iota(jnp.int32, alive_ref.shape, 0) < (i + 1) * 8, -1, 0)

    def bit_step(n, state):
        thr, need = state
        b = 31 - n

        def count_group(g, cnt):
            r0 = pl.multiple_of(g * group, group)
            hit = alive_ref[pl.ds(r0, group), :] & planes_ref[b, pl.ds(r0, group), :]
            return cnt + lax.population_count(hit)

        cnt = lax.fori_loop(0, n_groups, count_group, jnp.zeros((group, tq), jnp.int32))
        cnt = jnp.sum(cnt, axis=0, keepdims=True)
        ok = cnt >= need

        def update_group(g, carry):
            r0 = pl.multiple_of(g * group, group)
            alive = alive_ref[pl.ds(r0, group), :]
            hit = alive & planes_ref[b, pl.ds(r0, group), :]
            alive_ref[pl.ds(r0, group), :] = jnp.where(ok, hit, alive ^ hit)
            return carry

        lax.fori_loop(0, n_groups, update_group, 0)
        return jnp.where(ok, thr | jnp.left_shift(jnp.int32(1), b), thr), jnp.where(ok, need, need - cnt)

    uthr, n_take = lax.fori_loop(0, 32, bit_step, (jnp.zeros((1, tq), jnp.int32),
                                                   jnp.full((1, tq), TOPK_KEYS, jnp.int32)))
    thr = uthr ^ INT_MIN

    def count_alive(g, cnt):
        return cnt + lax.population_count(alive_ref[pl.ds(pl.multiple_of(g * group, group), group), :])

    n_eq = jnp.sum(lax.fori_loop(0, n_groups, count_alive, jnp.zeros((group, tq), jnp.int32)), axis=0, keepdims=True)
    tied = jnp.logical_and(n_eq > n_take, thr != INT_MIN)
    any_tied = jnp.max(jnp.where(tied, 1.0, 0.0)) > 0.5

    def logits(h, c, qk, neg, diag, pen):
        pos = bias_ref[h] + (c.astype(F32) * float(slopes[h] * LOG2E * CH) + pen)
        return qk + _lanes(pos, tq) + neg

    @pl.when(jnp.logical_not(any_tied))
    def _no_ties():
        thr_lo = jnp.where(thr == INT_MIN, INT_MIN + 1, thr)

        def selection(c):
            off = pl.multiple_of(c * CH, CH)
            return jnp.where(key_ref[pl.ds(off, CH), :] >= thr_lo, 0.0, NEG)

        _attention(i, 0, k_ref, vT_ref, st, logits, selection, diag_first=False)

    @pl.when(any_tied)
    def _ties():
        n_first = jnp.where(thr == INT_MIN, 0, n_take).astype(F32)
        rr = lax.broadcasted_iota(jnp.int32, (CH, CH), 0)
        cc = lax.broadcasted_iota(jnp.int32, (CH, CH), 1)
        tri = jnp.where(cc <= rr, 1.0, 0.0).astype(BF16)
        seen_ref[...] = jnp.zeros(seen_ref.shape, F32)

        def selection(c):
            off = pl.multiple_of(c * CH, CH)
            key = key_ref[pl.ds(off, CH), :]
            eq = key == thr
            rank = _dot(tri, jnp.where(eq, 1.0, 0.0).astype(BF16)) + seen_ref[...]
            seen_ref[...] = rank[CH - 1:CH, :]
            take = jnp.logical_or(key > thr, jnp.logical_and(eq, rank <= n_first))
            return jnp.where(take, 0.0, NEG)

        _attention(i, 0, k_ref, vT_ref, st, logits, selection, diag_first=False)

    _finish(o_ref, acc_ref)


def _dsa(qT, k, vT, qiT, misc, miscT, slopes):
    bsz, seq, _ = k.shape
    nqi = IDX_HEADS * IDX_DIM
    plane_rows = -(-(seq // CH) // 4) * 32
    return pl.pallas_call(
        functools.partial(_dsa_kernel, slopes=tuple(float(s) for s in slopes)),
        out_shape=jax.ShapeDtypeStruct((bsz, seq, MIX_W), BF16),
        grid=(bsz, seq // CH),
        in_specs=_attn_specs(0, seq) + [
            pl.BlockSpec((1, nqi, CH), lambda b, i: (b, 0, i)),
            pl.BlockSpec((1, seq, MISC_W), lambda b, i: (b, 0, 0)),
            pl.BlockSpec((1, MISC_W, CH), lambda b, i: (b, 0, i))],
        out_specs=pl.BlockSpec((1, CH, MIX_W), lambda b, i: (b, i, 0)),
        scratch_shapes=[pltpu.VMEM((HEADS, CH, LANES), F32),
                        pltpu.VMEM((IDX_HEADS, MISC_W, CH), BF16),
                        pltpu.VMEM((seq, CH), jnp.int32),
                        pltpu.VMEM((32, plane_rows, CH), jnp.int32),
                        pltpu.VMEM((plane_rows, CH), jnp.int32),
                        pltpu.VMEM((1, CH), F32)] + _attn_scratch(),
        compiler_params=_params(2),
        name="dsa",
    )(qT, k, vT, qiT, misc, miscT)


def _merge_kernel(x_ref, ada_ref, oa_ref, ob_ref, oc_ref, od_ref, wmg_ref, wbr_ref, wout_ref,
                  g_ref, b_ref, out_ref):
    x = x_ref[0]
    sh = ada_ref[0, 0:1, :]
    sc = ada_ref[0, 1:2, :]
    g1 = ada_ref[0, 2:3, :]
    h = (x * (1.0 + sc) + sh).astype(BF16)
    mixed = None
    for m, o_ref in enumerate((oa_ref, ob_ref, oc_ref, od_ref)):
        gate = jax.nn.sigmoid(_dot(h, wmg_ref[m]))
        term = gate * _dot(o_ref[0], wbr_ref[m])
        mixed = term if mixed is None else mixed + term
    y = _dot(mixed.astype(BF16), wout_ref[...])
    out_ref[0] = _layer_norm(DN_ALPHA * x + g1 * y, g_ref[...], b_ref[...])


def _merge(x, ada, outs, w_mg, w_br, w_out, ln_g, ln_b, tm=512):
    bsz, seq, d = x.shape
    o_spec = pl.BlockSpec((1, tm, MIX_W), lambda b, i: (b, i, 0))
    return pl.pallas_call(
        _merge_kernel,
        out_shape=jax.ShapeDtypeStruct((bsz, seq, d), F32),
        grid=(bsz, seq // tm),
        in_specs=[pl.BlockSpec((1, tm, d), lambda b, i: (b, i, 0)),
                  pl.BlockSpec((1, 6, d), lambda b, i: (b, 0, 0)),
                  o_spec, o_spec, o_spec, o_spec,
                  pl.BlockSpec((N_MIX, d, d), lambda b, i: (0, 0, 0)),
                  pl.BlockSpec((N_MIX, MIX_W, d), lambda b, i: (0, 0, 0)),
                  pl.BlockSpec((d, d), lambda b, i: (0, 0)),
                  pl.BlockSpec((1, d), lambda b, i: (0, 0)),
                  pl.BlockSpec((1, d), lambda b, i: (0, 0))],
        out_specs=pl.BlockSpec((1, tm, d), lambda b, i: (b, i, 0)),
        compiler_params=_params(2),
        name="merge",
    )(x, ada, *outs, w_mg.astype(BF16), w_br.astype(BF16), w_out.astype(BF16),
      ln_g.reshape(1, d), ln_b.reshape(1, d))


def _route(logits_t, bias_col):
    per_group = N_EXPERTS // N_GROUPS
    scores = jax.nn.sigmoid(logits_t)
    biased = scores + bias_col
    s_rows = [scores[e:e + 1] for e in range(N_EXPERTS)]
    b_rows = [biased[e:e + 1] for e in range(N_EXPERTS)]
    best_g = None
    for g in range(N_GROUPS):
        r = b_rows[g * per_group:(g + 1) * per_group]
        gs = None
        for a in range(per_group):
            for b in range(a + 1, per_group):
                pair = r[a] + r[b]
                gs = pair if gs is None else jnp.maximum(gs, pair)
        if best_g is None:
            best_g, best_v = jnp.zeros_like(gs, dtype=jnp.int32), gs
        else:
            better = gs > best_v
            best_g = jnp.where(better, g, best_g)
            best_v = jnp.maximum(best_v, gs)
    masked = [jnp.where(best_g == e // per_group, b_rows[e], -jnp.inf) for e in range(N_EXPERTS)]

    def argmax_first(rows):
        idx, val = jnp.zeros_like(best_g), rows[0]
        for e in range(1, N_EXPERTS):
            better = rows[e] > val
            idx = jnp.where(better, e, idx)
            val = jnp.maximum(val, rows[e])
        return idx

    e1 = argmax_first(masked)
    e2 = argmax_first([jnp.where(e1 == e, -jnp.inf, masked[e]) for e in range(N_EXPERTS)])
    s1 = sum(jnp.where(e1 == e, s_rows[e], 0.0) for e in range(N_EXPERTS))
    s2 = sum(jnp.where(e2 == e, s_rows[e], 0.0) for e in range(N_EXPERTS))
    tot = s1 + s2
    w1, w2 = s1 / tot, s2 / tot
    rows = [jnp.where(e1 == e, w1, 0.0) + jnp.where(e2 == e, w2, 0.0) for e in range(N_EXPERTS)]
    return jnp.concatenate(rows, axis=0), best_g


def _moe_kernel(x_ref, ada_ref, wr_ref, br_ref, wg_ref, wu_ref, wd_ref, g_ref, b_ref, out_ref,
                perm_ref, xs_ref, combs_ref, ys_ref, sub_ref, *, tm, rows):
    e = pl.program_id(1)
    per_group = N_EXPERTS // N_GROUPS

    @pl.when(e == 0)
    def _route_and_sort():
        x = x_ref[...]
        h2 = x * (1.0 + ada_ref[0, 4:5, :]) + ada_ref[0, 3:4, :]
        h_hi = h2.astype(BF16)
        h_lo = (h2 - h_hi.astype(F32)).astype(BF16)
        first = _dot(h_hi, wr_ref[...])
        logits = first[:, 0:LANES] + first[:, LANES:] + _dot(h_lo, wr_ref[:, 0:LANES])
        logits_t = logits.T[0:N_EXPERTS]
        comb_t, best_g = _route(logits_t, br_ref[...])
        member = [jnp.where(best_g == g, 1.0, 0.0) for g in range(N_GROUPS)]
        grp = jnp.concatenate(member + [jnp.zeros((8 - N_GROUPS, tm), F32)], axis=0).astype(BF16)
        s_i = lax.broadcasted_iota(jnp.int32, (tm, tm), 0)
        t_i = lax.broadcasted_iota(jnp.int32, (tm, tm), 1)
        rank = _dot(grp, jnp.where(s_i <= t_i, 1.0, 0.0).astype(BF16))
        pos = jnp.zeros((1, tm), F32)
        start = jnp.zeros((1, 1), F32)
        for g in range(N_GROUPS):
            cap = jnp.ceil(rank[g:g + 1, tm - 1:tm] * (1.0 / MOE_SUB)) * MOE_SUB
            pos = pos + member[g] * (start + rank[g:g + 1] - 1.0)
            sub_ref[2 * g] = (jnp.sum(start) * (1.0 / MOE_SUB)).astype(jnp.int32)
            sub_ref[2 * g + 1] = (jnp.sum(cap) * (1.0 / MOE_SUB)).astype(jnp.int32)
            start = start + cap
        r_f = lax.broadcasted_iota(jnp.int32, (rows, tm), 0).astype(F32)
        perm = jnp.where(r_f == pos, 1.0, 0.0).astype(BF16)
        perm_ref[...] = perm
        xs_ref[...] = _dot(perm, h_hi).astype(BF16)
        comb = jnp.concatenate([comb_t, jnp.zeros((LANES - N_EXPERTS, tm), F32)], axis=0).T
        c_hi = comb.astype(BF16)
        c_lo = (comb - c_hi.astype(F32)).astype(BF16)
        combs_ref[...] = _dot(perm, c_hi) + _dot(perm, c_lo)
        ys_ref[...] = jnp.zeros(ys_ref.shape, F32)

    g = e // per_group
    lane = lax.broadcasted_iota(jnp.int32, (MOE_SUB, LANES), 1)

    def sub_tile(j, carry):
        r0 = pl.multiple_of((sub_ref[2 * g] + j) * MOE_SUB, MOE_SUB)
        xj = xs_ref[pl.ds(r0, MOE_SUB), :]
        w_e = jnp.sum(jnp.where(lane == e, combs_ref[pl.ds(r0, MOE_SUB), :], 0.0), axis=1, keepdims=True)
        hid = (jax.nn.silu(_dot(xj, wg_ref[0, 0])) * _dot(xj, wu_ref[0, 0]) * w_e).astype(BF16)
        ys_ref[pl.ds(r0, MOE_SUB), :] += _dot(hid, wd_ref[0, 0])
        return carry

    lax.fori_loop(0, sub_ref[2 * g + 1], sub_tile, 0)

    @pl.when(e == N_EXPERTS - 1)
    def _unsort_and_norm():
        y = lax.dot_general(perm_ref[...], ys_ref[...].astype(BF16), (((0,), (0,)), ((), ())),
                            preferred_element_type=F32)
        z = DN_ALPHA * x_ref[...] + ada_ref[0, 5:6, :] * y
        out_ref[...] = _layer_norm(z, g_ref[...], b_ref[...])


def _moe(x, ada_rows, w_router, b_router, w_gate, w_up, w_down, ln_g, ln_b, batch_len, tm=1024):
    n, d = x.shape
    per_b = batch_len // tm
    rows = tm + N_GROUPS * MOE_SUB
    wr32 = jnp.pad(w_router, ((0, 0), (0, LANES - N_EXPERTS)))
    wr_hi = wr32.astype(BF16)
    wr = jnp.concatenate([wr_hi, (wr32 - wr_hi.astype(F32)).astype(BF16)], axis=1)
    br = b_router.reshape(N_EXPERTS, 1)
    return pl.pallas_call(
        functools.partial(_moe_kernel, tm=tm, rows=rows),
        out_shape=jax.ShapeDtypeStruct((n, d), F32),
        grid=(n // tm, N_EXPERTS),
        in_specs=[pl.BlockSpec((tm, d), lambda i, e: (i, 0)),
                  pl.BlockSpec((1, 6, d), lambda i, e: (i // per_b, 0, 0)),
                  pl.BlockSpec((d, 2 * LANES), lambda i, e: (0, 0)),
                  pl.BlockSpec((N_EXPERTS, 1), lambda i, e: (0, 0)),
                  pl.BlockSpec((1, 1, d, D_EXPERT), lambda i, e: (0, e, 0, 0)),
                  pl.BlockSpec((1, 1, d, D_EXPERT), lambda i, e: (0, e, 0, 0)),
                  pl.BlockSpec((1, 1, D_EXPERT, d), lambda i, e: (0, e, 0, 0)),
                  pl.BlockSpec((1, d), lambda i, e: (0, 0)),
                  pl.BlockSpec((1, d), lambda i, e: (0, 0))],
        out_specs=pl.BlockSpec((tm, d), lambda i, e: (i, 0)),
        scratch_shapes=[pltpu.VMEM((rows, tm), BF16),
                        pltpu.VMEM((rows, d), BF16),
                        pltpu.VMEM((rows, LANES), F32),
                        pltpu.VMEM((rows, d), F32),
                        pltpu.SMEM((2 * N_GROUPS,), jnp.int32)],
        compiler_params=_params(2),
        name="moe",
    )(x, ada_rows, wr, br, w_gate[None].astype(BF16), w_up[None].astype(BF16), w_down[None].astype(BF16),
      ln_g.reshape(1, d), ln_b.reshape(1, d))


def kernel(x, c, w_in, b_forget, w_branch, w_merge_gate, w_out, w_ada, b_ada, ln1_g, ln1_b, ln2_g, ln2_b,
           w_router, b_router, w_exp_gate, w_exp_up, w_exp_down):
    bsz, seq, d = x.shape
    slopes = _alibi_slopes()
    ada_all = _ada(c, w_ada, b_ada).reshape(DEPTH, bsz, 6, d)
    for l in range(DEPTH):
        ada = ada_all[l]
        k, kmean, misc, qT, vT, qiT, miscT = _proj(x, ada, w_in[l])
        o_a = _dsa(qT, k, vT, qiT, misc, miscT, slopes[0])
        o_b = _dilated(qT, k, vT, slopes[1])
        o_c = _moba(qT, k, vT, kmean, slopes[2])
        o_d = _fox(qT, k, vT, misc, b_forget[l])
        x = _merge(x, ada, (o_a, o_b, o_c, o_d), w_merge_gate[l], w_branch[l], w_out[l], ln1_g[l], ln1_b[l])
        x = _moe(x.reshape(bsz * seq, d), ada, w_router, b_router, w_exp_gate[l], w_exp_up[l], w_exp_down[l],
                 ln2_g[l], ln2_b[l], seq).reshape(bsz, seq, d)
    return x
```

```python
import functools

import numpy as np
import jax
import jax.numpy as jnp
from jax import lax
from jax.experimental import pallas as pl
from jax.experimental.pallas import tpu as pltpu

D_MODEL = 1024
HEAD_DIM = 64
HEADS = 4
MIX_W = HEADS * HEAD_DIM
N_MIX = 4
IDX_HEADS = 8
IDX_DIM = 64
TOPK_KEYS = 256
MOBA_BLOCK = 256
MOBA_TOPK = 3
N_EXPERTS = 16
N_GROUPS = 4
D_EXPERT = 512
DEPTH = 2
DN_ALPHA = (2 * DEPTH) ** 0.25
LN_EPS = 1e-5
IN_COLS = 3 * N_MIX * MIX_W + IDX_HEADS * IDX_DIM + IDX_DIM + IDX_HEADS + HEADS

CH = 256
ACC_ROWS = HEAD_DIM + 16
MOE_SUB = 128
TQ_WIDE = CH
LANES = 128
MISC_W = 128
WI_ROW = IDX_DIM
FL_ROW = IDX_DIM + IDX_HEADS
LOG2E = 1.4426950408889634
NEG = -1e30
QSCALE = HEAD_DIM ** -0.5 * LOG2E
INT_MIN = -(2 ** 31)
VMEM_LIMIT = 60 * 1024 * 1024

F32 = jnp.float32
BF16 = jnp.bfloat16
HI = lax.Precision.HIGHEST
NT = (((1,), (1,)), ((), ()))


def _alibi_slopes():
    n = 3 * HEADS
    s = 2.0 ** (-8.0 * np.arange(1, n + 1) / n)
    return s.reshape(HEADS, 3).T


def _dot(a, b):
    return jnp.dot(a, b, preferred_element_type=F32)


def _dot_hi(a, b):
    return jnp.dot(a, b, preferred_element_type=F32, precision=HI)


def _bit_transpose32(words):
    a = list(words)
    j, m = 16, 0x0000FFFF
    while j:
        k = 0
        while k < 32:
            t = (a[k] ^ (a[k + j] >> j)) & m
            a[k] = a[k] ^ t
            a[k + j] = a[k + j] ^ (t << j)
            k = (k + j + 1) & ~j
        j >>= 1
        m = (m ^ (m << j)) & 0xFFFFFFFF
    return a


def _split3(x):
    x1 = x.astype(BF16)
    r1 = x - x1.astype(F32)
    x2 = r1.astype(BF16)
    x3 = (r1 - x2.astype(F32)).astype(BF16)
    return x1, x2, x3


def _params(n_axes):
    return pltpu.CompilerParams(dimension_semantics=("arbitrary",) * n_axes,
                                vmem_limit_bytes=VMEM_LIMIT)


def _layer_norm(z, g, b):
    mu = jnp.mean(z, axis=-1, keepdims=True)
    var = jnp.mean(jnp.square(z - mu), axis=-1, keepdims=True)
    return (z - mu) * lax.rsqrt(var + LN_EPS) * g + b


def _ada_kernel(c_ref, w_ref, b_ref, o_ref):
    o_ref[0] = _dot_hi(c_ref[...], w_ref[0]) + b_ref[0]


def _ada(c, w_ada, b_ada):
    depth, d, n = w_ada.shape
    bsz = c.shape[0]
    tn = D_MODEL
    return pl.pallas_call(
        _ada_kernel,
        out_shape=jax.ShapeDtypeStruct((depth, bsz, n), F32),
        grid=(depth, n // tn),
        in_specs=[pl.BlockSpec((bsz, d), lambda l, j: (0, 0)),
                  pl.BlockSpec((1, d, tn), lambda l, j: (l, 0, j)),
                  pl.BlockSpec((1, 1, tn), lambda l, j: (l, 0, j))],
        out_specs=pl.BlockSpec((1, bsz, tn), lambda l, j: (l, 0, j)),
        compiler_params=_params(2),
        name="ada",
    )(c, w_ada, b_ada.reshape(depth, 1, n))


def _proj_kernel(x_ref, ada_ref, wk_ref, wm_ref, wt_ref, wmt_ref,
                 k_ref, kmean_ref, misc_ref, qT_ref, vT_ref, qiT_ref, miscT_ref, *, tm):
    x = x_ref[0]
    sh = ada_ref[0, 0:1, :]
    sc = ada_ref[0, 1:2, :]
    h = (x * (1.0 + sc) + sh).astype(BF16)
    kf = _dot(h, wk_ref[...])
    k_ref[0] = kf.astype(BF16)
    for g in range(tm // MOBA_BLOCK):
        kmean_ref[0, 0, g:g + 1, :] = jnp.mean(kf[g * MOBA_BLOCK:(g + 1) * MOBA_BLOCK], axis=0, keepdims=True)
    misc_ref[0] = _dot(h, wm_ref[...])
    t = lax.dot_general(wt_ref[...], h, NT, preferred_element_type=F32)
    nq = N_MIX * MIX_W
    qT_ref[0] = (t[0:nq] * QSCALE).astype(BF16)
    for g in range(tm // CH):
        vT_ref[0, g] = t[nq:2 * nq, g * CH:(g + 1) * CH].astype(BF16)
    qiT_ref[0] = t[2 * nq:].astype(BF16)
    miscT_ref[0] = lax.dot_general(wmt_ref[...], h, NT, preferred_element_type=F32)


def _proj(x, ada, w_in, tm=512):
    bsz, seq, d = x.shape
    nq = N_MIX * MIX_W
    nqi = IDX_HEADS * IDX_DIM
    w = w_in.astype(BF16)
    wk = w[:, nq:2 * nq]
    wm = jnp.pad(w[:, 3 * nq + nqi:], ((0, 0), (0, MISC_W - (IN_COLS - 3 * nq - nqi))))
    wt = jnp.concatenate([w[:, 0:nq], w[:, 2 * nq:3 * nq], w[:, 3 * nq:3 * nq + nqi]], axis=1).T
    wmt = wm.T
    nt = seq // tm
    full = lambda b, i: (0, 0)
    outs = pl.pallas_call(
        functools.partial(_proj_kernel, tm=tm),
        out_shape=(jax.ShapeDtypeStruct((bsz, seq, nq), BF16),
                   jax.ShapeDtypeStruct((bsz, nt, tm // MOBA_BLOCK, nq), F32),
                   jax.ShapeDtypeStruct((bsz, seq, MISC_W), F32),
                   jax.ShapeDtypeStruct((bsz, nq, seq), BF16),
                   jax.ShapeDtypeStruct((bsz, seq // CH, nq, CH), BF16),
                   jax.ShapeDtypeStruct((bsz, nqi, seq), BF16),
                   jax.ShapeDtypeStruct((bsz, MISC_W, seq), F32)),
        grid=(bsz, nt),
        in_specs=[pl.BlockSpec((1, tm, d), lambda b, i: (b, i, 0)),
                  pl.BlockSpec((1, 6, d), lambda b, i: (b, 0, 0)),
                  pl.BlockSpec(wk.shape, full),
                  pl.BlockSpec(wm.shape, full),
                  pl.BlockSpec(wt.shape, full),
                  pl.BlockSpec(wmt.shape, full)],
        out_specs=(pl.BlockSpec((1, tm, nq), lambda b, i: (b, i, 0)),
                   pl.BlockSpec((1, 1, tm // MOBA_BLOCK, nq), lambda b, i: (b, i, 0, 0)),
                   pl.BlockSpec((1, tm, MISC_W), lambda b, i: (b, i, 0)),
                   pl.BlockSpec((1, nq, tm), lambda b, i: (b, 0, i)),
                   pl.BlockSpec((1, tm // CH, nq, CH), lambda b, i: (b, i, 0, 0)),
                   pl.BlockSpec((1, nqi, tm), lambda b, i: (b, 0, i)),
                   pl.BlockSpec((1, MISC_W, tm), lambda b, i: (b, 0, i))),
        compiler_params=_params(2),
        name="proj",
    )(x, ada, wk, wm, wt, wmt)
    k, kmean, misc, qT, vT, qiT, miscT = outs
    return k, kmean.reshape(bsz, seq // MOBA_BLOCK, nq), misc, qT, vT, qiT, miscT


def _mask_heads(qT_ref, qm_ref):
    q = qT_ref[0]
    rowh = lax.broadcasted_iota(jnp.int32, q.shape, 0) // HEAD_DIM
    for h in range(HEADS):
        qm_ref[h] = jnp.where(rowh == h, q, jnp.zeros_like(q))


def _init_state(m_ref, acc_ref):
    m_ref[...] = jnp.full(m_ref.shape, NEG, F32)
    acc_ref[...] = jnp.zeros(acc_ref.shape, F32)


def _attention(i, first, k_ref, vT_ref, st, logits_fn, chunk_ctx=None, n_diag=None):
    qm_ref, m_ref, acc_ref, s_ref, mx_ref, p_ref, alpha_ref = st
    if n_diag is None:
        n_chunks = i - first + 1
    else:
        n_chunks = n_diag + (n_diag * i - first)
    ones = jnp.ones((ACC_ROWS - HEAD_DIM, CH), BF16)

    def chunk_of(n):
        n = jnp.clip(n, 0, n_chunks - 1)
        if n_diag is None:
            return first + n
        return jnp.where(n < n_diag, n_diag * i + n, first + n - n_diag)

    def position(n):
        if n_diag is None:
            return None
        return n if n < n_diag else -1

    def logits_head(h, slot, c, kc, ctx, diag, pen):
        x = logits_fn(h, c, _dot(kc, qm_ref[h]), ctx, diag, pen)
        s_ref[slot, h] = x
        mx_ref[slot, h] = jnp.max(x, axis=0, keepdims=True)

    def softmax_head(h, slot):
        m_old = m_ref[h]
        m_new = jnp.maximum(m_old, mx_ref[slot, h])
        alpha_ref[slot, h] = jnp.exp2(m_old - m_new)
        p_ref[slot, h] = jnp.exp2(s_ref[slot, h] - m_new).astype(BF16)
        m_ref[h] = m_new

    def pv_head(h, slot, vc):
        v1 = jnp.concatenate([vc[h * HEAD_DIM:(h + 1) * HEAD_DIM, :], ones], axis=0)
        acc_ref[h] = alpha_ref[slot, h] * acc_ref[h] + _dot(v1, p_ref[slot, h])

    def step(n, a, diag):
        c = chunk_of(n + 1)
        pen = jnp.where(n + 1 < n_chunks, 0.0, NEG)
        kc = k_ref[0, pl.ds(pl.multiple_of(c * CH, CH), CH), :]
        vc = vT_ref[0, chunk_of(n - 1)]
        ctx = chunk_ctx(c) if chunk_ctx is not None else None
        for h in range(HEADS):
            softmax_head(h, a)
        for h in range(HEADS):
            pv_head(h, 1 - a, vc)
        for h in range(HEADS):
            logits_head(h, 1 - a, c, kc, ctx, diag, pen)

    p_ref[1] = jnp.zeros(p_ref.shape[1:], BF16)
    alpha_ref[1] = jnp.ones(alpha_ref.shape[1:], F32)
    c0 = chunk_of(jnp.int32(0))
    k0 = k_ref[0, pl.ds(pl.multiple_of(c0 * CH, CH), CH), :]
    ctx0 = chunk_ctx(c0) if chunk_ctx is not None else None
    for h in range(HEADS):
        logits_head(h, 0, c0, k0, ctx0, position(0), 0.0)

    first_trip = 0
    if n_diag is not None and n_diag > 1:
        step(0, 0, position(1))
        step(1, 1, position(2))
        first_trip = 1
    later = position(2 * first_trip + 1)

    def pair(t, carry):
        step(2 * t, 0, later)
        step(2 * t + 1, 1, later)
        return carry

    n_trips = (n_chunks + 1) // 2
    lax.fori_loop(first_trip, n_trips, pair, 0)
    v_last = vT_ref[0, chunk_of(2 * n_trips - 1)]
    for h in range(HEADS):
        pv_head(h, 1, v_last)


def _finish(o_ref, acc_ref):
    parts = [acc_ref[h, 0:HEAD_DIM, :] / acc_ref[h, HEAD_DIM:HEAD_DIM + 1, :] for h in range(HEADS)]
    o_ref[0] = jnp.concatenate(parts, axis=0).T.astype(BF16)


def _lanes(tile, tq):
    return tile if tq == LANES else jnp.concatenate([tile] * (tq // LANES), axis=1)


def _causal_neg(tq, d=0):
    s_i = lax.broadcasted_iota(jnp.int32, (CH, tq), 0) + d * CH
    t_i = lax.broadcasted_iota(jnp.int32, (CH, tq), 1)
    return jnp.where(s_i <= t_i, 0.0, NEG).astype(F32)


def _key_pos_bias(bias_ref, slopes):
    s_i = lax.broadcasted_iota(jnp.int32, (CH, LANES), 0).astype(F32)
    for h in range(HEADS):
        bias_ref[h] = s_i * float(slopes[h] * LOG2E)


def _attn_specs(mixer, seq, tq=CH):
    return [pl.BlockSpec((1, MIX_W, tq), lambda b, i: (b, mixer, i)),
            pl.BlockSpec((1, seq, MIX_W), lambda b, i: (b, 0, mixer)),
            pl.BlockSpec((1, seq // CH, MIX_W, CH), lambda b, i: (b, 0, mixer, 0))]


def _attn_scratch(tq=CH):
    return [pltpu.VMEM((HEADS, MIX_W, tq), BF16),
            pltpu.VMEM((HEADS, 1, tq), F32),
            pltpu.VMEM((HEADS, ACC_ROWS, tq), F32),
            pltpu.VMEM((2, HEADS, CH, tq), F32),
            pltpu.VMEM((2, HEADS, 1, tq), F32),
            pltpu.VMEM((2, HEADS, CH, tq), BF16),
            pltpu.VMEM((2, HEADS, 1, tq), F32)]


def _fox_kernel(qT_ref, k_ref, vT_ref, misc_ref, bf_ref, o_ref,
                cum_ref, *st, n_chunks):
    qm_ref, m_ref, acc_ref = st[:3]
    i = pl.program_id(1)

    @pl.when(i == 0)
    def _cumulative_gates():
        rr = lax.broadcasted_iota(jnp.int32, (CH, CH), 0)
        cc = lax.broadcasted_iota(jnp.int32, (CH, CH), 1)
        tri = jnp.where(cc <= rr, 1.0, 0.0).astype(BF16)
        erow = lax.broadcasted_iota(jnp.int32, (LANES, LANES), 0)

        def body(blk, carry):
            off = pl.multiple_of(blk * CH, CH)
            z = misc_ref[0, pl.ds(off, CH), :] + bf_ref[...]
            ls = jnp.minimum(z, 0.0) - jnp.log1p(jnp.exp(-jnp.abs(z)))
            csum = sum(_dot(tri, part) for part in _split3(ls))
            parts = _split3(csum)
            new = []
            for h in range(HEADS):
                sel = jnp.where(erow == FL_ROW + h, 1.0, 0.0).astype(BF16)
                cum = sum(_dot(part, sel) for part in parts) + carry[h]
                cum_ref[h, pl.ds(off, CH), :] = cum * LOG2E
                new.append(cum[CH - 1:CH, :])
            return tuple(new)

        lax.fori_loop(0, n_chunks, body, tuple(jnp.zeros((1, LANES), F32) for _ in range(HEADS)))

    _mask_heads(qT_ref, qm_ref)
    _init_state(m_ref, acc_ref)

    def logits(h, c, qk, _, diag, pen):
        off = pl.multiple_of(c * CH, CH)
        cum = cum_ref[h, pl.ds(off, CH), :]
        if diag >= 0:
            return qk - _lanes(cum, TQ_WIDE) + _causal_neg(TQ_WIDE, diag)
        return qk - _lanes(cum - pen, TQ_WIDE)

    _attention(i, 0, k_ref, vT_ref, st, logits, n_diag=TQ_WIDE // CH)
    _finish(o_ref, acc_ref)


def _fox(qT, k, vT, misc, b_forget_l):
    bsz, seq, _ = k.shape
    n_chunks = seq // CH
    bf = jnp.zeros((1, MISC_W), F32).at[0, FL_ROW:FL_ROW + HEADS].set(b_forget_l)
    return pl.pallas_call(
        functools.partial(_fox_kernel, n_chunks=n_chunks),
        out_shape=jax.ShapeDtypeStruct((bsz, seq, MIX_W), BF16),
        grid=(bsz, seq // TQ_WIDE),
        in_specs=_attn_specs(3, seq, TQ_WIDE) + [
            pl.BlockSpec((1, seq, MISC_W), lambda b, i: (b, 0, 0)),
            pl.BlockSpec((1, MISC_W), lambda b, i: (0, 0))],
        out_specs=pl.BlockSpec((1, TQ_WIDE, MIX_W), lambda b, i: (b, i, 0)),
        scratch_shapes=[pltpu.VMEM((HEADS, seq, LANES), F32)] + _attn_scratch(TQ_WIDE),
        compiler_params=_params(2),
        name="fox",
    )(qT, k, vT, misc, bf)


def _moba_kernel(qT_ref, k_ref, vT_ref, kmean_ref, o_ref,
                 bias_ref, rowadd_ref, *st, slopes, n_blocks):
    qm_ref, m_ref, acc_ref = st[:3]
    i = pl.program_id(1)
    tq = TQ_WIDE
    n_diag = tq // CH
    _mask_heads(qT_ref, qm_ref)
    _init_state(m_ref, acc_ref)

    @pl.when(jnp.logical_and(pl.program_id(0) == 0, i == 0))
    def _bias_table():
        s_f = lax.broadcasted_iota(jnp.int32, (CH, tq), 0).astype(F32)
        for h in range(HEADS):
            pos = s_f * float(slopes[h] * LOG2E)
            bias_ref[0, h] = pos
            for d in range(n_diag):
                bias_ref[1 + d, h] = pos + _causal_neg(tq, d)

    n_i = lax.broadcasted_iota(jnp.int32, (n_blocks, tq), 0)
    n_f = n_i.astype(F32)
    own = n_diag * i + lax.broadcasted_iota(jnp.int32, (n_blocks, tq), 1) // MOBA_BLOCK
    past = n_i < own
    kmean_parts = _split3(kmean_ref[0])
    for h in range(HEADS):
        gate = sum(_dot(part, qm_ref[h]) for part in kmean_parts)
        gate = jnp.where(past, gate, -jnp.inf)
        chosen = jnp.zeros((n_blocks, tq), F32)
        for _ in range(MOBA_TOPK):
            mx = jnp.max(gate, axis=0, keepdims=True)
            first = jnp.min(jnp.where(gate == mx, n_f, float(n_blocks)), axis=0, keepdims=True)
            pick = n_f == first
            chosen = jnp.where(pick, 1.0, chosen)
            gate = jnp.where(pick, -jnp.inf, gate)
        chosen = jnp.where(past, chosen, 0.0)
        blk_shift = n_f * float(slopes[h] * LOG2E * MOBA_BLOCK)
        rowadd_ref[h] = jnp.where(jnp.logical_or(chosen > 0.5, n_i == own), blk_shift, NEG)

    def logits(h, c, qk, _, diag, pen):
        row = rowadd_ref[h, pl.ds(c, 1), :]
        if diag >= 0:
            return qk + bias_ref[1 + diag, h] + row
        return qk + bias_ref[0, h] + (row + pen)

    _attention(i, 0, k_ref, vT_ref, st, logits, n_diag=n_diag)
    _finish(o_ref, acc_ref)


def _moba(qT, k, vT, kmean, slopes):
    bsz, seq, _ = k.shape
    n_blocks = seq // MOBA_BLOCK
    return pl.pallas_call(
        functools.partial(_moba_kernel, slopes=tuple(float(s) for s in slopes), n_blocks=n_blocks),
        out_shape=jax.ShapeDtypeStruct((bsz, seq, MIX_W), BF16),
        grid=(bsz, seq // TQ_WIDE),
        in_specs=_attn_specs(2, seq, TQ_WIDE) + [
            pl.BlockSpec((1, n_blocks, MIX_W), lambda b, i: (b, 0, 2))],
        out_specs=pl.BlockSpec((1, TQ_WIDE, MIX_W), lambda b, i: (b, i, 0)),
        scratch_shapes=[pltpu.VMEM((1 + TQ_WIDE // CH, HEADS, CH, TQ_WIDE), F32),
                        pltpu.VMEM((HEADS, n_blocks, TQ_WIDE), F32)] + _attn_scratch(TQ_WIDE),
        compiler_params=_params(2),
        name="moba",
    )(qT, k, vT, kmean)


DIL_SPAN = 2048 // CH + 1


def _dilated_kernel(qT_ref, k_ref, vT_ref, o_ref, table_ref, *st, slopes):
    qm_ref, m_ref, acc_ref = st[:3]
    b = pl.program_id(0)
    i = pl.program_id(1)

    tq = TQ_WIDE
    n_diag = tq // CH
    n_entries = n_diag - 1 + DIL_SPAN

    @pl.when(jnp.logical_and(b == 0, i == 0))
    def _bias_table():
        s_i = lax.broadcasted_iota(jnp.int32, (CH, tq), 0)
        t_i = lax.broadcasted_iota(jnp.int32, (CH, tq), 1)
        for e in range(n_entries):
            j = e - (n_diag - 1)
            d = t_i - s_i + j * CH
            ok = d >= 0
            mult = (jnp.where(jnp.logical_and(ok, d <= 128), 1.0, 0.0)
                    + jnp.where(jnp.logical_and(ok, jnp.logical_and(d <= 512, (d & 3) == 0)), 1.0, 0.0)
                    + jnp.where(jnp.logical_and(ok, jnp.logical_and(d <= 2048, (d & 15) == 0)), 1.0, 0.0))
            logm = jnp.where(mult > 0.5, jnp.log2(jnp.maximum(mult, 1.0)), NEG)
            df = d.astype(F32)
            for h in range(HEADS):
                table_ref[h, e] = logm - df * float(slopes[h] * LOG2E)
        for h in range(HEADS):
            table_ref[h, n_entries] = jnp.full((CH, tq), NEG, F32)

    _mask_heads(qT_ref, qm_ref)
    _init_state(m_ref, acc_ref)

    def logits(h, c, qk, _, diag, pen):
        if diag >= 0:
            return qk + table_ref[h, n_diag - 1 - diag]
        return qk + table_ref[h, jnp.where(pen < 0.0, n_entries, n_diag * i - c + (n_diag - 1))]

    _attention(i, jnp.maximum(n_diag * i - (DIL_SPAN - 1), 0), k_ref, vT_ref, st, logits, n_diag=n_diag)
    _finish(o_ref, acc_ref)


def _dilated(qT, k, vT, slopes):
    bsz, seq, _ = k.shape
    return pl.pallas_call(
        functools.partial(_dilated_kernel, slopes=tuple(float(s) for s in slopes)),
        out_shape=jax.ShapeDtypeStruct((bsz, seq, MIX_W), BF16),
        grid=(bsz, seq // TQ_WIDE),
        in_specs=_attn_specs(1, seq, TQ_WIDE),
        out_specs=pl.BlockSpec((1, TQ_WIDE, MIX_W), lambda b, i: (b, i, 0)),
        scratch_shapes=[pltpu.VMEM((HEADS, TQ_WIDE // CH + DIL_SPAN, CH, TQ_WIDE), F32)]
        + _attn_scratch(TQ_WIDE),
        compiler_params=_params(2),
        name="dilated",
    )(qT, k, vT)


def _dsa_kernel(qT_ref, k_ref, vT_ref, qiT_ref, misc_ref, miscT_ref, o_ref,
                bias_ref, qi_ref, key_ref, planes_ref, alive_ref, seen_ref, *st, slopes):
    qm_ref, m_ref, acc_ref = st[:3]
    i = pl.program_id(1)
    tq = CH

    @pl.when(jnp.logical_and(pl.program_id(0) == 0, i == 0))
    def _clear_planes():
        planes_ref[...] = jnp.zeros(planes_ref.shape, jnp.int32)
    _mask_heads(qT_ref, qm_ref)
    _init_state(m_ref, acc_ref)
    _key_pos_bias(bias_ref, slopes)

    qi_all = qiT_ref[0]
    zpad = jnp.zeros((MISC_W - IDX_DIM, tq), BF16)
    for h in range(IDX_HEADS):
        qi_ref[h] = jnp.concatenate([qi_all[h * IDX_DIM:(h + 1) * IDX_DIM], zpad], axis=0)
    w_rows = miscT_ref[0, WI_ROW:WI_ROW + IDX_HEADS, :] * float(IDX_HEADS ** -0.5 * IDX_DIM ** -0.5)

    s_i = lax.broadcasted_iota(jnp.int32, (CH, tq), 0)
    t_i = lax.broadcasted_iota(jnp.int32, (CH, tq), 1)

    def score_chunk(c, diag):
        off = pl.multiple_of(c * CH, CH)
        ki = misc_ref[0, pl.ds(off, CH), :].astype(BF16)
        score = jnp.zeros((CH, tq), F32)
        for h in range(IDX_HEADS):
            rel = jnp.maximum(_dot(ki, qi_ref[h]), 0.0)
            score = score + rel * w_rows[h:h + 1, :]
        bits = lax.bitcast_convert_type(score, jnp.int32)
        key = bits ^ ((bits >> 31) & 0x7FFFFFFF)
        key = jnp.where(key == -1, 0, key)
        if diag:
            key = jnp.where(s_i <= t_i, key, INT_MIN)
        key_ref[pl.ds(off, CH), :] = key
        ukey = key ^ INT_MIN
        words = _bit_transpose32([ukey[8 * j:8 * j + 8, :] for j in range(32)])
        row0 = pl.multiple_of(c * 8, 8)
        for b in range(32):
            planes_ref[b, pl.ds(row0, 8), :] = words[31 - b]

    score_chunk(i, True)

    def score_pair(t, carry):
        score_chunk(2 * t, False)
        score_chunk(jnp.minimum(2 * t + 1, i - 1), False)
        return carry

    lax.fori_loop(0, (i + 1) // 2, score_pair, 0)
    group = 4 * 8
    n_groups = i // 4 + 1
    alive_ref[...] = jnp.where(lax.broadcasted_iota(jnp.int32, alive_ref.shape, 0) < (i + 1) * 8, -1, 0)

    def bit_step(n, state):
        thr, need = state
        b = 31 - n

        def count_group(g, cnt):
            r0 = pl.multiple_of(g * group, group)
            hit = alive_ref[pl.ds(r0, group), :] & planes_ref[b, pl.ds(r0, group), :]
            return cnt + lax.population_count(hit)

        cnt = lax.fori_loop(0, n_groups, count_group, jnp.zeros((group, tq), jnp.int32))
        cnt = jnp.sum(cnt, axis=0, keepdims=True)
        ok = cnt >= need

        def update_group(g, carry):
            r0 = pl.multiple_of(g * group, group)
            alive = alive_ref[pl.ds(r0, group), :]
            hit = alive & planes_ref[b, pl.ds(r0, group), :]
            alive_ref[pl.ds(r0, group), :] = jnp.where(ok, hit, alive ^ hit)
            return carry

        lax.fori_loop(0, n_groups, update_group, 0)
        return jnp.where(ok, thr | jnp.left_shift(jnp.int32(1), b), thr), jnp.where(ok, need, need - cnt)

    uthr, n_take = lax.fori_loop(0, 32, bit_step, (jnp.zeros((1, tq), jnp.int32),
                                                   jnp.full((1, tq), TOPK_KEYS, jnp.int32)))
    thr = uthr ^ INT_MIN

    def count_alive(g, cnt):
        return cnt + lax.population_count(alive_ref[pl.ds(pl.multiple_of(g * group, group), group), :])

    n_eq = jnp.sum(lax.fori_loop(0, n_groups, count_alive, jnp.zeros((group, tq), jnp.int32)), axis=0, keepdims=True)
    tied = jnp.logical_and(n_eq > n_take, thr != INT_MIN)
    any_tied = jnp.max(jnp.where(tied, 1.0, 0.0)) > 0.5

    def logits(h, c, qk, neg, diag, pen):
        pos = bias_ref[h] + (c.astype(F32) * float(slopes[h] * LOG2E * CH) + pen)
        return qk + _lanes(pos, tq) + neg

    @pl.when(jnp.logical_not(any_tied))
    def _no_ties():
        thr_lo = jnp.where(thr == INT_MIN, INT_MIN + 1, thr)

        def selection(c):
            off = pl.multiple_of(c * CH, CH)
            return jnp.where(key_ref[pl.ds(off, CH), :] >= thr_lo, 0.0, NEG)

        _attention(i, 0, k_ref, vT_ref, st, logits, selection)

    @pl.when(any_tied)
    def _ties():
        n_first = jnp.where(thr == INT_MIN, 0, n_take).astype(F32)
        rr = lax.broadcasted_iota(jnp.int32, (CH, CH), 0)
        cc = lax.broadcasted_iota(jnp.int32, (CH, CH), 1)
        tri = jnp.where(cc <= rr, 1.0, 0.0).astype(BF16)
        seen_ref[...] = jnp.zeros(seen_ref.shape, F32)

        def selection(c):
            off = pl.multiple_of(c * CH, CH)
            key = key_ref[pl.ds(off, CH), :]
            eq = key == thr
            rank = _dot(tri, jnp.where(eq, 1.0, 0.0).astype(BF16)) + seen_ref[...]
            seen_ref[...] = rank[CH - 1:CH, :]
            take = jnp.logical_or(key > thr, jnp.logical_and(eq, rank <= n_first))
            return jnp.where(take, 0.0, NEG)

        _attention(i, 0, k_ref, vT_ref, st, logits, selection)

    _finish(o_ref, acc_ref)


def _dsa(qT, k, vT, qiT, misc, miscT, slopes):
    bsz, seq, _ = k.shape
    nqi = IDX_HEADS * IDX_DIM
    plane_rows = -(-(seq // CH) // 4) * 32
    return pl.pallas_call(
        functools.partial(_dsa_kernel, slopes=tuple(float(s) for s in slopes)),
        out_shape=jax.ShapeDtypeStruct((bsz, seq, MIX_W), BF16),
        grid=(bsz, seq // CH),
        in_specs=_attn_specs(0, seq) + [
            pl.BlockSpec((1, nqi, CH), lambda b, i: (b, 0, i)),
            pl.BlockSpec((1, seq, MISC_W), lambda b, i: (b, 0, 0)),
            pl.BlockSpec((1, MISC_W, CH), lambda b, i: (b, 0, i))],
        out_specs=pl.BlockSpec((1, CH, MIX_W), lambda b, i: (b, i, 0)),
        scratch_shapes=[pltpu.VMEM((HEADS, CH, LANES), F32),
                        pltpu.VMEM((IDX_HEADS, MISC_W, CH), BF16),
                        pltpu.VMEM((seq, CH), jnp.int32),
                        pltpu.VMEM((32, plane_rows, CH), jnp.int32),
                        pltpu.VMEM((plane_rows, CH), jnp.int32),
                        pltpu.VMEM((1, CH), F32)] + _attn_scratch(),
        compiler_params=_params(2),
        name="dsa",
    )(qT, k, vT, qiT, misc, miscT)


def _merge_kernel(x_ref, ada_ref, oa_ref, ob_ref, oc_ref, od_ref, wmg_ref, wbr_ref, wout_ref,
                  g_ref, b_ref, out_ref):
    x = x_ref[0]
    sh = ada_ref[0, 0:1, :]
    sc = ada_ref[0, 1:2, :]
    g1 = ada_ref[0, 2:3, :]
    h = (x * (1.0 + sc) + sh).astype(BF16)
    mixed = None
    for m, o_ref in enumerate((oa_ref, ob_ref, oc_ref, od_ref)):
        gate = jax.nn.sigmoid(_dot(h, wmg_ref[m]))
        term = gate * _dot(o_ref[0], wbr_ref[m])
        mixed = term if mixed is None else mixed + term
    y = _dot(mixed.astype(BF16), wout_ref[...])
    out_ref[0] = _layer_norm(DN_ALPHA * x + g1 * y, g_ref[...], b_ref[...])


def _merge(x, ada, outs, w_mg, w_br, w_out, ln_g, ln_b, tm=512):
    bsz, seq, d = x.shape
    o_spec = pl.BlockSpec((1, tm, MIX_W), lambda b, i: (b, i, 0))
    return pl.pallas_call(
        _merge_kernel,
        out_shape=jax.ShapeDtypeStruct((bsz, seq, d), F32),
        grid=(bsz, seq // tm),
        in_specs=[pl.BlockSpec((1, tm, d), lambda b, i: (b, i, 0)),
                  pl.BlockSpec((1, 6, d), lambda b, i: (b, 0, 0)),
                  o_spec, o_spec, o_spec, o_spec,
                  pl.BlockSpec((N_MIX, d, d), lambda b, i: (0, 0, 0)),
                  pl.BlockSpec((N_MIX, MIX_W, d), lambda b, i: (0, 0, 0)),
                  pl.BlockSpec((d, d), lambda b, i: (0, 0)),
                  pl.BlockSpec((1, d), lambda b, i: (0, 0)),
                  pl.BlockSpec((1, d), lambda b, i: (0, 0))],
        out_specs=pl.BlockSpec((1, tm, d), lambda b, i: (b, i, 0)),
        compiler_params=_params(2),
        name="merge",
    )(x, ada, *outs, w_mg.astype(BF16), w_br.astype(BF16), w_out.astype(BF16),
      ln_g.reshape(1, d), ln_b.reshape(1, d))


def _route(logits_t, bias_col):
    per_group = N_EXPERTS // N_GROUPS
    scores = jax.nn.sigmoid(logits_t)
    biased = scores + bias_col
    s_rows = [scores[e:e + 1] for e in range(N_EXPERTS)]
    b_rows = [biased[e:e + 1] for e in range(N_EXPERTS)]
    best_g = None
    for g in range(N_GROUPS):
        r = b_rows[g * per_group:(g + 1) * per_group]
        gs = None
        for a in range(per_group):
            for b in range(a + 1, per_group):
                pair = r[a] + r[b]
                gs = pair if gs is None else jnp.maximum(gs, pair)
        if best_g is None:
            best_g, best_v = jnp.zeros_like(gs, dtype=jnp.int32), gs
        else:
            better = gs > best_v
            best_g = jnp.where(better, g, best_g)
            best_v = jnp.maximum(best_v, gs)
    masked = [jnp.where(best_g == e // per_group, b_rows[e], -jnp.inf) for e in range(N_EXPERTS)]

    def argmax_first(rows):
        idx, val = jnp.zeros_like(best_g), rows[0]
        for e in range(1, N_EXPERTS):
            better = rows[e] > val
            idx = jnp.where(better, e, idx)
            val = jnp.maximum(val, rows[e])
        return idx

    e1 = argmax_first(masked)
    e2 = argmax_first([jnp.where(e1 == e, -jnp.inf, masked[e]) for e in range(N_EXPERTS)])
    s1 = sum(jnp.where(e1 == e, s_rows[e], 0.0) for e in range(N_EXPERTS))
    s2 = sum(jnp.where(e2 == e, s_rows[e], 0.0) for e in range(N_EXPERTS))
    tot = s1 + s2
    w1, w2 = s1 / tot, s2 / tot
    rows = [jnp.where(e1 == e, w1, 0.0) + jnp.where(e2 == e, w2, 0.0) for e in range(N_EXPERTS)]
    return jnp.concatenate(rows, axis=0), best_g


def _moe_kernel(x_ref, ada_ref, wr_ref, br_ref, wg_ref, wu_ref, wd_ref, g_ref, b_ref, out_ref,
                perm_ref, xs_ref, combs_ref, ys_ref, sub_ref, *, tm, rows):
    g = pl.program_id(1)
    per_group = N_EXPERTS // N_GROUPS

    @pl.when(g == 0)
    def _route_and_sort():
        x = x_ref[...]
        h2 = x * (1.0 + ada_ref[0, 4:5, :]) + ada_ref[0, 3:4, :]
        h_hi = h2.astype(BF16)
        h_lo = (h2 - h_hi.astype(F32)).astype(BF16)
        first = _dot(h_hi, wr_ref[...])
        logits = first[:, 0:LANES] + first[:, LANES:] + _dot(h_lo, wr_ref[:, 0:LANES])
        logits_t = logits.T[0:N_EXPERTS]
        comb_t, best_g = _route(logits_t, br_ref[...])
        member = [jnp.where(best_g == g, 1.0, 0.0) for g in range(N_GROUPS)]
        grp = jnp.concatenate(member + [jnp.zeros((8 - N_GROUPS, tm), F32)], axis=0).astype(BF16)
        s_i = lax.broadcasted_iota(jnp.int32, (tm, tm), 0)
        t_i = lax.broadcasted_iota(jnp.int32, (tm, tm), 1)
        rank = _dot(grp, jnp.where(s_i <= t_i, 1.0, 0.0).astype(BF16))
        pos = jnp.zeros((1, tm), F32)
        start = jnp.zeros((1, 1), F32)
        for g in range(N_GROUPS):
            cap = jnp.ceil(rank[g:g + 1, tm - 1:tm] * (1.0 / MOE_SUB)) * MOE_SUB
            pos = pos + member[g] * (start + rank[g:g + 1] - 1.0)
            sub_ref[2 * g] = (jnp.sum(start) * (1.0 / MOE_SUB)).astype(jnp.int32)
            sub_ref[2 * g + 1] = (jnp.sum(cap) * (1.0 / MOE_SUB)).astype(jnp.int32)
            start = start + cap
        r_f = lax.broadcasted_iota(jnp.int32, (rows, tm), 0).astype(F32)
        perm = jnp.where(r_f == pos, 1.0, 0.0).astype(BF16)
        perm_ref[...] = perm
        xs_ref[...] = _dot(perm, h_hi).astype(BF16)
        comb = jnp.concatenate([comb_t, jnp.zeros((LANES - N_EXPERTS, tm), F32)], axis=0).T
        c_hi = comb.astype(BF16)
        c_lo = (comb - c_hi.astype(F32)).astype(BF16)
        combs_ref[...] = _dot(perm, c_hi) + _dot(perm, c_lo)
        ys_ref[...] = jnp.zeros(ys_ref.shape, BF16)

    lane = lax.broadcasted_iota(jnp.int32, (MOE_SUB, LANES), 1)

    def sub_tile(j, carry):
        r0 = pl.multiple_of((sub_ref[2 * g] + j) * MOE_SUB, MOE_SUB)
        xj = xs_ref[pl.ds(r0, MOE_SUB), :]
        cw = combs_ref[pl.ds(r0, MOE_SUB), :]
        hids = []
        for q in range(per_group):
            w_e = jnp.sum(jnp.where(lane == g * per_group + q, cw, 0.0), axis=1, keepdims=True)
            hids.append((jax.nn.silu(_dot(xj, wg_ref[0, q])) * _dot(xj, wu_ref[0, q]) * w_e).astype(BF16))
        ys_ref[pl.ds(r0, MOE_SUB), :] = _dot(jnp.concatenate(hids, axis=1), wd_ref[0, 0]).astype(BF16)
        return carry

    lax.fori_loop(0, sub_ref[2 * g + 1], sub_tile, 0)

    @pl.when(g == N_GROUPS - 1)
    def _unsort_and_norm():
        y = lax.dot_general(perm_ref[...], ys_ref[...], (((0,), (0,)), ((), ())),
                            preferred_element_type=F32)
        z = DN_ALPHA * x_ref[...] + ada_ref[0, 5:6, :] * y
        out_ref[...] = _layer_norm(z, g_ref[...], b_ref[...])


def _moe(x, ada_rows, w_router, b_router, w_gate, w_up, w_down, ln_g, ln_b, batch_len, tm=1024):
    n, d = x.shape
    per_b = batch_len // tm
    per_group = N_EXPERTS // N_GROUPS
    rows = tm + N_GROUPS * MOE_SUB
    wr32 = jnp.pad(w_router, ((0, 0), (0, LANES - N_EXPERTS)))
    wr_hi = wr32.astype(BF16)
    wr = jnp.concatenate([wr_hi, (wr32 - wr_hi.astype(F32)).astype(BF16)], axis=1)
    br = b_router.reshape(N_EXPERTS, 1)
    return pl.pallas_call(
        functools.partial(_moe_kernel, tm=tm, rows=rows),
        out_shape=jax.ShapeDtypeStruct((n, d), F32),
        grid=(n // tm, N_GROUPS),
        in_specs=[pl.BlockSpec((tm, d), lambda i, g: (i, 0)),
                  pl.BlockSpec((1, 6, d), lambda i, g: (i // per_b, 0, 0)),
                  pl.BlockSpec((d, 2 * LANES), lambda i, g: (0, 0)),
                  pl.BlockSpec((N_EXPERTS, 1), lambda i, g: (0, 0)),
                  pl.BlockSpec((1, per_group, d, D_EXPERT), lambda i, g: (0, g, 0, 0)),
                  pl.BlockSpec((1, per_group, d, D_EXPERT), lambda i, g: (0, g, 0, 0)),
                  pl.BlockSpec((1, 1, per_group * D_EXPERT, d), lambda i, g: (0, g, 0, 0)),
                  pl.BlockSpec((1, d), lambda i, g: (0, 0)),
                  pl.BlockSpec((1, d), lambda i, g: (0, 0))],
        out_specs=pl.BlockSpec((tm, d), lambda i, g: (i, 0)),
        scratch_shapes=[pltpu.VMEM((rows, tm), BF16),
                        pltpu.VMEM((rows, d), BF16),
                        pltpu.VMEM((rows, LANES), F32),
                        pltpu.VMEM((rows, d), BF16),
                        pltpu.SMEM((2 * N_GROUPS,), jnp.int32)],
        compiler_params=_params(2),
        name="moe",
    )(x, ada_rows, wr, br, w_gate[None].astype(BF16), w_up[None].astype(BF16),
      w_down.reshape(1, N_GROUPS, per_group * D_EXPERT, d).astype(BF16),
      ln_g.reshape(1, d), ln_b.reshape(1, d))


def kernel(x, c, w_in, b_forget, w_branch, w_merge_gate, w_out, w_ada, b_ada, ln1_g, ln1_b, ln2_g, ln2_b,
           w_router, b_router, w_exp_gate, w_exp_up, w_exp_down):
    bsz, seq, d = x.shape
    slopes = _alibi_slopes()
    ada_all = _ada(c, w_ada, b_ada).reshape(DEPTH, bsz, 6, d)
    for l in range(DEPTH):
        ada = ada_all[l]
        k, kmean, misc, qT, vT, qiT, miscT = _proj(x, ada, w_in[l])
        o_a = _dsa(qT, k, vT, qiT, misc, miscT, slopes[0])
        o_b = _dilated(qT, k, vT, slopes[1])
        o_c = _moba(qT, k, vT, kmean, slopes[2])
        o_d = _fox(qT, k, vT, misc, b_forget[l])
        x = _merge(x, ada, (o_a, o_b, o_c, o_d), w_merge_gate[l], w_branch[l], w_out[l], ln1_g[l], ln1_b[l])
        x = _moe(x.reshape(bsz * seq, d), ada, w_router, b_router, w_exp_gate[l], w_exp_up[l], w_exp_down[l],
                 ln2_g[l], ln2_b[l], seq).reshape(bsz, seq, d)
    return x
```

```python
import functools

import numpy as np
import jax
import jax.numpy as jnp
from jax import lax
from jax.experimental import pallas as pl
from jax.experimental.pallas import tpu as pltpu

D_MODEL = 1024
HEAD_DIM = 64
HEADS = 4
MIX_W = HEADS * HEAD_DIM
N_MIX = 4
IDX_HEADS = 8
IDX_DIM = 64
TOPK_KEYS = 256
MOBA_BLOCK = 256
MOBA_TOPK = 3
N_EXPERTS = 16
N_GROUPS = 4
D_EXPERT = 512
DEPTH = 2
DN_ALPHA = (2 * DEPTH) ** 0.25
LN_EPS = 1e-5
IN_COLS = 3 * N_MIX * MIX_W + IDX_HEADS * IDX_DIM + IDX_DIM + IDX_HEADS + HEADS

CH = 256
ACC_ROWS = HEAD_DIM + 16
MOE_SUB = 128
TQ_WIDE = CH
LANES = 128
MISC_W = 128
WI_ROW = IDX_DIM
FL_ROW = IDX_DIM + IDX_HEADS
LOG2E = 1.4426950408889634
NEG = -1e30
QSCALE = HEAD_DIM ** -0.5 * LOG2E
INT_MIN = -(2 ** 31)
VMEM_LIMIT = 60 * 1024 * 1024

F32 = jnp.float32
BF16 = jnp.bfloat16
HI = lax.Precision.HIGHEST
NT = (((1,), (1,)), ((), ()))


def _alibi_slopes():
    n = 3 * HEADS
    s = 2.0 ** (-8.0 * np.arange(1, n + 1) / n)
    return s.reshape(HEADS, 3).T


def _dot(a, b):
    return jnp.dot(a, b, preferred_element_type=F32)


def _dot_hi(a, b):
    return jnp.dot(a, b, preferred_element_type=F32, precision=HI)


def _bit_transpose32(words):
    a = list(words)
    j, m = 16, 0x0000FFFF
    while j:
        k = 0
        while k < 32:
            t = (a[k] ^ (a[k + j] >> j)) & m
            a[k] = a[k] ^ t
            a[k + j] = a[k + j] ^ (t << j)
            k = (k + j + 1) & ~j
        j >>= 1
        m = (m ^ (m << j)) & 0xFFFFFFFF
    return a


def _split3(x):
    x1 = x.astype(BF16)
    r1 = x - x1.astype(F32)
    x2 = r1.astype(BF16)
    x3 = (r1 - x2.astype(F32)).astype(BF16)
    return x1, x2, x3


def _params(n_axes):
    return pltpu.CompilerParams(dimension_semantics=("arbitrary",) * n_axes,
                                vmem_limit_bytes=VMEM_LIMIT)


def _layer_norm(z, g, b):
    mu = jnp.mean(z, axis=-1, keepdims=True)
    var = jnp.mean(jnp.square(z - mu), axis=-1, keepdims=True)
    return (z - mu) * lax.rsqrt(var + LN_EPS) * g + b


def _ada_kernel(c_ref, w_ref, b_ref, o_ref):
    o_ref[0] = _dot_hi(c_ref[...], w_ref[0]) + b_ref[0]


def _ada(c, w_ada, b_ada):
    depth, d, n = w_ada.shape
    bsz = c.shape[0]
    tn = D_MODEL
    return pl.pallas_call(
        _ada_kernel,
        out_shape=jax.ShapeDtypeStruct((depth, bsz, n), F32),
        grid=(depth, n // tn),
        in_specs=[pl.BlockSpec((bsz, d), lambda l, j: (0, 0)),
                  pl.BlockSpec((1, d, tn), lambda l, j: (l, 0, j)),
                  pl.BlockSpec((1, 1, tn), lambda l, j: (l, 0, j))],
        out_specs=pl.BlockSpec((1, bsz, tn), lambda l, j: (l, 0, j)),
        compiler_params=_params(2),
        name="ada",
    )(c, w_ada, b_ada.reshape(depth, 1, n))


def _proj_kernel(x_ref, ada_ref, wk_ref, wm_ref, wt_ref, wmt_ref,
                 k_ref, kmean_ref, misc_ref, qT_ref, vT_ref, qiT_ref, miscT_ref, *, tm):
    x = x_ref[0]
    sh = ada_ref[0, 0:1, :]
    sc = ada_ref[0, 1:2, :]
    h = (x * (1.0 + sc) + sh).astype(BF16)
    kf = _dot(h, wk_ref[...])
    k_ref[0] = kf.astype(BF16)
    for g in range(tm // MOBA_BLOCK):
        kmean_ref[0, 0, g:g + 1, :] = jnp.mean(kf[g * MOBA_BLOCK:(g + 1) * MOBA_BLOCK], axis=0, keepdims=True)
    misc_ref[0] = _dot(h, wm_ref[...])
    t = lax.dot_general(wt_ref[...], h, NT, preferred_element_type=F32)
    nq = N_MIX * MIX_W
    qT_ref[0] = (t[0:nq] * QSCALE).astype(BF16)
    for g in range(tm // CH):
        vT_ref[0, g] = t[nq:2 * nq, g * CH:(g + 1) * CH].astype(BF16)
    qiT_ref[0] = t[2 * nq:].astype(BF16)
    miscT_ref[0] = lax.dot_general(wmt_ref[...], h, NT, preferred_element_type=F32)


def _proj(x, ada, w_in, tm=512):
    bsz, seq, d = x.shape
    nq = N_MIX * MIX_W
    nqi = IDX_HEADS * IDX_DIM
    w = w_in.astype(BF16)
    wk = w[:, nq:2 * nq]
    wm = jnp.pad(w[:, 3 * nq + nqi:], ((0, 0), (0, MISC_W - (IN_COLS - 3 * nq - nqi))))
    wt = jnp.concatenate([w[:, 0:nq], w[:, 2 * nq:3 * nq], w[:, 3 * nq:3 * nq + nqi]], axis=1).T
    wmt = wm.T
    nt = seq // tm
    full = lambda b, i: (0, 0)
    outs = pl.pallas_call(
        functools.partial(_proj_kernel, tm=tm),
        out_shape=(jax.ShapeDtypeStruct((bsz, seq, nq), BF16),
                   jax.ShapeDtypeStruct((bsz, nt, tm // MOBA_BLOCK, nq), F32),
                   jax.ShapeDtypeStruct((bsz, seq, MISC_W), F32),
                   jax.ShapeDtypeStruct((bsz, nq, seq), BF16),
                   jax.ShapeDtypeStruct((bsz, seq // CH, nq, CH), BF16),
                   jax.ShapeDtypeStruct((bsz, nqi, seq), BF16),
                   jax.ShapeDtypeStruct((bsz, MISC_W, seq), F32)),
        grid=(bsz, nt),
        in_specs=[pl.BlockSpec((1, tm, d), lambda b, i: (b, i, 0)),
                  pl.BlockSpec((1, 6, d), lambda b, i: (b, 0, 0)),
                  pl.BlockSpec(wk.shape, full),
                  pl.BlockSpec(wm.shape, full),
                  pl.BlockSpec(wt.shape, full),
                  pl.BlockSpec(wmt.shape, full)],
        out_specs=(pl.BlockSpec((1, tm, nq), lambda b, i: (b, i, 0)),
                   pl.BlockSpec((1, 1, tm // MOBA_BLOCK, nq), lambda b, i: (b, i, 0, 0)),
                   pl.BlockSpec((1, tm, MISC_W), lambda b, i: (b, i, 0)),
                   pl.BlockSpec((1, nq, tm), lambda b, i: (b, 0, i)),
                   pl.BlockSpec((1, tm // CH, nq, CH), lambda b, i: (b, i, 0, 0)),
                   pl.BlockSpec((1, nqi, tm), lambda b, i: (b, 0, i)),
                   pl.BlockSpec((1, MISC_W, tm), lambda b, i: (b, 0, i))),
        compiler_params=_params(2),
        name="proj",
    )(x, ada, wk, wm, wt, wmt)
    k, kmean, misc, qT, vT, qiT, miscT = outs
    return k, kmean.reshape(bsz, seq // MOBA_BLOCK, nq), misc, qT, vT, qiT, miscT


def _mask_heads(qT_ref, qm_ref):
    q = qT_ref[0]
    rowh = lax.broadcasted_iota(jnp.int32, q.shape, 0) // HEAD_DIM
    for h in range(HEADS):
        qm_ref[h] = jnp.where(rowh == h, q, jnp.zeros_like(q))


def _init_state(m_ref, acc_ref):
    m_ref[...] = jnp.full(m_ref.shape, NEG, F32)
    acc_ref[...] = jnp.zeros(acc_ref.shape, F32)


def _attention(i, first, k_ref, vT_ref, st, logits_fn, chunk_ctx=None, n_diag=None):
    qm_ref, m_ref, acc_ref, s_ref, mx_ref, p_ref, alpha_ref = st
    if n_diag is None:
        n_chunks = i - first + 1
    else:
        n_chunks = n_diag + (n_diag * i - first)
    ones = jnp.ones((ACC_ROWS - HEAD_DIM, CH), BF16)

    def chunk_of(n):
        n = jnp.clip(n, 0, n_chunks - 1)
        if n_diag is None:
            return first + n
        return jnp.where(n < n_diag, n_diag * i + n, first + n - n_diag)

    def position(n):
        if n_diag is None:
            return None
        return n if n < n_diag else -1

    def logits_head(h, slot, c, kc, ctx, diag, pen):
        x = logits_fn(h, c, _dot(kc, qm_ref[h]), ctx, diag, pen)
        s_ref[slot, h] = x
        mx_ref[slot, h] = jnp.max(x, axis=0, keepdims=True)

    def softmax_head(h, slot):
        m_old = m_ref[h]
        m_new = jnp.maximum(m_old, mx_ref[slot, h])
        alpha_ref[slot, h] = jnp.exp2(m_old - m_new)
        p_ref[slot, h] = jnp.exp2(s_ref[slot, h] - m_new).astype(BF16)
        m_ref[h] = m_new

    def pv_head(h, slot, vc):
        v1 = jnp.concatenate([vc[h * HEAD_DIM:(h + 1) * HEAD_DIM, :], ones], axis=0)
        acc_ref[h] = alpha_ref[slot, h] * acc_ref[h] + _dot(v1, p_ref[slot, h])

    def step(n, a, diag):
        c = chunk_of(n + 1)
        pen = jnp.where(n + 1 < n_chunks, 0.0, NEG)
        kc = k_ref[0, pl.ds(pl.multiple_of(c * CH, CH), CH), :]
        vc = vT_ref[0, chunk_of(n - 1)]
        ctx = chunk_ctx(c) if chunk_ctx is not None else None
        for h in range(HEADS):
            softmax_head(h, a)
        for h in range(HEADS):
            pv_head(h, 1 - a, vc)
        for h in range(HEADS):
            logits_head(h, 1 - a, c, kc, ctx, diag, pen)

    p_ref[1] = jnp.zeros(p_ref.shape[1:], BF16)
    alpha_ref[1] = jnp.ones(alpha_ref.shape[1:], F32)
    c0 = chunk_of(jnp.int32(0))
    k0 = k_ref[0, pl.ds(pl.multiple_of(c0 * CH, CH), CH), :]
    ctx0 = chunk_ctx(c0) if chunk_ctx is not None else None
    for h in range(HEADS):
        logits_head(h, 0, c0, k0, ctx0, position(0), 0.0)

    first_trip = 0
    if n_diag is not None and n_diag > 1:
        step(0, 0, position(1))
        step(1, 1, position(2))
        first_trip = 1
    later = position(2 * first_trip + 1)

    def pair(t, carry):
        step(2 * t, 0, later)
        step(2 * t + 1, 1, later)
        return carry

    n_trips = (n_chunks + 1) // 2
    lax.fori_loop(first_trip, n_trips, pair, 0)
    v_last = vT_ref[0, chunk_of(2 * n_trips - 1)]
    for h in range(HEADS):
        pv_head(h, 1, v_last)


def _finish(o_ref, acc_ref):
    parts = [acc_ref[h, 0:HEAD_DIM, :] / acc_ref[h, HEAD_DIM:HEAD_DIM + 1, :] for h in range(HEADS)]
    o_ref[0] = jnp.concatenate(parts, axis=0).T.astype(BF16)


def _lanes(tile, tq):
    return tile if tq == LANES else jnp.concatenate([tile] * (tq // LANES), axis=1)


def _causal_neg(tq, d=0):
    s_i = lax.broadcasted_iota(jnp.int32, (CH, tq), 0) + d * CH
    t_i = lax.broadcasted_iota(jnp.int32, (CH, tq), 1)
    return jnp.where(s_i <= t_i, 0.0, NEG).astype(F32)


def _key_pos_bias(bias_ref, slopes):
    s_i = lax.broadcasted_iota(jnp.int32, (CH, LANES), 0).astype(F32)
    for h in range(HEADS):
        bias_ref[h] = s_i * float(slopes[h] * LOG2E)


def _attn_specs(mixer, seq, tq=CH):
    return [pl.BlockSpec((1, MIX_W, tq), lambda b, i: (b, mixer, i)),
            pl.BlockSpec((1, seq, MIX_W), lambda b, i: (b, 0, mixer)),
            pl.BlockSpec((1, seq // CH, MIX_W, CH), lambda b, i: (b, 0, mixer, 0))]


def _attn_scratch(tq=CH):
    return [pltpu.VMEM((HEADS, MIX_W, tq), BF16),
            pltpu.VMEM((HEADS, 1, tq), F32),
            pltpu.VMEM((HEADS, ACC_ROWS, tq), F32),
            pltpu.VMEM((2, HEADS, CH, tq), F32),
            pltpu.VMEM((2, HEADS, 1, tq), F32),
            pltpu.VMEM((2, HEADS, CH, tq), BF16),
            pltpu.VMEM((2, HEADS, 1, tq), F32)]


def _fox_kernel(qT_ref, k_ref, vT_ref, misc_ref, bf_ref, o_ref,
                cum_ref, *st, n_chunks):
    qm_ref, m_ref, acc_ref = st[:3]
    i = pl.program_id(1)

    @pl.when(i == 0)
    def _cumulative_gates():
        rr = lax.broadcasted_iota(jnp.int32, (CH, CH), 0)
        cc = lax.broadcasted_iota(jnp.int32, (CH, CH), 1)
        tri = jnp.where(cc <= rr, 1.0, 0.0).astype(BF16)
        erow = lax.broadcasted_iota(jnp.int32, (LANES, LANES), 0)

        def body(blk, carry):
            off = pl.multiple_of(blk * CH, CH)
            z = misc_ref[0, pl.ds(off, CH), :] + bf_ref[...]
            ls = jnp.minimum(z, 0.0) - jnp.log1p(jnp.exp(-jnp.abs(z)))
            csum = sum(_dot(tri, part) for part in _split3(ls))
            parts = _split3(csum)
            new = []
            for h in range(HEADS):
                sel = jnp.where(erow == FL_ROW + h, 1.0, 0.0).astype(BF16)
                cum = sum(_dot(part, sel) for part in parts) + carry[h]
                cum_ref[h, pl.ds(off, CH), :] = cum * LOG2E
                new.append(cum[CH - 1:CH, :])
            return tuple(new)

        lax.fori_loop(0, n_chunks, body, tuple(jnp.zeros((1, LANES), F32) for _ in range(HEADS)))

    _mask_heads(qT_ref, qm_ref)
    _init_state(m_ref, acc_ref)

    def logits(h, c, qk, _, diag, pen):
        off = pl.multiple_of(c * CH, CH)
        cum = cum_ref[h, pl.ds(off, CH), :]
        if diag >= 0:
            return qk - _lanes(cum, TQ_WIDE) + _causal_neg(TQ_WIDE, diag)
        return qk - _lanes(cum - pen, TQ_WIDE)

    _attention(i, 0, k_ref, vT_ref, st, logits, n_diag=TQ_WIDE // CH)
    _finish(o_ref, acc_ref)


def _fox(qT, k, vT, misc, b_forget_l):
    bsz, seq, _ = k.shape
    n_chunks = seq // CH
    bf = jnp.zeros((1, MISC_W), F32).at[0, FL_ROW:FL_ROW + HEADS].set(b_forget_l)
    return pl.pallas_call(
        functools.partial(_fox_kernel, n_chunks=n_chunks),
        out_shape=jax.ShapeDtypeStruct((bsz, seq, MIX_W), BF16),
        grid=(bsz, seq // TQ_WIDE),
        in_specs=_attn_specs(3, seq, TQ_WIDE) + [
            pl.BlockSpec((1, seq, MISC_W), lambda b, i: (b, 0, 0)),
            pl.BlockSpec((1, MISC_W), lambda b, i: (0, 0))],
        out_specs=pl.BlockSpec((1, TQ_WIDE, MIX_W), lambda b, i: (b, i, 0)),
        scratch_shapes=[pltpu.VMEM((HEADS, seq, LANES), F32)] + _attn_scratch(TQ_WIDE),
        compiler_params=_params(2),
        name="fox",
    )(qT, k, vT, misc, bf)


def _moba_kernel(qT_ref, k_ref, vT_ref, kmean_ref, o_ref,
                 bias_ref, rowadd_ref, *st, slopes, n_blocks):
    qm_ref, m_ref, acc_ref = st[:3]
    i = pl.program_id(1)
    tq = TQ_WIDE
    n_diag = tq // CH
    _mask_heads(qT_ref, qm_ref)
    _init_state(m_ref, acc_ref)

    @pl.when(jnp.logical_and(pl.program_id(0) == 0, i == 0))
    def _bias_table():
        s_f = lax.broadcasted_iota(jnp.int32, (CH, tq), 0).astype(F32)
        for h in range(HEADS):
            pos = s_f * float(slopes[h] * LOG2E)
            bias_ref[0, h] = pos
            for d in range(n_diag):
                bias_ref[1 + d, h] = pos + _causal_neg(tq, d)

    n_i = lax.broadcasted_iota(jnp.int32, (n_blocks, tq), 0)
    n_f = n_i.astype(F32)
    own = n_diag * i + lax.broadcasted_iota(jnp.int32, (n_blocks, tq), 1) // MOBA_BLOCK
    past = n_i < own
    kmean_parts = _split3(kmean_ref[0])
    for h in range(HEADS):
        gate = sum(_dot(part, qm_ref[h]) for part in kmean_parts)
        gate = jnp.where(past, gate, -jnp.inf)
        chosen = jnp.zeros((n_blocks, tq), F32)
        for _ in range(MOBA_TOPK):
            mx = jnp.max(gate, axis=0, keepdims=True)
            first = jnp.min(jnp.where(gate == mx, n_f, float(n_blocks)), axis=0, keepdims=True)
            pick = n_f == first
            chosen = jnp.where(pick, 1.0, chosen)
            gate = jnp.where(pick, -jnp.inf, gate)
        chosen = jnp.where(past, chosen, 0.0)
        blk_shift = n_f * float(slopes[h] * LOG2E * MOBA_BLOCK)
        rowadd_ref[h] = jnp.where(jnp.logical_or(chosen > 0.5, n_i == own), blk_shift, NEG)

    def logits(h, c, qk, _, diag, pen):
        row = rowadd_ref[h, pl.ds(c, 1), :]
        if diag >= 0:
            return qk + bias_ref[1 + diag, h] + row
        return qk + bias_ref[0, h] + (row + pen)

    _attention(i, 0, k_ref, vT_ref, st, logits, n_diag=n_diag)
    _finish(o_ref, acc_ref)


def _moba(qT, k, vT, kmean, slopes):
    bsz, seq, _ = k.shape
    n_blocks = seq // MOBA_BLOCK
    return pl.pallas_call(
        functools.partial(_moba_kernel, slopes=tuple(float(s) for s in slopes), n_blocks=n_blocks),
        out_shape=jax.ShapeDtypeStruct((bsz, seq, MIX_W), BF16),
        grid=(bsz, seq // TQ_WIDE),
        in_specs=_attn_specs(2, seq, TQ_WIDE) + [
            pl.BlockSpec((1, n_blocks, MIX_W), lambda b, i: (b, 0, 2))],
        out_specs=pl.BlockSpec((1, TQ_WIDE, MIX_W), lambda b, i: (b, i, 0)),
        scratch_shapes=[pltpu.VMEM((1 + TQ_WIDE // CH, HEADS, CH, TQ_WIDE), F32),
                        pltpu.VMEM((HEADS, n_blocks, TQ_WIDE), F32)] + _attn_scratch(TQ_WIDE),
        compiler_params=_params(2),
        name="moba",
    )(qT, k, vT, kmean)


DIL_SPAN = 2048 // CH + 1


def _dilated_kernel(qT_ref, k_ref, vT_ref, o_ref, table_ref, *st, slopes):
    qm_ref, m_ref, acc_ref = st[:3]
    b = pl.program_id(0)
    i = pl.program_id(1)

    tq = TQ_WIDE
    n_diag = tq // CH
    n_entries = n_diag - 1 + DIL_SPAN

    @pl.when(jnp.logical_and(b == 0, i == 0))
    def _bias_table():
        s_i = lax.broadcasted_iota(jnp.int32, (CH, tq), 0)
        t_i = lax.broadcasted_iota(jnp.int32, (CH, tq), 1)
        for e in range(n_entries):
            j = e - (n_diag - 1)
            d = t_i - s_i + j * CH
            ok = d >= 0
            mult = (jnp.where(jnp.logical_and(ok, d <= 128), 1.0, 0.0)
                    + jnp.where(jnp.logical_and(ok, jnp.logical_and(d <= 512, (d & 3) == 0)), 1.0, 0.0)
                    + jnp.where(jnp.logical_and(ok, jnp.logical_and(d <= 2048, (d & 15) == 0)), 1.0, 0.0))
            logm = jnp.where(mult > 0.5, jnp.log2(jnp.maximum(mult, 1.0)), NEG)
            df = d.astype(F32)
            for h in range(HEADS):
                table_ref[h, e] = logm - df * float(slopes[h] * LOG2E)
        for h in range(HEADS):
            table_ref[h, n_entries] = jnp.full((CH, tq), NEG, F32)

    _mask_heads(qT_ref, qm_ref)
    _init_state(m_ref, acc_ref)

    def logits(h, c, qk, _, diag, pen):
        if diag >= 0:
            return qk + table_ref[h, n_diag - 1 - diag]
        return qk + table_ref[h, jnp.where(pen < 0.0, n_entries, n_diag * i - c + (n_diag - 1))]

    _attention(i, jnp.maximum(n_diag * i - (DIL_SPAN - 1), 0), k_ref, vT_ref, st, logits, n_diag=n_diag)
    _finish(o_ref, acc_ref)


def _dilated(qT, k, vT, slopes):
    bsz, seq, _ = k.shape
    return pl.pallas_call(
        functools.partial(_dilated_kernel, slopes=tuple(float(s) for s in slopes)),
        out_shape=jax.ShapeDtypeStruct((bsz, seq, MIX_W), BF16),
        grid=(bsz, seq // TQ_WIDE),
        in_specs=_attn_specs(1, seq, TQ_WIDE),
        out_specs=pl.BlockSpec((1, TQ_WIDE, MIX_W), lambda b, i: (b, i, 0)),
        scratch_shapes=[pltpu.VMEM((HEADS, TQ_WIDE // CH + DIL_SPAN, CH, TQ_WIDE), F32)]
        + _attn_scratch(TQ_WIDE),
        compiler_params=_params(2),
        name="dilated",
    )(qT, k, vT)


def _dsa_kernel(qT_ref, k_ref, vT_ref, qiT_ref, misc_ref, miscT_ref, o_ref,
                bias_ref, qi_ref, key_ref, planes_ref, alive_ref, seen_ref, *st, slopes):
    qm_ref, m_ref, acc_ref = st[:3]
    i = pl.program_id(1)
    tq = CH

    @pl.when(jnp.logical_and(pl.program_id(0) == 0, i == 0))
    def _clear_planes():
        planes_ref[0:32] = jnp.zeros((32,) + planes_ref.shape[1:], jnp.int32)
        planes_ref[32] = jnp.full(planes_ref.shape[1:], -1, jnp.int32)
    _mask_heads(qT_ref, qm_ref)
    _init_state(m_ref, acc_ref)
    _key_pos_bias(bias_ref, slopes)

    qi_all = qiT_ref[0]
    zpad = jnp.zeros((MISC_W - IDX_DIM, tq), BF16)
    for h in range(IDX_HEADS):
        qi_ref[h] = jnp.concatenate([qi_all[h * IDX_DIM:(h + 1) * IDX_DIM], zpad], axis=0)
    w_rows = miscT_ref[0, WI_ROW:WI_ROW + IDX_HEADS, :] * float(IDX_HEADS ** -0.5 * IDX_DIM ** -0.5)

    s_i = lax.broadcasted_iota(jnp.int32, (CH, tq), 0)
    t_i = lax.broadcasted_iota(jnp.int32, (CH, tq), 1)

    def score_chunk(c, diag):
        off = pl.multiple_of(c * CH, CH)
        ki = misc_ref[0, pl.ds(off, CH), :].astype(BF16)
        score = jnp.zeros((CH, tq), F32)
        for h in range(IDX_HEADS):
            rel = jnp.maximum(_dot(ki, qi_ref[h]), 0.0)
            score = score + rel * w_rows[h:h + 1, :]
        bits = lax.bitcast_convert_type(score, jnp.int32)
        key = bits ^ ((bits >> 31) & 0x7FFFFFFF)
        key = jnp.where(key == -1, 0, key)
        if diag:
            key = jnp.where(s_i <= t_i, key, INT_MIN)
        key_ref[pl.ds(off, CH), :] = key
        ukey = key ^ INT_MIN
        words = _bit_transpose32([ukey[8 * j:8 * j + 8, :] for j in range(32)])
        row0 = pl.multiple_of(c * 8, 8)
        for b in range(32):
            planes_ref[b, pl.ds(row0, 8), :] = words[31 - b]

    score_chunk(i, True)

    def score_pair(t, carry):
        score_chunk(2 * t, False)
        score_chunk(jnp.minimum(2 * t + 1, i - 1), False)
        return carry

    lax.fori_loop(0, (i + 1) // 2, score_pair, 0)
    group = 4 * 8
    n_groups = i // 4 + 1
    alive_ref[...] = jnp.where(lax.broadcasted_iota(jnp.int32, alive_ref.shape, 0) < (i + 1) * 8, -1, 0)

    def narrow(alive, rows, b_prev, took_prev):
        hit = alive & planes_ref[b_prev, rows, :]
        return jnp.where(took_prev, hit, alive ^ hit)

    def bit_step(n, state):
        thr, need, took_prev = state
        b = 31 - n
        took_prev = took_prev != 0

        def sweep(g, cnt):
            rows = pl.ds(pl.multiple_of(g * group, group), group)
            alive = narrow(alive_ref[rows, :], rows, b + 1, took_prev)
            alive_ref[rows, :] = alive
            return cnt + lax.population_count(alive & planes_ref[b, rows, :])

        cnt = lax.fori_loop(0, n_groups, sweep, jnp.zeros((group, tq), jnp.int32))
        cnt = jnp.sum(cnt, axis=0, keepdims=True)
        ok = cnt >= need
        return (jnp.where(ok, thr | jnp.left_shift(jnp.int32(1), b), thr), jnp.where(ok, need, need - cnt),
                jnp.where(ok, 1, 0))

    uthr, n_take, took_last = lax.fori_loop(
        0, 32, bit_step, (jnp.zeros((1, tq), jnp.int32), jnp.full((1, tq), TOPK_KEYS, jnp.int32),
                          jnp.ones((1, tq), jnp.int32)))
    thr = uthr ^ INT_MIN

    def count_alive(g, cnt):
        rows = pl.ds(pl.multiple_of(g * group, group), group)
        return cnt + lax.population_count(narrow(alive_ref[rows, :], rows, 0, took_last != 0))

    n_eq = jnp.sum(lax.fori_loop(0, n_groups, count_alive, jnp.zeros((group, tq), jnp.int32)), axis=0, keepdims=True)
    tied = jnp.logical_and(n_eq > n_take, thr != INT_MIN)
    any_tied = jnp.max(jnp.where(tied, 1.0, 0.0)) > 0.5

    def logits(h, c, qk, neg, diag, pen):
        pos = bias_ref[h] + (c.astype(F32) * float(slopes[h] * LOG2E * CH) + pen)
        return qk + _lanes(pos, tq) + neg

    @pl.when(jnp.logical_not(any_tied))
    def _no_ties():
        thr_lo = jnp.where(thr == INT_MIN, INT_MIN + 1, thr)

        def selection(c):
            off = pl.multiple_of(c * CH, CH)
            return jnp.where(key_ref[pl.ds(off, CH), :] >= thr_lo, 0.0, NEG)

        _attention(i, 0, k_ref, vT_ref, st, logits, selection)

    @pl.when(any_tied)
    def _ties():
        n_first = jnp.where(thr == INT_MIN, 0, n_take).astype(F32)
        rr = lax.broadcasted_iota(jnp.int32, (CH, CH), 0)
        cc = lax.broadcasted_iota(jnp.int32, (CH, CH), 1)
        tri = jnp.where(cc <= rr, 1.0, 0.0).astype(BF16)
        seen_ref[...] = jnp.zeros(seen_ref.shape, F32)

        def selection(c):
            off = pl.multiple_of(c * CH, CH)
            key = key_ref[pl.ds(off, CH), :]
            eq = key == thr
            rank = _dot(tri, jnp.where(eq, 1.0, 0.0).astype(BF16)) + seen_ref[...]
            seen_ref[...] = rank[CH - 1:CH, :]
            take = jnp.logical_or(key > thr, jnp.logical_and(eq, rank <= n_first))
            return jnp.where(take, 0.0, NEG)

        _attention(i, 0, k_ref, vT_ref, st, logits, selection)

    _finish(o_ref, acc_ref)


def _dsa(qT, k, vT, qiT, misc, miscT, slopes):
    bsz, seq, _ = k.shape
    nqi = IDX_HEADS * IDX_DIM
    plane_rows = -(-(seq // CH) // 4) * 32
    return pl.pallas_call(
        functools.partial(_dsa_kernel, slopes=tuple(float(s) for s in slopes)),
        out_shape=jax.ShapeDtypeStruct((bsz, seq, MIX_W), BF16),
        grid=(bsz, seq // CH),
        in_specs=_attn_specs(0, seq) + [
            pl.BlockSpec((1, nqi, CH), lambda b, i: (b, 0, i)),
            pl.BlockSpec((1, seq, MISC_W), lambda b, i: (b, 0, 0)),
            pl.BlockSpec((1, MISC_W, CH), lambda b, i: (b, 0, i))],
        out_specs=pl.BlockSpec((1, CH, MIX_W), lambda b, i: (b, i, 0)),
        scratch_shapes=[pltpu.VMEM((HEADS, CH, LANES), F32),
                        pltpu.VMEM((IDX_HEADS, MISC_W, CH), BF16),
                        pltpu.VMEM((seq, CH), jnp.int32),
                        pltpu.VMEM((33, plane_rows, CH), jnp.int32),
                        pltpu.VMEM((plane_rows, CH), jnp.int32),
                        pltpu.VMEM((1, CH), F32)] + _attn_scratch(),
        compiler_params=_params(2),
        name="dsa",
    )(qT, k, vT, qiT, misc, miscT)


def _merge_kernel(x_ref, ada_ref, oa_ref, ob_ref, oc_ref, od_ref, wmg_ref, wbr_ref, wout_ref,
                  g_ref, b_ref, out_ref):
    x = x_ref[0]
    sh = ada_ref[0, 0:1, :]
    sc = ada_ref[0, 1:2, :]
    g1 = ada_ref[0, 2:3, :]
    h = (x * (1.0 + sc) + sh).astype(BF16)
    mixed = None
    for m, o_ref in enumerate((oa_ref, ob_ref, oc_ref, od_ref)):
        gate = jax.nn.sigmoid(_dot(h, wmg_ref[m]))
        term = gate * _dot(o_ref[0], wbr_ref[m])
        mixed = term if mixed is None else mixed + term
    y = _dot(mixed.astype(BF16), wout_ref[...])
    out_ref[0] = _layer_norm(DN_ALPHA * x + g1 * y, g_ref[...], b_ref[...])


def _merge(x, ada, outs, w_mg, w_br, w_out, ln_g, ln_b, tm=512):
    bsz, seq, d = x.shape
    o_spec = pl.BlockSpec((1, tm, MIX_W), lambda b, i: (b, i, 0))
    return pl.pallas_call(
        _merge_kernel,
        out_shape=jax.ShapeDtypeStruct((bsz, seq, d), F32),
        grid=(bsz, seq // tm),
        in_specs=[pl.BlockSpec((1, tm, d), lambda b, i: (b, i, 0)),
                  pl.BlockSpec((1, 6, d), lambda b, i: (b, 0, 0)),
                  o_spec, o_spec, o_spec, o_spec,
                  pl.BlockSpec((N_MIX, d, d), lambda b, i: (0, 0, 0)),
                  pl.BlockSpec((N_MIX, MIX_W, d), lambda b, i: (0, 0, 0)),
                  pl.BlockSpec((d, d), lambda b, i: (0, 0)),
                  pl.BlockSpec((1, d), lambda b, i: (0, 0)),
                  pl.BlockSpec((1, d), lambda b, i: (0, 0))],
        out_specs=pl.BlockSpec((1, tm, d), lambda b, i: (b, i, 0)),
        compiler_params=_params(2),
        name="merge",
    )(x, ada, *outs, w_mg.astype(BF16), w_br.astype(BF16), w_out.astype(BF16),
      ln_g.reshape(1, d), ln_b.reshape(1, d))


def _route(logits_t, bias_col):
    per_group = N_EXPERTS // N_GROUPS
    scores = jax.nn.sigmoid(logits_t)
    biased = scores + bias_col
    s_rows = [scores[e:e + 1] for e in range(N_EXPERTS)]
    b_rows = [biased[e:e + 1] for e in range(N_EXPERTS)]
    best_g = None
    for g in range(N_GROUPS):
        r = b_rows[g * per_group:(g + 1) * per_group]
        gs = None
        for a in range(per_group):
            for b in range(a + 1, per_group):
                pair = r[a] + r[b]
                gs = pair if gs is None else jnp.maximum(gs, pair)
        if best_g is None:
            best_g, best_v = jnp.zeros_like(gs, dtype=jnp.int32), gs
        else:
            better = gs > best_v
            best_g = jnp.where(better, g, best_g)
            best_v = jnp.maximum(best_v, gs)
    masked = [jnp.where(best_g == e // per_group, b_rows[e], -jnp.inf) for e in range(N_EXPERTS)]

    def argmax_first(rows):
        idx, val = jnp.zeros_like(best_g), rows[0]
        for e in range(1, N_EXPERTS):
            better = rows[e] > val
            idx = jnp.where(better, e, idx)
            val = jnp.maximum(val, rows[e])
        return idx

    e1 = argmax_first(masked)
    e2 = argmax_first([jnp.where(e1 == e, -jnp.inf, masked[e]) for e in range(N_EXPERTS)])
    s1 = sum(jnp.where(e1 == e, s_rows[e], 0.0) for e in range(N_EXPERTS))
    s2 = sum(jnp.where(e2 == e, s_rows[e], 0.0) for e in range(N_EXPERTS))
    tot = s1 + s2
    w1, w2 = s1 / tot, s2 / tot
    rows = [jnp.where(e1 == e, w1, 0.0) + jnp.where(e2 == e, w2, 0.0) for e in range(N_EXPERTS)]
    return jnp.concatenate(rows, axis=0), best_g


def _moe_kernel(x_ref, ada_ref, wr_ref, br_ref, wg_ref, wu_ref, wd_ref, g_ref, b_ref, out_ref,
                perm_ref, xs_ref, combs_ref, ys_ref, sub_ref, *, tm, rows):
    g = pl.program_id(1)
    per_group = N_EXPERTS // N_GROUPS

    @pl.when(g == 0)
    def _route_and_sort():
        x = x_ref[...]
        h2 = x * (1.0 + ada_ref[0, 4:5, :]) + ada_ref[0, 3:4, :]
        h_hi = h2.astype(BF16)
        h_lo = (h2 - h_hi.astype(F32)).astype(BF16)
        first = _dot(h_hi, wr_ref[...])
        logits = first[:, 0:LANES] + first[:, LANES:] + _dot(h_lo, wr_ref[:, 0:LANES])
        logits_t = logits.T[0:N_EXPERTS]
        comb_t, best_g = _route(logits_t, br_ref[...])
        member = [jnp.where(best_g == g, 1.0, 0.0) for g in range(N_GROUPS)]
        grp = jnp.concatenate(member + [jnp.zeros((8 - N_GROUPS, tm), F32)], axis=0).astype(BF16)
        s_i = lax.broadcasted_iota(jnp.int32, (tm, tm), 0)
        t_i = lax.broadcasted_iota(jnp.int32, (tm, tm), 1)
        rank = _dot(grp, jnp.where(s_i <= t_i, 1.0, 0.0).astype(BF16))
        pos = jnp.zeros((1, tm), F32)
        start = jnp.zeros((1, 1), F32)
        for g in range(N_GROUPS):
            cap = jnp.ceil(rank[g:g + 1, tm - 1:tm] * (1.0 / MOE_SUB)) * MOE_SUB
            pos = pos + member[g] * (start + rank[g:g + 1] - 1.0)
            sub_ref[2 * g] = (jnp.sum(start) * (1.0 / MOE_SUB)).astype(jnp.int32)
            sub_ref[2 * g + 1] = (jnp.sum(cap) * (1.0 / MOE_SUB)).astype(jnp.int32)
            start = start + cap
        r_f = lax.broadcasted_iota(jnp.int32, (rows, tm), 0).astype(F32)
        perm = jnp.where(r_f == pos, 1.0, 0.0).astype(BF16)
        perm_ref[...] = perm
        xs_ref[...] = _dot(perm, h_hi).astype(BF16)
        comb = jnp.concatenate([comb_t, jnp.zeros((LANES - N_EXPERTS, tm), F32)], axis=0).T
        c_hi = comb.astype(BF16)
        c_lo = (comb - c_hi.astype(F32)).astype(BF16)
        combs_ref[...] = _dot(perm, c_hi) + _dot(perm, c_lo)
        ys_ref[...] = jnp.zeros(ys_ref.shape, BF16)

    lane = lax.broadcasted_iota(jnp.int32, (MOE_SUB, LANES), 1)

    def sub_tile(j, carry):
        r0 = pl.multiple_of((sub_ref[2 * g] + j) * MOE_SUB, MOE_SUB)
        xj = xs_ref[pl.ds(r0, MOE_SUB), :]
        cw = combs_ref[pl.ds(r0, MOE_SUB), :]
        hids = []
        for q in range(per_group):
            w_e = jnp.sum(jnp.where(lane == g * per_group + q, cw, 0.0), axis=1, keepdims=True)
            hids.append((jax.nn.silu(_dot(xj, wg_ref[0, q])) * _dot(xj, wu_ref[0, q]) * w_e).astype(BF16))
        ys_ref[pl.ds(r0, MOE_SUB), :] = _dot(jnp.concatenate(hids, axis=1), wd_ref[0, 0]).astype(BF16)
        return carry

    lax.fori_loop(0, sub_ref[2 * g + 1], sub_tile, 0)

    @pl.when(g == N_GROUPS - 1)
    def _unsort_and_norm():
        y = lax.dot_general(perm_ref[...], ys_ref[...], (((0,), (0,)), ((), ())),
                            preferred_element_type=F32)
        z = DN_ALPHA * x_ref[...] + ada_ref[0, 5:6, :] * y
        out_ref[...] = _layer_norm(z, g_ref[...], b_ref[...])


def _moe(x, ada_rows, w_router, b_router, w_gate, w_up, w_down, ln_g, ln_b, batch_len, tm=1024):
    n, d = x.shape
    per_b = batch_len // tm
    per_group = N_EXPERTS // N_GROUPS
    rows = tm + N_GROUPS * MOE_SUB
    wr32 = jnp.pad(w_router, ((0, 0), (0, LANES - N_EXPERTS)))
    wr_hi = wr32.astype(BF16)
    wr = jnp.concatenate([wr_hi, (wr32 - wr_hi.astype(F32)).astype(BF16)], axis=1)
    br = b_router.reshape(N_EXPERTS, 1)
    return pl.pallas_call(
        functools.partial(_moe_kernel, tm=tm, rows=rows),
        out_shape=jax.ShapeDtypeStruct((n, d), F32),
        grid=(n // tm, N_GROUPS),
        in_specs=[pl.BlockSpec((tm, d), lambda i, g: (i, 0)),
                  pl.BlockSpec((1, 6, d), lambda i, g: (i // per_b, 0, 0)),
                  pl.BlockSpec((d, 2 * LANES), lambda i, g: (0, 0)),
                  pl.BlockSpec((N_EXPERTS, 1), lambda i, g: (0, 0)),
                  pl.BlockSpec((1, per_group, d, D_EXPERT), lambda i, g: (0, g, 0, 0)),
                  pl.BlockSpec((1, per_group, d, D_EXPERT), lambda i, g: (0, g, 0, 0)),
                  pl.BlockSpec((1, 1, per_group * D_EXPERT, d), lambda i, g: (0, g, 0, 0)),
                  pl.BlockSpec((1, d), lambda i, g: (0, 0)),
                  pl.BlockSpec((1, d), lambda i, g: (0, 0))],
        out_specs=pl.BlockSpec((tm, d), lambda i, g: (i, 0)),
        scratch_shapes=[pltpu.VMEM((rows, tm), BF16),
                        pltpu.VMEM((rows, d), BF16),
                        pltpu.VMEM((rows, LANES), F32),
                        pltpu.VMEM((rows, d), BF16),
                        pltpu.SMEM((2 * N_GROUPS,), jnp.int32)],
        compiler_params=_params(2),
        name="moe",
    )(x, ada_rows, wr, br, w_gate[None].astype(BF16), w_up[None].astype(BF16),
      w_down.reshape(1, N_GROUPS, per_group * D_EXPERT, d).astype(BF16),
      ln_g.reshape(1, d), ln_b.reshape(1, d))


def kernel(x, c, w_in, b_forget, w_branch, w_merge_gate, w_out, w_ada, b_ada, ln1_g, ln1_b, ln2_g, ln2_b,
           w_router, b_router, w_exp_gate, w_exp_up, w_exp_down):
    bsz, seq, d = x.shape
    slopes = _alibi_slopes()
    ada_all = _ada(c, w_ada, b_ada).reshape(DEPTH, bsz, 6, d)
    for l in range(DEPTH):
        ada = ada_all[l]
        k, kmean, misc, qT, vT, qiT, miscT = _proj(x, ada, w_in[l])
        o_a = _dsa(qT, k, vT, qiT, misc, miscT, slopes[0])
        o_b = _dilated(qT, k, vT, slopes[1])
        o_c = _moba(qT, k, vT, kmean, slopes[2])
        o_d = _fox(qT, k, vT, misc, b_forget[l])
        x = _merge(x, ada, (o_a, o_b, o_c, o_d), w_merge_gate[l], w_branch[l], w_out[l], ln1_g[l], ln1_b[l])
        x = _moe(x.reshape(bsz * seq, d), ada, w_router, b_router, w_exp_gate[l], w_exp_up[l], w_exp_down[l],
                 ln2_g[l], ln2_b[l], seq).reshape(bsz, seq, d)
    return x
```

```python
import functools

import numpy as np
import jax
import jax.numpy as jnp
from jax import lax
from jax.experimental import pallas as pl
from jax.experimental.pallas import tpu as pltpu

D_MODEL = 1024
HEAD_DIM = 64
HEADS = 4
MIX_W = HEADS * HEAD_DIM
N_MIX = 4
IDX_HEADS = 8
IDX_DIM = 64
TOPK_KEYS = 256
MOBA_BLOCK = 256
MOBA_TOPK = 3
N_EXPERTS = 16
N_GROUPS = 4
D_EXPERT = 512
DEPTH = 2
DN_ALPHA = (2 * DEPTH) ** 0.25
LN_EPS = 1e-5
IN_COLS = 3 * N_MIX * MIX_W + IDX_HEADS * IDX_DIM + IDX_DIM + IDX_HEADS + HEADS

CH = 256
ACC_ROWS = HEAD_DIM + 16
MOE_SUB = 128
TQ_WIDE = CH
LANES = 128
MISC_W = 128
WI_ROW = IDX_DIM
FL_ROW = IDX_DIM + IDX_HEADS
LOG2E = 1.4426950408889634
NEG = -1e30
QSCALE = HEAD_DIM ** -0.5 * LOG2E
INT_MIN = -(2 ** 31)
VMEM_LIMIT = 60 * 1024 * 1024

F32 = jnp.float32
BF16 = jnp.bfloat16
HI = lax.Precision.HIGHEST
NT = (((1,), (1,)), ((), ()))


def _alibi_slopes():
    n = 3 * HEADS
    s = 2.0 ** (-8.0 * np.arange(1, n + 1) / n)
    return s.reshape(HEADS, 3).T


def _dot(a, b):
    return jnp.dot(a, b, preferred_element_type=F32)


def _dot_hi(a, b):
    return jnp.dot(a, b, preferred_element_type=F32, precision=HI)


def _bit_transpose32(words):
    a = list(words)
    j, m = 16, 0x0000FFFF
    while j:
        k = 0
        while k < 32:
            t = (a[k] ^ (a[k + j] >> j)) & m
            a[k] = a[k] ^ t
            a[k + j] = a[k + j] ^ (t << j)
            k = (k + j + 1) & ~j
        j >>= 1
        m = (m ^ (m << j)) & 0xFFFFFFFF
    return a


def _split3(x):
    x1 = x.astype(BF16)
    r1 = x - x1.astype(F32)
    x2 = r1.astype(BF16)
    x3 = (r1 - x2.astype(F32)).astype(BF16)
    return x1, x2, x3


def _params(n_axes):
    return pltpu.CompilerParams(dimension_semantics=("arbitrary",) * n_axes,
                                vmem_limit_bytes=VMEM_LIMIT)


def _layer_norm(z, g, b):
    mu = jnp.mean(z, axis=-1, keepdims=True)
    var = jnp.mean(jnp.square(z - mu), axis=-1, keepdims=True)
    return (z - mu) * lax.rsqrt(var + LN_EPS) * g + b


def _ada_kernel(c_ref, w_ref, b_ref, o_ref):
    o_ref[0] = _dot_hi(c_ref[...], w_ref[0]) + b_ref[0]


def _ada(c, w_ada, b_ada):
    depth, d, n = w_ada.shape
    bsz = c.shape[0]
    tn = D_MODEL
    return pl.pallas_call(
        _ada_kernel,
        out_shape=jax.ShapeDtypeStruct((depth, bsz, n), F32),
        grid=(depth, n // tn),
        in_specs=[pl.BlockSpec((bsz, d), lambda l, j: (0, 0)),
                  pl.BlockSpec((1, d, tn), lambda l, j: (l, 0, j)),
                  pl.BlockSpec((1, 1, tn), lambda l, j: (l, 0, j))],
        out_specs=pl.BlockSpec((1, bsz, tn), lambda l, j: (l, 0, j)),
        compiler_params=_params(2),
        name="ada",
    )(c, w_ada, b_ada.reshape(depth, 1, n))


def _proj_kernel(x_ref, ada_ref, wk_ref, wm_ref, wt_ref, wmt_ref,
                 k_ref, kmean_ref, misc_ref, qT_ref, vT_ref, qiT_ref, miscT_ref, *, tm):
    x = x_ref[0]
    sh = ada_ref[0, 0:1, :]
    sc = ada_ref[0, 1:2, :]
    h = (x * (1.0 + sc) + sh).astype(BF16)
    kf = _dot(h, wk_ref[...])
    k_ref[0] = kf.astype(BF16)
    for g in range(tm // MOBA_BLOCK):
        kmean_ref[0, 0, g:g + 1, :] = jnp.mean(kf[g * MOBA_BLOCK:(g + 1) * MOBA_BLOCK], axis=0, keepdims=True)
    misc_ref[0] = _dot(h, wm_ref[...])
    t = lax.dot_general(wt_ref[...], h, NT, preferred_element_type=F32)
    nq = N_MIX * MIX_W
    qT_ref[0] = (t[0:nq] * QSCALE).astype(BF16)
    for g in range(tm // CH):
        vT_ref[0, g] = t[nq:2 * nq, g * CH:(g + 1) * CH].astype(BF16)
    qiT_ref[0] = t[2 * nq:].astype(BF16)
    miscT_ref[0] = lax.dot_general(wmt_ref[...], h, NT, preferred_element_type=F32)


def _proj(x, ada, w_in, tm=512):
    bsz, seq, d = x.shape
    nq = N_MIX * MIX_W
    nqi = IDX_HEADS * IDX_DIM
    w = w_in.astype(BF16)
    wk = w[:, nq:2 * nq]
    wm = jnp.pad(w[:, 3 * nq + nqi:], ((0, 0), (0, MISC_W - (IN_COLS - 3 * nq - nqi))))
    wt = jnp.concatenate([w[:, 0:nq], w[:, 2 * nq:3 * nq], w[:, 3 * nq:3 * nq + nqi]], axis=1).T
    wmt = wm.T
    nt = seq // tm
    full = lambda b, i: (0, 0)
    outs = pl.pallas_call(
        functools.partial(_proj_kernel, tm=tm),
        out_shape=(jax.ShapeDtypeStruct((bsz, seq, nq), BF16),
                   jax.ShapeDtypeStruct((bsz, nt, tm // MOBA_BLOCK, nq), F32),
                   jax.ShapeDtypeStruct((bsz, seq, MISC_W), F32),
                   jax.ShapeDtypeStruct((bsz, nq, seq), BF16),
                   jax.ShapeDtypeStruct((bsz, seq // CH, nq, CH), BF16),
                   jax.ShapeDtypeStruct((bsz, nqi, seq), BF16),
                   jax.ShapeDtypeStruct((bsz, MISC_W, seq), F32)),
        grid=(bsz, nt),
        in_specs=[pl.BlockSpec((1, tm, d), lambda b, i: (b, i, 0)),
                  pl.BlockSpec((1, 6, d), lambda b, i: (b, 0, 0)),
                  pl.BlockSpec(wk.shape, full),
                  pl.BlockSpec(wm.shape, full),
                  pl.BlockSpec(wt.shape, full),
                  pl.BlockSpec(wmt.shape, full)],
        out_specs=(pl.BlockSpec((1, tm, nq), lambda b, i: (b, i, 0)),
                   pl.BlockSpec((1, 1, tm // MOBA_BLOCK, nq), lambda b, i: (b, i, 0, 0)),
                   pl.BlockSpec((1, tm, MISC_W), lambda b, i: (b, i, 0)),
                   pl.BlockSpec((1, nq, tm), lambda b, i: (b, 0, i)),
                   pl.BlockSpec((1, tm // CH, nq, CH), lambda b, i: (b, i, 0, 0)),
                   pl.BlockSpec((1, nqi, tm), lambda b, i: (b, 0, i)),
                   pl.BlockSpec((1, MISC_W, tm), lambda b, i: (b, 0, i))),
        compiler_params=_params(2),
        name="proj",
    )(x, ada, wk, wm, wt, wmt)
    k, kmean, misc, qT, vT, qiT, miscT = outs
    return k, kmean.reshape(bsz, seq // MOBA_BLOCK, nq), misc, qT, vT, qiT, miscT


def _mask_heads(qT_ref, qm_ref):
    q = qT_ref[0]
    rowh = lax.broadcasted_iota(jnp.int32, q.shape, 0) // HEAD_DIM
    for h in range(HEADS):
        qm_ref[h] = jnp.where(rowh == h, q, jnp.zeros_like(q))


def _init_state(m_ref, acc_ref):
    m_ref[...] = jnp.full(m_ref.shape, NEG, F32)
    acc_ref[...] = jnp.zeros(acc_ref.shape, F32)


def _attention(i, first, k_ref, vT_ref, st, logits_fn, chunk_ctx=None, n_diag=None):
    qm_ref, m_ref, acc_ref, s_ref, mx_ref, p_ref, alpha_ref = st
    if n_diag is None:
        n_chunks = i - first + 1
    else:
        n_chunks = n_diag + (n_diag * i - first)
    ones = jnp.ones((ACC_ROWS - HEAD_DIM, CH), BF16)

    def chunk_of(n):
        n = jnp.clip(n, 0, n_chunks - 1)
        if n_diag is None:
            return first + n
        return jnp.where(n < n_diag, n_diag * i + n, first + n - n_diag)

    def position(n):
        if n_diag is None:
            return None
        return n if n < n_diag else -1

    def logits_head(h, slot, c, kc, ctx, diag, pen):
        x = logits_fn(h, c, _dot(kc, qm_ref[h]), ctx, diag, pen)
        s_ref[slot, h] = x
        mx_ref[slot, h] = jnp.max(x, axis=0, keepdims=True)

    def softmax_head(h, slot):
        m_old = m_ref[h]
        m_new = jnp.maximum(m_old, mx_ref[slot, h])
        alpha_ref[slot, h] = jnp.exp2(m_old - m_new)
        p_ref[slot, h] = jnp.exp2(s_ref[slot, h] - m_new).astype(BF16)
        m_ref[h] = m_new

    def pv_head(h, slot, vc):
        v1 = jnp.concatenate([vc[h * HEAD_DIM:(h + 1) * HEAD_DIM, :], ones], axis=0)
        acc_ref[h] = alpha_ref[slot, h] * acc_ref[h] + _dot(v1, p_ref[slot, h])

    def step(n, a, diag):
        c = chunk_of(n + 1)
        pen = jnp.where(n + 1 < n_chunks, 0.0, NEG)
        kc = k_ref[0, pl.ds(pl.multiple_of(c * CH, CH), CH), :]
        vc = vT_ref[0, chunk_of(n - 1)]
        ctx = chunk_ctx(c) if chunk_ctx is not None else None
        for h in range(HEADS):
            softmax_head(h, a)
        for h in range(HEADS):
            pv_head(h, 1 - a, vc)
        for h in range(HEADS):
            logits_head(h, 1 - a, c, kc, ctx, diag, pen)

    p_ref[1] = jnp.zeros(p_ref.shape[1:], BF16)
    alpha_ref[1] = jnp.ones(alpha_ref.shape[1:], F32)
    c0 = chunk_of(jnp.int32(0))
    k0 = k_ref[0, pl.ds(pl.multiple_of(c0 * CH, CH), CH), :]
    ctx0 = chunk_ctx(c0) if chunk_ctx is not None else None
    for h in range(HEADS):
        logits_head(h, 0, c0, k0, ctx0, position(0), 0.0)

    first_trip = 0
    if n_diag is not None and n_diag > 1:
        step(0, 0, position(1))
        step(1, 1, position(2))
        first_trip = 1
    later = position(2 * first_trip + 1)

    def pair(t, carry):
        step(2 * t, 0, later)
        step(2 * t + 1, 1, later)
        return carry

    n_trips = (n_chunks + 1) // 2
    lax.fori_loop(first_trip, n_trips, pair, 0)
    v_last = vT_ref[0, chunk_of(2 * n_trips - 1)]
    for h in range(HEADS):
        pv_head(h, 1, v_last)


def _finish(o_ref, acc_ref):
    parts = [acc_ref[h, 0:HEAD_DIM, :] / acc_ref[h, HEAD_DIM:HEAD_DIM + 1, :] for h in range(HEADS)]
    o_ref[0] = jnp.concatenate(parts, axis=0).T.astype(BF16)


def _lanes(tile, tq):
    return tile if tq == LANES else jnp.concatenate([tile] * (tq // LANES), axis=1)


def _causal_neg(tq, d=0):
    s_i = lax.broadcasted_iota(jnp.int32, (CH, tq), 0) + d * CH
    t_i = lax.broadcasted_iota(jnp.int32, (CH, tq), 1)
    return jnp.where(s_i <= t_i, 0.0, NEG).astype(F32)


def _key_pos_bias(bias_ref, slopes):
    s_i = lax.broadcasted_iota(jnp.int32, (CH, LANES), 0).astype(F32)
    for h in range(HEADS):
        bias_ref[h] = s_i * float(slopes[h] * LOG2E)


def _attn_specs(mixer, seq, tq=CH):
    return [pl.BlockSpec((1, MIX_W, tq), lambda b, i: (b, mixer, i)),
            pl.BlockSpec((1, seq, MIX_W), lambda b, i: (b, 0, mixer)),
            pl.BlockSpec((1, seq // CH, MIX_W, CH), lambda b, i: (b, 0, mixer, 0))]


def _attn_scratch(tq=CH):
    return [pltpu.VMEM((HEADS, MIX_W, tq), BF16),
            pltpu.VMEM((HEADS, 1, tq), F32),
            pltpu.VMEM((HEADS, ACC_ROWS, tq), F32),
            pltpu.VMEM((2, HEADS, CH, tq), F32),
            pltpu.VMEM((2, HEADS, 1, tq), F32),
            pltpu.VMEM((2, HEADS, CH, tq), BF16),
            pltpu.VMEM((2, HEADS, 1, tq), F32)]


def _fox_kernel(qT_ref, k_ref, vT_ref, misc_ref, bf_ref, o_ref,
                cum_ref, *st, n_chunks):
    qm_ref, m_ref, acc_ref = st[:3]
    i = pl.program_id(1)

    @pl.when(i == 0)
    def _cumulative_gates():
        rr = lax.broadcasted_iota(jnp.int32, (CH, CH), 0)
        cc = lax.broadcasted_iota(jnp.int32, (CH, CH), 1)
        tri = jnp.where(cc <= rr, 1.0, 0.0).astype(BF16)

        def body(blk, carry):
            off = pl.multiple_of(blk * CH, CH)
            z = misc_ref[0, pl.ds(off, CH), :] + bf_ref[...]
            ls = jnp.minimum(z, 0.0) - jnp.log1p(jnp.exp(-jnp.abs(z)))
            csum = sum(_dot(tri, part) for part in _split3(ls))
            new = []
            for h in range(HEADS):
                col = csum[:, FL_ROW + h:FL_ROW + h + 1]
                cum = jnp.broadcast_to(col, (CH, LANES)) + carry[h]
                cum_ref[h, pl.ds(off, CH), :] = cum * LOG2E
                new.append(cum[CH - 1:CH, :])
            return tuple(new)

        lax.fori_loop(0, n_chunks, body, tuple(jnp.zeros((1, LANES), F32) for _ in range(HEADS)))

    _mask_heads(qT_ref, qm_ref)
    _init_state(m_ref, acc_ref)

    def logits(h, c, qk, _, diag, pen):
        off = pl.multiple_of(c * CH, CH)
        cum = cum_ref[h, pl.ds(off, CH), :]
        if diag >= 0:
            return qk - _lanes(cum, TQ_WIDE) + _causal_neg(TQ_WIDE, diag)
        return qk - _lanes(cum - pen, TQ_WIDE)

    _attention(i, 0, k_ref, vT_ref, st, logits, n_diag=TQ_WIDE // CH)
    _finish(o_ref, acc_ref)


def _fox(qT, k, vT, misc, b_forget_l):
    bsz, seq, _ = k.shape
    n_chunks = seq // CH
    bf = jnp.zeros((1, MISC_W), F32).at[0, FL_ROW:FL_ROW + HEADS].set(b_forget_l)
    return pl.pallas_call(
        functools.partial(_fox_kernel, n_chunks=n_chunks),
        out_shape=jax.ShapeDtypeStruct((bsz, seq, MIX_W), BF16),
        grid=(bsz, seq // TQ_WIDE),
        in_specs=_attn_specs(3, seq, TQ_WIDE) + [
            pl.BlockSpec((1, seq, MISC_W), lambda b, i: (b, 0, 0)),
            pl.BlockSpec((1, MISC_W), lambda b, i: (0, 0))],
        out_specs=pl.BlockSpec((1, TQ_WIDE, MIX_W), lambda b, i: (b, i, 0)),
        scratch_shapes=[pltpu.VMEM((HEADS, seq, LANES), F32)] + _attn_scratch(TQ_WIDE),
        compiler_params=_params(2),
        name="fox",
    )(qT, k, vT, misc, bf)


def _moba_kernel(qT_ref, k_ref, vT_ref, kmean_ref, o_ref,
                 bias_ref, rowadd_ref, *st, slopes, n_blocks):
    qm_ref, m_ref, acc_ref = st[:3]
    i = pl.program_id(1)
    tq = TQ_WIDE
    n_diag = tq // CH
    _mask_heads(qT_ref, qm_ref)
    _init_state(m_ref, acc_ref)

    @pl.when(jnp.logical_and(pl.program_id(0) == 0, i == 0))
    def _bias_table():
        s_f = lax.broadcasted_iota(jnp.int32, (CH, tq), 0).astype(F32)
        for h in range(HEADS):
            pos = s_f * float(slopes[h] * LOG2E)
            bias_ref[0, h] = pos
            for d in range(n_diag):
                bias_ref[1 + d, h] = pos + _causal_neg(tq, d)

    n_i = lax.broadcasted_iota(jnp.int32, (n_blocks, tq), 0)
    n_f = n_i.astype(F32)
    own = n_diag * i + lax.broadcasted_iota(jnp.int32, (n_blocks, tq), 1) // MOBA_BLOCK
    past = n_i < own
    kmean_parts = _split3(kmean_ref[0])
    for h in range(HEADS):
        gate = sum(_dot(part, qm_ref[h]) for part in kmean_parts)
        gate = jnp.where(past, gate, -jnp.inf)
        chosen = jnp.zeros((n_blocks, tq), F32)
        for _ in range(MOBA_TOPK):
            mx = jnp.max(gate, axis=0, keepdims=True)
            first = jnp.min(jnp.where(gate == mx, n_f, float(n_blocks)), axis=0, keepdims=True)
            pick = n_f == first
            chosen = jnp.where(pick, 1.0, chosen)
            gate = jnp.where(pick, -jnp.inf, gate)
        chosen = jnp.where(past, chosen, 0.0)
        blk_shift = n_f * float(slopes[h] * LOG2E * MOBA_BLOCK)
        rowadd_ref[h] = jnp.where(jnp.logical_or(chosen > 0.5, n_i == own), blk_shift, NEG)

    def logits(h, c, qk, _, diag, pen):
        row = rowadd_ref[h, pl.ds(c, 1), :]
        if diag >= 0:
            return qk + bias_ref[1 + diag, h] + row
        return qk + bias_ref[0, h] + (row + pen)

    _attention(i, 0, k_ref, vT_ref, st, logits, n_diag=n_diag)
    _finish(o_ref, acc_ref)


def _moba(qT, k, vT, kmean, slopes):
    bsz, seq, _ = k.shape
    n_blocks = seq // MOBA_BLOCK
    return pl.pallas_call(
        functools.partial(_moba_kernel, slopes=tuple(float(s) for s in slopes), n_blocks=n_blocks),
        out_shape=jax.ShapeDtypeStruct((bsz, seq, MIX_W), BF16),
        grid=(bsz, seq // TQ_WIDE),
        in_specs=_attn_specs(2, seq, TQ_WIDE) + [
            pl.BlockSpec((1, n_blocks, MIX_W), lambda b, i: (b, 0, 2))],
        out_specs=pl.BlockSpec((1, TQ_WIDE, MIX_W), lambda b, i: (b, i, 0)),
        scratch_shapes=[pltpu.VMEM((1 + TQ_WIDE // CH, HEADS, CH, TQ_WIDE), F32),
                        pltpu.VMEM((HEADS, n_blocks, TQ_WIDE), F32)] + _attn_scratch(TQ_WIDE),
        compiler_params=_params(2),
        name="moba",
    )(qT, k, vT, kmean)


DIL_SPAN = 2048 // CH + 1


def _dilated_kernel(qT_ref, k_ref, vT_ref, o_ref, table_ref, *st, slopes):
    qm_ref, m_ref, acc_ref = st[:3]
    b = pl.program_id(0)
    i = pl.program_id(1)

    tq = TQ_WIDE
    n_diag = tq // CH
    n_entries = n_diag - 1 + DIL_SPAN

    @pl.when(jnp.logical_and(b == 0, i == 0))
    def _bias_table():
        s_i = lax.broadcasted_iota(jnp.int32, (CH, tq), 0)
        t_i = lax.broadcasted_iota(jnp.int32, (CH, tq), 1)
        for e in range(n_entries):
            j = e - (n_diag - 1)
            d = t_i - s_i + j * CH
            ok = d >= 0
            mult = (jnp.where(jnp.logical_and(ok, d <= 128), 1.0, 0.0)
                    + jnp.where(jnp.logical_and(ok, jnp.logical_and(d <= 512, (d & 3) == 0)), 1.0, 0.0)
                    + jnp.where(jnp.logical_and(ok, jnp.logical_and(d <= 2048, (d & 15) == 0)), 1.0, 0.0))
            logm = jnp.where(mult > 0.5, jnp.log2(jnp.maximum(mult, 1.0)), NEG)
            df = d.astype(F32)
            for h in range(HEADS):
                table_ref[h, e] = logm - df * float(slopes[h] * LOG2E)
        for h in range(HEADS):
            table_ref[h, n_entries] = jnp.full((CH, tq), NEG, F32)

    _mask_heads(qT_ref, qm_ref)
    _init_state(m_ref, acc_ref)

    def logits(h, c, qk, _, diag, pen):
        if diag >= 0:
            return qk + table_ref[h, n_diag - 1 - diag]
        return qk + table_ref[h, jnp.where(pen < 0.0, n_entries, n_diag * i - c + (n_diag - 1))]

    _attention(i, jnp.maximum(n_diag * i - (DIL_SPAN - 1), 0), k_ref, vT_ref, st, logits, n_diag=n_diag)
    _finish(o_ref, acc_ref)


def _dilated(qT, k, vT, slopes):
    bsz, seq, _ = k.shape
    return pl.pallas_call(
        functools.partial(_dilated_kernel, slopes=tuple(float(s) for s in slopes)),
        out_shape=jax.ShapeDtypeStruct((bsz, seq, MIX_W), BF16),
        grid=(bsz, seq // TQ_WIDE),
        in_specs=_attn_specs(1, seq, TQ_WIDE),
        out_specs=pl.BlockSpec((1, TQ_WIDE, MIX_W), lambda b, i: (b, i, 0)),
        scratch_shapes=[pltpu.VMEM((HEADS, TQ_WIDE // CH + DIL_SPAN, CH, TQ_WIDE), F32)]
        + _attn_scratch(TQ_WIDE),
        compiler_params=_params(2),
        name="dilated",
    )(qT, k, vT)


def _dsa_kernel(qT_ref, k_ref, vT_ref, qiT_ref, misc_ref, miscT_ref, o_ref,
                bias_ref, qi_ref, key_ref, planes_ref, alive_ref, seen_ref, *st, slopes):
    qm_ref, m_ref, acc_ref = st[:3]
    i = pl.program_id(1)
    tq = CH

    @pl.when(jnp.logical_and(pl.program_id(0) == 0, i == 0))
    def _clear_planes():
        planes_ref[0:32] = jnp.zeros((32,) + planes_ref.shape[1:], jnp.int32)
        planes_ref[32] = jnp.full(planes_ref.shape[1:], -1, jnp.int32)
    _mask_heads(qT_ref, qm_ref)
    _init_state(m_ref, acc_ref)
    _key_pos_bias(bias_ref, slopes)

    qi_all = qiT_ref[0]
    zpad = jnp.zeros((MISC_W - IDX_DIM, tq), BF16)
    for h in range(IDX_HEADS):
        qi_ref[h] = jnp.concatenate([qi_all[h * IDX_DIM:(h + 1) * IDX_DIM], zpad], axis=0)
    w_rows = miscT_ref[0, WI_ROW:WI_ROW + IDX_HEADS, :] * float(IDX_HEADS ** -0.5 * IDX_DIM ** -0.5)

    s_i = lax.broadcasted_iota(jnp.int32, (CH, tq), 0)
    t_i = lax.broadcasted_iota(jnp.int32, (CH, tq), 1)

    def score_chunk(c, diag):
        off = pl.multiple_of(c * CH, CH)
        ki = misc_ref[0, pl.ds(off, CH), :].astype(BF16)
        score = jnp.zeros((CH, tq), F32)
        for h in range(IDX_HEADS):
            rel = jnp.maximum(_dot(ki, qi_ref[h]), 0.0)
            score = score + rel * w_rows[h:h + 1, :]
        bits = lax.bitcast_convert_type(score, jnp.int32)
        key = bits ^ ((bits >> 31) & 0x7FFFFFFF)
        key = jnp.where(key == -1, 0, key)
        if diag:
            key = jnp.where(s_i <= t_i, key, INT_MIN)
        key_ref[pl.ds(off, CH), :] = key
        ukey = key ^ INT_MIN
        words = _bit_transpose32([ukey[8 * j:8 * j + 8, :] for j in range(32)])
        row0 = pl.multiple_of(c * 8, 8)
        for b in range(32):
            planes_ref[b, pl.ds(row0, 8), :] = words[31 - b]

    score_chunk(i, True)

    def score_pair(t, carry):
        score_chunk(2 * t, False)
        score_chunk(jnp.minimum(2 * t + 1, i - 1), False)
        return carry

    lax.fori_loop(0, (i + 1) // 2, score_pair, 0)
    group = 4 * 8
    n_groups = i // 4 + 1
    alive_ref[...] = jnp.where(lax.broadcasted_iota(jnp.int32, alive_ref.shape, 0) < (i + 1) * 8, -1, 0)

    def narrow(alive, rows, b_prev, took_prev):
        hit = alive & planes_ref[b_prev, rows, :]
        return jnp.where(took_prev, hit, alive ^ hit)

    def bit_step(n, state):
        thr, need, took_prev = state
        b = 31 - n
        took_prev = took_prev != 0

        def sweep(g, cnt):
            rows = pl.ds(pl.multiple_of(g * group, group), group)
            alive = narrow(alive_ref[rows, :], rows, b + 1, took_prev)
            alive_ref[rows, :] = alive
            return cnt + lax.population_count(alive & planes_ref[b, rows, :])

        cnt = lax.fori_loop(0, n_groups, sweep, jnp.zeros((group, tq), jnp.int32))
        cnt = jnp.sum(cnt, axis=0, keepdims=True)
        ok = cnt >= need
        return (jnp.where(ok, thr | jnp.left_shift(jnp.int32(1), b), thr), jnp.where(ok, need, need - cnt),
                jnp.where(ok, 1, 0))

    uthr, n_take, took_last = lax.fori_loop(
        0, 32, bit_step, (jnp.zeros((1, tq), jnp.int32), jnp.full((1, tq), TOPK_KEYS, jnp.int32),
                          jnp.ones((1, tq), jnp.int32)))
    thr = uthr ^ INT_MIN

    def count_alive(g, cnt):
        rows = pl.ds(pl.multiple_of(g * group, group), group)
        return cnt + lax.population_count(narrow(alive_ref[rows, :], rows, 0, took_last != 0))

    n_eq = jnp.sum(lax.fori_loop(0, n_groups, count_alive, jnp.zeros((group, tq), jnp.int32)), axis=0, keepdims=True)
    tied = jnp.logical_and(n_eq > n_take, thr != INT_MIN)
    any_tied = jnp.max(jnp.where(tied, 1.0, 0.0)) > 0.5

    def logits(h, c, qk, neg, diag, pen):
        pos = bias_ref[h] + (c.astype(F32) * float(slopes[h] * LOG2E * CH) + pen)
        return qk + _lanes(pos, tq) + neg

    @pl.when(jnp.logical_not(any_tied))
    def _no_ties():
        thr_lo = jnp.where(thr == INT_MIN, INT_MIN + 1, thr)

        def selection(c):
            off = pl.multiple_of(c * CH, CH)
            return jnp.where(key_ref[pl.ds(off, CH), :] >= thr_lo, 0.0, NEG)

        _attention(i, 0, k_ref, vT_ref, st, logits, selection)

    @pl.when(any_tied)
    def _ties():
        n_first = jnp.where(thr == INT_MIN, 0, n_take).astype(F32)
        rr = lax.broadcasted_iota(jnp.int32, (CH, CH), 0)
        cc = lax.broadcasted_iota(jnp.int32, (CH, CH), 1)
        tri = jnp.where(cc <= rr, 1.0, 0.0).astype(BF16)
        seen_ref[...] = jnp.zeros(seen_ref.shape, F32)

        def selection(c):
            off = pl.multiple_of(c * CH, CH)
            key = key_ref[pl.ds(off, CH), :]
            eq = key == thr
            rank = _dot(tri, jnp.where(eq, 1.0, 0.0).astype(BF16)) + seen_ref[...]
            seen_ref[...] = rank[CH - 1:CH, :]
            take = jnp.logical_or(key > thr, jnp.logical_and(eq, rank <= n_first))
            return jnp.where(take, 0.0, NEG)

        _attention(i, 0, k_ref, vT_ref, st, logits, selection)

    _finish(o_ref, acc_ref)


def _dsa(qT, k, vT, qiT, misc, miscT, slopes):
    bsz, seq, _ = k.shape
    nqi = IDX_HEADS * IDX_DIM
    plane_rows = -(-(seq // CH) // 4) * 32
    return pl.pallas_call(
        functools.partial(_dsa_kernel, slopes=tuple(float(s) for s in slopes)),
        out_shape=jax.ShapeDtypeStruct((bsz, seq, MIX_W), BF16),
        grid=(bsz, seq // CH),
        in_specs=_attn_specs(0, seq) + [
            pl.BlockSpec((1, nqi, CH), lambda b, i: (b, 0, i)),
            pl.BlockSpec((1, seq, MISC_W), lambda b, i: (b, 0, 0)),
            pl.BlockSpec((1, MISC_W, CH), lambda b, i: (b, 0, i))],
        out_specs=pl.BlockSpec((1, CH, MIX_W), lambda b, i: (b, i, 0)),
        scratch_shapes=[pltpu.VMEM((HEADS, CH, LANES), F32),
                        pltpu.VMEM((IDX_HEADS, MISC_W, CH), BF16),
                        pltpu.VMEM((seq, CH), jnp.int32),
                        pltpu.VMEM((33, plane_rows, CH), jnp.int32),
                        pltpu.VMEM((plane_rows, CH), jnp.int32),
                        pltpu.VMEM((1, CH), F32)] + _attn_scratch(),
        compiler_params=_params(2),
        name="dsa",
    )(qT, k, vT, qiT, misc, miscT)


def _merge_kernel(x_ref, ada_ref, oa_ref, ob_ref, oc_ref, od_ref, wmg_ref, wbr_ref, wout_ref,
                  g_ref, b_ref, out_ref):
    x = x_ref[0]
    sh = ada_ref[0, 0:1, :]
    sc = ada_ref[0, 1:2, :]
    g1 = ada_ref[0, 2:3, :]
    h = (x * (1.0 + sc) + sh).astype(BF16)
    mixed = None
    for m, o_ref in enumerate((oa_ref, ob_ref, oc_ref, od_ref)):
        gate = jax.nn.sigmoid(_dot(h, wmg_ref[m]))
        term = gate * _dot(o_ref[0], wbr_ref[m])
        mixed = term if mixed is None else mixed + term
    y = _dot(mixed.astype(BF16), wout_ref[...])
    out_ref[0] = _layer_norm(DN_ALPHA * x + g1 * y, g_ref[...], b_ref[...])


def _merge(x, ada, outs, w_mg, w_br, w_out, ln_g, ln_b, tm=512):
    bsz, seq, d = x.shape
    o_spec = pl.BlockSpec((1, tm, MIX_W), lambda b, i: (b, i, 0))
    return pl.pallas_call(
        _merge_kernel,
        out_shape=jax.ShapeDtypeStruct((bsz, seq, d), F32),
        grid=(bsz, seq // tm),
        in_specs=[pl.BlockSpec((1, tm, d), lambda b, i: (b, i, 0)),
                  pl.BlockSpec((1, 6, d), lambda b, i: (b, 0, 0)),
                  o_spec, o_spec, o_spec, o_spec,
                  pl.BlockSpec((N_MIX, d, d), lambda b, i: (0, 0, 0)),
                  pl.BlockSpec((N_MIX, MIX_W, d), lambda b, i: (0, 0, 0)),
                  pl.BlockSpec((d, d), lambda b, i: (0, 0)),
                  pl.BlockSpec((1, d), lambda b, i: (0, 0)),
                  pl.BlockSpec((1, d), lambda b, i: (0, 0))],
        out_specs=pl.BlockSpec((1, tm, d), lambda b, i: (b, i, 0)),
        compiler_params=_params(2),
        name="merge",
    )(x, ada, *outs, w_mg.astype(BF16), w_br.astype(BF16), w_out.astype(BF16),
      ln_g.reshape(1, d), ln_b.reshape(1, d))


def _route(logits_t, bias_col):
    per_group = N_EXPERTS // N_GROUPS
    scores = jax.nn.sigmoid(logits_t)
    biased = scores + bias_col
    s_rows = [scores[e:e + 1] for e in range(N_EXPERTS)]
    b_rows = [biased[e:e + 1] for e in range(N_EXPERTS)]
    best_g = None
    for g in range(N_GROUPS):
        r = b_rows[g * per_group:(g + 1) * per_group]
        gs = None
        for a in range(per_group):
            for b in range(a + 1, per_group):
                pair = r[a] + r[b]
                gs = pair if gs is None else jnp.maximum(gs, pair)
        if best_g is None:
            best_g, best_v = jnp.zeros_like(gs, dtype=jnp.int32), gs
        else:
            better = gs > best_v
            best_g = jnp.where(better, g, best_g)
            best_v = jnp.maximum(best_v, gs)
    masked = [jnp.where(best_g == e // per_group, b_rows[e], -jnp.inf) for e in range(N_EXPERTS)]

    def argmax_first(rows):
        idx, val = jnp.zeros_like(best_g), rows[0]
        for e in range(1, N_EXPERTS):
            better = rows[e] > val
            idx = jnp.where(better, e, idx)
            val = jnp.maximum(val, rows[e])
        return idx

    e1 = argmax_first(masked)
    e2 = argmax_first([jnp.where(e1 == e, -jnp.inf, masked[e]) for e in range(N_EXPERTS)])
    s1 = sum(jnp.where(e1 == e, s_rows[e], 0.0) for e in range(N_EXPERTS))
    s2 = sum(jnp.where(e2 == e, s_rows[e], 0.0) for e in range(N_EXPERTS))
    tot = s1 + s2
    w1, w2 = s1 / tot, s2 / tot
    rows = [jnp.where(e1 == e, w1, 0.0) + jnp.where(e2 == e, w2, 0.0) for e in range(N_EXPERTS)]
    return jnp.concatenate(rows, axis=0), best_g


def _moe_kernel(x_ref, ada_ref, wr_ref, br_ref, wg_ref, wu_ref, wd_ref, g_ref, b_ref, out_ref,
                perm_ref, xs_ref, combs_ref, ys_ref, sub_ref, *, tm, rows):
    g = pl.program_id(1)
    per_group = N_EXPERTS // N_GROUPS

    @pl.when(g == 0)
    def _route_and_sort():
        x = x_ref[...]
        h2 = x * (1.0 + ada_ref[0, 4:5, :]) + ada_ref[0, 3:4, :]
        h_hi = h2.astype(BF16)
        h_lo = (h2 - h_hi.astype(F32)).astype(BF16)
        first = _dot(h_hi, wr_ref[...])
        logits = first[:, 0:LANES] + first[:, LANES:] + _dot(h_lo, wr_ref[:, 0:LANES])
        logits_t = logits.T[0:N_EXPERTS]
        comb_t, best_g = _route(logits_t, br_ref[...])
        member = [jnp.where(best_g == g, 1.0, 0.0) for g in range(N_GROUPS)]
        grp = jnp.concatenate(member + [jnp.zeros((8 - N_GROUPS, tm), F32)], axis=0).astype(BF16)
        s_i = lax.broadcasted_iota(jnp.int32, (tm, tm), 0)
        t_i = lax.broadcasted_iota(jnp.int32, (tm, tm), 1)
        rank = _dot(grp, jnp.where(s_i <= t_i, 1.0, 0.0).astype(BF16))
        pos = jnp.zeros((1, tm), F32)
        start = jnp.zeros((1, 1), F32)
        for g in range(N_GROUPS):
            cap = jnp.ceil(rank[g:g + 1, tm - 1:tm] * (1.0 / MOE_SUB)) * MOE_SUB
            pos = pos + member[g] * (start + rank[g:g + 1] - 1.0)
            sub_ref[2 * g] = (jnp.sum(start) * (1.0 / MOE_SUB)).astype(jnp.int32)
            sub_ref[2 * g + 1] = (jnp.sum(cap) * (1.0 / MOE_SUB)).astype(jnp.int32)
            start = start + cap
        r_f = lax.broadcasted_iota(jnp.int32, (rows, tm), 0).astype(F32)
        perm = jnp.where(r_f == pos, 1.0, 0.0).astype(BF16)
        perm_ref[...] = perm
        comb = jnp.concatenate([comb_t, jnp.zeros((LANES - N_EXPERTS, tm), F32)], axis=0).T
        c_hi = comb.astype(BF16)
        c_lo = (comb - c_hi.astype(F32)).astype(BF16)
        d = h_hi.shape[1]
        moved = _dot(perm, jnp.concatenate([h_hi, c_hi, c_lo], axis=1))
        xs_ref[...] = moved[:, 0:d].astype(BF16)
        combs_ref[...] = moved[:, d:d + LANES] + moved[:, d + LANES:]
        ys_ref[...] = jnp.zeros(ys_ref.shape, BF16)

    lane = lax.broadcasted_iota(jnp.int32, (MOE_SUB, LANES), 1)

    def sub_tile(j, carry):
        r0 = pl.multiple_of((sub_ref[2 * g] + j) * MOE_SUB, MOE_SUB)
        xj = xs_ref[pl.ds(r0, MOE_SUB), :]
        cw = combs_ref[pl.ds(r0, MOE_SUB), :]
        hids = []
        for q in range(per_group):
            w_e = jnp.sum(jnp.where(lane == g * per_group + q, cw, 0.0), axis=1, keepdims=True)
            hids.append((jax.nn.silu(_dot(xj, wg_ref[0, q])) * _dot(xj, wu_ref[0, q]) * w_e).astype(BF16))
        ys_ref[pl.ds(r0, MOE_SUB), :] = _dot(jnp.concatenate(hids, axis=1), wd_ref[0, 0]).astype(BF16)
        return carry

    lax.fori_loop(0, sub_ref[2 * g + 1], sub_tile, 0)

    @pl.when(g == N_GROUPS - 1)
    def _unsort_and_norm():
        y = lax.dot_general(perm_ref[...], ys_ref[...], (((0,), (0,)), ((), ())),
                            preferred_element_type=F32)
        z = DN_ALPHA * x_ref[...] + ada_ref[0, 5:6, :] * y
        out_ref[...] = _layer_norm(z, g_ref[...], b_ref[...])


def _moe(x, ada_rows, w_router, b_router, w_gate, w_up, w_down, ln_g, ln_b, batch_len, tm=1024):
    n, d = x.shape
    per_b = batch_len // tm
    per_group = N_EXPERTS // N_GROUPS
    rows = tm + N_GROUPS * MOE_SUB
    wr32 = jnp.pad(w_router, ((0, 0), (0, LANES - N_EXPERTS)))
    wr_hi = wr32.astype(BF16)
    wr = jnp.concatenate([wr_hi, (wr32 - wr_hi.astype(F32)).astype(BF16)], axis=1)
    br = b_router.reshape(N_EXPERTS, 1)
    return pl.pallas_call(
        functools.partial(_moe_kernel, tm=tm, rows=rows),
        out_shape=jax.ShapeDtypeStruct((n, d), F32),
        grid=(n // tm, N_GROUPS),
        in_specs=[pl.BlockSpec((tm, d), lambda i, g: (i, 0)),
                  pl.BlockSpec((1, 6, d), lambda i, g: (i // per_b, 0, 0)),
                  pl.BlockSpec((d, 2 * LANES), lambda i, g: (0, 0)),
                  pl.BlockSpec((N_EXPERTS, 1), lambda i, g: (0, 0)),
                  pl.BlockSpec((1, per_group, d, D_EXPERT), lambda i, g: (0, g, 0, 0)),
                  pl.BlockSpec((1, per_group, d, D_EXPERT), lambda i, g: (0, g, 0, 0)),
                  pl.BlockSpec((1, 1, per_group * D_EXPERT, d), lambda i, g: (0, g, 0, 0)),
                  pl.BlockSpec((1, d), lambda i, g: (0, 0)),
                  pl.BlockSpec((1, d), lambda i, g: (0, 0))],
        out_specs=pl.BlockSpec((tm, d), lambda i, g: (i, 0)),
        scratch_shapes=[pltpu.VMEM((rows, tm), BF16),
                        pltpu.VMEM((rows, d), BF16),
                        pltpu.VMEM((rows, LANES), F32),
                        pltpu.VMEM((rows, d), BF16),
                        pltpu.SMEM((2 * N_GROUPS,), jnp.int32)],
        compiler_params=_params(2),
        name="moe",
    )(x, ada_rows, wr, br, w_gate[None].astype(BF16), w_up[None].astype(BF16),
      w_down.reshape(1, N_GROUPS, per_group * D_EXPERT, d).astype(BF16),
      ln_g.reshape(1, d), ln_b.reshape(1, d))


def kernel(x, c, w_in, b_forget, w_branch, w_merge_gate, w_out, w_ada, b_ada, ln1_g, ln1_b, ln2_g, ln2_b,
           w_router, b_router, w_exp_gate, w_exp_up, w_exp_down):
    bsz, seq, d = x.shape
    slopes = _alibi_slopes()
    ada_all = _ada(c, w_ada, b_ada).reshape(DEPTH, bsz, 6, d)
    for l in range(DEPTH):
        ada = ada_all[l]
        k, kmean, misc, qT, vT, qiT, miscT = _proj(x, ada, w_in[l])
        o_a = _dsa(qT, k, vT, qiT, misc, miscT, slopes[0])
        o_b = _dilated(qT, k, vT, slopes[1])
        o_c = _moba(qT, k, vT, kmean, slopes[2])
        o_d = _fox(qT, k, vT, misc, b_forget[l])
        x = _merge(x, ada, (o_a, o_b, o_c, o_d), w_merge_gate[l], w_branch[l], w_out[l], ln1_g[l], ln1_b[l])
        x = _moe(x.reshape(bsz * seq, d), ada, w_router, b_router, w_exp_gate[l], w_exp_up[l], w_exp_down[l],
                 ln2_g[l], ln2_b[l], seq).reshape(bsz, seq, d)
    return x
```

```python
import functools

import numpy as np
import jax
import jax.numpy as jnp
from jax import lax
from jax.experimental import pallas as pl
from jax.experimental.pallas import tpu as pltpu

D_MODEL = 1024
HEAD_DIM = 64
HEADS = 4
MIX_W = HEADS * HEAD_DIM
N_MIX = 4
IDX_HEADS = 8
IDX_DIM = 64
TOPK_KEYS = 256
MOBA_BLOCK = 256
MOBA_TOPK = 3
N_EXPERTS = 16
N_GROUPS = 4
D_EXPERT = 512
DEPTH = 2
DN_ALPHA = (2 * DEPTH) ** 0.25
LN_EPS = 1e-5
IN_COLS = 3 * N_MIX * MIX_W + IDX_HEADS * IDX_DIM + IDX_DIM + IDX_HEADS + HEADS

CH = 256
ACC_ROWS = HEAD_DIM + 16
MOE_SUB = 128
TQ_WIDE = CH
SHIFT_MARGIN = 96.0
LANES = 128
MISC_W = 128
WI_ROW = IDX_DIM
FL_ROW = IDX_DIM + IDX_HEADS
LOG2E = 1.4426950408889634
NEG = -1e30
QSCALE = HEAD_DIM ** -0.5 * LOG2E
INT_MIN = -(2 ** 31)
VMEM_LIMIT = 60 * 1024 * 1024

F32 = jnp.float32
BF16 = jnp.bfloat16
HI = lax.Precision.HIGHEST
NT = (((1,), (1,)), ((), ()))


def _alibi_slopes():
    n = 3 * HEADS
    s = 2.0 ** (-8.0 * np.arange(1, n + 1) / n)
    return s.reshape(HEADS, 3).T


def _dot(a, b):
    return jnp.dot(a, b, preferred_element_type=F32)


def _dot_hi(a, b):
    return jnp.dot(a, b, preferred_element_type=F32, precision=HI)


def _bit_transpose32(words):
    a = list(words)
    j, m = 16, 0x0000FFFF
    while j:
        k = 0
        while k < 32:
            t = (a[k] ^ (a[k + j] >> j)) & m
            a[k] = a[k] ^ t
            a[k + j] = a[k + j] ^ (t << j)
            k = (k + j + 1) & ~j
        j >>= 1
        m = (m ^ (m << j)) & 0xFFFFFFFF
    return a


def _split3(x):
    x1 = x.astype(BF16)
    r1 = x - x1.astype(F32)
    x2 = r1.astype(BF16)
    x3 = (r1 - x2.astype(F32)).astype(BF16)
    return x1, x2, x3


def _params(n_axes):
    return pltpu.CompilerParams(dimension_semantics=("arbitrary",) * n_axes,
                                vmem_limit_bytes=VMEM_LIMIT)


def _layer_norm(z, g, b):
    mu = jnp.mean(z, axis=-1, keepdims=True)
    var = jnp.mean(jnp.square(z - mu), axis=-1, keepdims=True)
    return (z - mu) * lax.rsqrt(var + LN_EPS) * g + b


def _ada_kernel(c_ref, w_ref, b_ref, o_ref):
    o_ref[0] = _dot_hi(c_ref[...], w_ref[0]) + b_ref[0]


def _ada(c, w_ada, b_ada):
    depth, d, n = w_ada.shape
    bsz = c.shape[0]
    tn = D_MODEL
    return pl.pallas_call(
        _ada_kernel,
        out_shape=jax.ShapeDtypeStruct((depth, bsz, n), F32),
        grid=(depth, n // tn),
        in_specs=[pl.BlockSpec((bsz, d), lambda l, j: (0, 0)),
                  pl.BlockSpec((1, d, tn), lambda l, j: (l, 0, j)),
                  pl.BlockSpec((1, 1, tn), lambda l, j: (l, 0, j))],
        out_specs=pl.BlockSpec((1, bsz, tn), lambda l, j: (l, 0, j)),
        compiler_params=_params(2),
        name="ada",
    )(c, w_ada, b_ada.reshape(depth, 1, n))


def _proj_kernel(x_ref, ada_ref, wk_ref, wm_ref, wt_ref, wmt_ref,
                 k_ref, kmean_ref, misc_ref, qT_ref, vT_ref, qiT_ref, miscT_ref, *, tm):
    x = x_ref[0]
    sh = ada_ref[0, 0:1, :]
    sc = ada_ref[0, 1:2, :]
    h = (x * (1.0 + sc) + sh).astype(BF16)
    kf = _dot(h, wk_ref[...])
    k_ref[0] = kf.astype(BF16)
    for g in range(tm // MOBA_BLOCK):
        kmean_ref[0, 0, g:g + 1, :] = jnp.mean(kf[g * MOBA_BLOCK:(g + 1) * MOBA_BLOCK], axis=0, keepdims=True)
    misc_ref[0] = _dot(h, wm_ref[...])
    t = lax.dot_general(wt_ref[...], h, NT, preferred_element_type=F32)
    nq = N_MIX * MIX_W
    qT_ref[0] = (t[0:nq] * QSCALE).astype(BF16)
    for g in range(tm // CH):
        vT_ref[0, g] = t[nq:2 * nq, g * CH:(g + 1) * CH].astype(BF16)
    qiT_ref[0] = t[2 * nq:].astype(BF16)
    miscT_ref[0] = lax.dot_general(wmt_ref[...], h, NT, preferred_element_type=F32)


def _proj(x, ada, w_in, tm=512):
    bsz, seq, d = x.shape
    nq = N_MIX * MIX_W
    nqi = IDX_HEADS * IDX_DIM
    w = w_in.astype(BF16)
    wk = w[:, nq:2 * nq]
    wm = jnp.pad(w[:, 3 * nq + nqi:], ((0, 0), (0, MISC_W - (IN_COLS - 3 * nq - nqi))))
    wt = jnp.concatenate([w[:, 0:nq], w[:, 2 * nq:3 * nq], w[:, 3 * nq:3 * nq + nqi]], axis=1).T
    wmt = wm.T
    nt = seq // tm
    full = lambda b, i: (0, 0)
    outs = pl.pallas_call(
        functools.partial(_proj_kernel, tm=tm),
        out_shape=(jax.ShapeDtypeStruct((bsz, seq, nq), BF16),
                   jax.ShapeDtypeStruct((bsz, nt, tm // MOBA_BLOCK, nq), F32),
                   jax.ShapeDtypeStruct((bsz, seq, MISC_W), F32),
                   jax.ShapeDtypeStruct((bsz, nq, seq), BF16),
                   jax.ShapeDtypeStruct((bsz, seq // CH, nq, CH), BF16),
                   jax.ShapeDtypeStruct((bsz, nqi, seq), BF16),
                   jax.ShapeDtypeStruct((bsz, MISC_W, seq), F32)),
        grid=(bsz, nt),
        in_specs=[pl.BlockSpec((1, tm, d), lambda b, i: (b, i, 0)),
                  pl.BlockSpec((1, 6, d), lambda b, i: (b, 0, 0)),
                  pl.BlockSpec(wk.shape, full),
                  pl.BlockSpec(wm.shape, full),
                  pl.BlockSpec(wt.shape, full),
                  pl.BlockSpec(wmt.shape, full)],
        out_specs=(pl.BlockSpec((1, tm, nq), lambda b, i: (b, i, 0)),
                   pl.BlockSpec((1, 1, tm // MOBA_BLOCK, nq), lambda b, i: (b, i, 0, 0)),
                   pl.BlockSpec((1, tm, MISC_W), lambda b, i: (b, i, 0)),
                   pl.BlockSpec((1, nq, tm), lambda b, i: (b, 0, i)),
                   pl.BlockSpec((1, tm // CH, nq, CH), lambda b, i: (b, i, 0, 0)),
                   pl.BlockSpec((1, nqi, tm), lambda b, i: (b, 0, i)),
                   pl.BlockSpec((1, MISC_W, tm), lambda b, i: (b, 0, i))),
        compiler_params=_params(2),
        name="proj",
    )(x, ada, wk, wm, wt, wmt)
    k, kmean, misc, qT, vT, qiT, miscT = outs
    return k, kmean.reshape(bsz, seq // MOBA_BLOCK, nq), misc, qT, vT, qiT, miscT


def _mask_heads(qT_ref, qm_ref):
    q = qT_ref[0]
    rowh = lax.broadcasted_iota(jnp.int32, q.shape, 0) // HEAD_DIM
    for h in range(HEADS):
        qm_ref[h] = jnp.where(rowh == h, q, jnp.zeros_like(q))


def _init_state(m_ref, acc_ref):
    m_ref[...] = jnp.full(m_ref.shape, NEG, F32)
    acc_ref[...] = jnp.zeros(acc_ref.shape, F32)


def _attention(i, first, k_ref, vT_ref, st, logits_fn, chunk_ctx=None, n_diag=None):
    qm_ref, m_ref, acc_ref, s_ref, mx_ref, p_ref, alpha_ref = st
    if n_diag is None:
        n_chunks = i - first + 1
    else:
        n_chunks = n_diag + (n_diag * i - first)
    ones = jnp.ones((ACC_ROWS - HEAD_DIM, CH), BF16)

    def chunk_of(n):
        n = jnp.clip(n, 0, n_chunks - 1)
        if n_diag is None:
            return first + n
        return jnp.where(n < n_diag, n_diag * i + n, first + n - n_diag)

    def position(n):
        if n_diag is None:
            return None
        return n if n < n_diag else -1

    def logits_head(h, slot, c, kc, ctx, diag, pen):
        x = logits_fn(h, c, _dot(kc, qm_ref[h]), ctx, diag, pen)
        s_ref[slot, h] = x
        mx_ref[slot, h] = jnp.max(x, axis=0, keepdims=True)

    def softmax_head(h, slot):
        m_old = m_ref[h]
        m_new = jnp.maximum(m_old, mx_ref[slot, h])
        alpha_ref[slot, h] = jnp.exp2(m_old - m_new)
        p_ref[slot, h] = jnp.exp2(s_ref[slot, h] - m_new).astype(BF16)
        m_ref[h] = m_new

    def pv_head(h, slot, vc):
        v1 = jnp.concatenate([vc[h * HEAD_DIM:(h + 1) * HEAD_DIM, :], ones], axis=0)
        acc_ref[h] = alpha_ref[slot, h] * acc_ref[h] + _dot(v1, p_ref[slot, h])

    def step(n, a, diag):
        c = chunk_of(n + 1)
        pen = jnp.where(n + 1 < n_chunks, 0.0, NEG)
        kc = k_ref[0, pl.ds(pl.multiple_of(c * CH, CH), CH), :]
        vc = vT_ref[0, chunk_of(n - 1)]
        ctx = chunk_ctx(c) if chunk_ctx is not None else None
        for h in range(HEADS):
            softmax_head(h, a)
        for h in range(HEADS):
            pv_head(h, 1 - a, vc)
        for h in range(HEADS):
            logits_head(h, 1 - a, c, kc, ctx, diag, pen)

    p_ref[1] = jnp.zeros(p_ref.shape[1:], BF16)
    alpha_ref[1] = jnp.ones(alpha_ref.shape[1:], F32)
    c0 = chunk_of(jnp.int32(0))
    k0 = k_ref[0, pl.ds(pl.multiple_of(c0 * CH, CH), CH), :]
    ctx0 = chunk_ctx(c0) if chunk_ctx is not None else None
    for h in range(HEADS):
        logits_head(h, 0, c0, k0, ctx0, position(0), 0.0)

    first_trip = 0
    if n_diag is not None and n_diag > 1:
        step(0, 0, position(1))
        step(1, 1, position(2))
        first_trip = 1
    later = position(2 * first_trip + 1)

    def pair(t, carry):
        step(2 * t, 0, later)
        step(2 * t + 1, 1, later)
        return carry

    n_trips = (n_chunks + 1) // 2
    lax.fori_loop(first_trip, n_trips, pair, 0)
    v_last = vT_ref[0, chunk_of(2 * n_trips - 1)]
    for h in range(HEADS):
        pv_head(h, 1, v_last)


def _attention_fixed_shift(i, first, k_ref, vT_ref, st, logits_fn):
    qm_ref, m_ref, acc_ref, _, _, p_ref, _ = st
    n_past = i - first
    ones = jnp.ones((ACC_ROWS - HEAD_DIM, CH), BF16)

    def chunk_of(u):
        return jnp.where(u == 0, i, jnp.minimum(first + u - 1, i))

    def produce(u, slot):
        c = chunk_of(u)
        pen = jnp.where(u <= n_past, 0.0, NEG)
        kc = k_ref[0, pl.ds(pl.multiple_of(c * CH, CH), CH), :]
        for h in range(HEADS):
            x = logits_fn(h, c, _dot(kc, qm_ref[h]), m_ref[h], -1, pen)
            p_ref[slot, h] = jnp.exp2(x).astype(BF16)

    def consume(u, slot):
        vc = vT_ref[0, chunk_of(u)]
        for h in range(HEADS):
            v1 = jnp.concatenate([vc[h * HEAD_DIM:(h + 1) * HEAD_DIM, :], ones], axis=0)
            acc_ref[h] = acc_ref[h] + _dot(v1, p_ref[slot, h])

    c0 = jnp.asarray(i, jnp.int32)
    k0 = k_ref[0, pl.ds(pl.multiple_of(c0 * CH, CH), CH), :]
    for h in range(HEADS):
        x = logits_fn(h, c0, _dot(k0, qm_ref[h]), None, 0, 0.0)
        shift = jnp.max(x, axis=0, keepdims=True) + SHIFT_MARGIN
        m_ref[h] = shift
        p_ref[0, h] = jnp.exp2(x - shift).astype(BF16)
    produce(jnp.int32(1), 1)

    def pair(t, carry):
        consume(2 * t, 0)
        produce(2 * t + 2, 0)
        consume(2 * t + 1, 1)
        produce(2 * t + 3, 1)
        return carry

    lax.fori_loop(0, (n_past + 2) // 2, pair, 0)


def _key_norms(k_ref, ksq_ref, n_chunks):
    row = lax.broadcasted_iota(jnp.int32, (MIX_W, LANES), 0) // HEAD_DIM
    ind = jnp.where(row == lax.broadcasted_iota(jnp.int32, (MIX_W, LANES), 1), 1.0, 0.0).astype(BF16)

    def body(c, best):
        kc = k_ref[0, pl.ds(pl.multiple_of(c * CH, CH), CH), :].astype(F32)
        sq = (kc * kc * (1.0 + 2.0 ** -7)).astype(BF16)
        return jnp.maximum(best, jnp.max(_dot(sq, ind), axis=0, keepdims=True))

    ksq_ref[...] = lax.fori_loop(0, n_chunks, body, jnp.zeros((1, LANES), F32))


def _small_products(qm_ref, ksq_ref):
    lane = lax.broadcasted_iota(jnp.int32, (1, LANES), 1)
    worst = jnp.float32(0.0)
    for h in range(HEADS):
        q = qm_ref[h].astype(F32)
        qsq = jnp.max(jnp.sum(q * q, axis=0, keepdims=True))
        ksq = jnp.max(jnp.where(lane == h, ksq_ref[...], 0.0))
        worst = jnp.maximum(worst, qsq * ksq)
    return worst * 1.01 <= (SHIFT_MARGIN / 2.0) ** 2


def _self_attending(i, first, k_ref, vT_ref, st, logits_fn, ksq_ref):
    assert TQ_WIDE == CH
    small = _small_products(st[0], ksq_ref)

    @pl.when(small)
    def _fixed():
        _attention_fixed_shift(i, first, k_ref, vT_ref, st, logits_fn)

    @pl.when(jnp.logical_not(small))
    def _running():
        _attention(i, first, k_ref, vT_ref, st, logits_fn, n_diag=1)


def _finish(o_ref, acc_ref):
    parts = [acc_ref[h, 0:HEAD_DIM, :] / acc_ref[h, HEAD_DIM:HEAD_DIM + 1, :] for h in range(HEADS)]
    o_ref[0] = jnp.concatenate(parts, axis=0).T.astype(BF16)


def _lanes(tile, tq):
    return tile if tq == LANES else jnp.concatenate([tile] * (tq // LANES), axis=1)


def _causal_neg(tq, d=0):
    s_i = lax.broadcasted_iota(jnp.int32, (CH, tq), 0) + d * CH
    t_i = lax.broadcasted_iota(jnp.int32, (CH, tq), 1)
    return jnp.where(s_i <= t_i, 0.0, NEG).astype(F32)


def _key_pos_bias(bias_ref, slopes):
    s_i = lax.broadcasted_iota(jnp.int32, (CH, LANES), 0).astype(F32)
    for h in range(HEADS):
        bias_ref[h] = s_i * float(slopes[h] * LOG2E)


def _attn_specs(mixer, seq, tq=CH):
    return [pl.BlockSpec((1, MIX_W, tq), lambda b, i: (b, mixer, i)),
            pl.BlockSpec((1, seq, MIX_W), lambda b, i: (b, 0, mixer)),
            pl.BlockSpec((1, seq // CH, MIX_W, CH), lambda b, i: (b, 0, mixer, 0))]


def _attn_scratch(tq=CH):
    return [pltpu.VMEM((HEADS, MIX_W, tq), BF16),
            pltpu.VMEM((HEADS, 1, tq), F32),
            pltpu.VMEM((HEADS, ACC_ROWS, tq), F32),
            pltpu.VMEM((2, HEADS, CH, tq), F32),
            pltpu.VMEM((2, HEADS, 1, tq), F32),
            pltpu.VMEM((2, HEADS, CH, tq), BF16),
            pltpu.VMEM((2, HEADS, 1, tq), F32)]


def _fox_kernel(qT_ref, k_ref, vT_ref, misc_ref, bf_ref, o_ref,
                cum_ref, ksq_ref, *st, n_chunks):
    qm_ref, m_ref, acc_ref = st[:3]
    i = pl.program_id(1)

    @pl.when(i == 0)
    def _cumulative_gates():
        _key_norms(k_ref, ksq_ref, n_chunks)
        rr = lax.broadcasted_iota(jnp.int32, (CH, CH), 0)
        cc = lax.broadcasted_iota(jnp.int32, (CH, CH), 1)
        tri = jnp.where(cc <= rr, 1.0, 0.0).astype(BF16)

        def body(blk, carry):
            off = pl.multiple_of(blk * CH, CH)
            z = misc_ref[0, pl.ds(off, CH), :] + bf_ref[...]
            ls = jnp.minimum(z, 0.0) - jnp.log1p(jnp.exp(-jnp.abs(z)))
            csum = sum(_dot(tri, part) for part in _split3(ls))
            new = []
            for h in range(HEADS):
                col = csum[:, FL_ROW + h:FL_ROW + h + 1]
                cum = jnp.broadcast_to(col, (CH, LANES)) + carry[h]
                cum_ref[h, pl.ds(off, CH), :] = cum * LOG2E
                new.append(cum[CH - 1:CH, :])
            return tuple(new)

        lax.fori_loop(0, n_chunks, body, tuple(jnp.zeros((1, LANES), F32) for _ in range(HEADS)))

    _mask_heads(qT_ref, qm_ref)
    _init_state(m_ref, acc_ref)

    def logits(h, c, qk, shift, diag, pen):
        off = pl.multiple_of(c * CH, CH)
        cum = cum_ref[h, pl.ds(off, CH), :]
        if diag >= 0:
            return qk - _lanes(cum, TQ_WIDE) + _causal_neg(TQ_WIDE, diag)
        x = qk - _lanes(cum - pen, TQ_WIDE)
        return x if shift is None else x - shift

    _self_attending(i, 0, k_ref, vT_ref, st, logits, ksq_ref)
    _finish(o_ref, acc_ref)


def _fox(qT, k, vT, misc, b_forget_l):
    bsz, seq, _ = k.shape
    n_chunks = seq // CH
    bf = jnp.zeros((1, MISC_W), F32).at[0, FL_ROW:FL_ROW + HEADS].set(b_forget_l)
    return pl.pallas_call(
        functools.partial(_fox_kernel, n_chunks=n_chunks),
        out_shape=jax.ShapeDtypeStruct((bsz, seq, MIX_W), BF16),
        grid=(bsz, seq // TQ_WIDE),
        in_specs=_attn_specs(3, seq, TQ_WIDE) + [
            pl.BlockSpec((1, seq, MISC_W), lambda b, i: (b, 0, 0)),
            pl.BlockSpec((1, MISC_W), lambda b, i: (0, 0))],
        out_specs=pl.BlockSpec((1, TQ_WIDE, MIX_W), lambda b, i: (b, i, 0)),
        scratch_shapes=[pltpu.VMEM((HEADS, seq, LANES), F32),
                        pltpu.VMEM((1, LANES), F32)] + _attn_scratch(TQ_WIDE),
        compiler_params=_params(2),
        name="fox",
    )(qT, k, vT, misc, bf)


def _moba_kernel(qT_ref, k_ref, vT_ref, kmean_ref, o_ref,
                 bias_ref, rowadd_ref, ksq_ref, *st, slopes, n_blocks):
    qm_ref, m_ref, acc_ref = st[:3]
    i = pl.program_id(1)
    tq = TQ_WIDE
    n_diag = tq // CH
    _mask_heads(qT_ref, qm_ref)
    _init_state(m_ref, acc_ref)

    @pl.when(i == 0)
    def _norms():
        _key_norms(k_ref, ksq_ref, n_blocks * MOBA_BLOCK // CH)

    @pl.when(jnp.logical_and(pl.program_id(0) == 0, i == 0))
    def _bias_table():
        s_f = lax.broadcasted_iota(jnp.int32, (CH, tq), 0).astype(F32)
        for h in range(HEADS):
            pos = s_f * float(slopes[h] * LOG2E)
            bias_ref[0, h] = pos
            for d in range(n_diag):
                bias_ref[1 + d, h] = pos + _causal_neg(tq, d)

    n_i = lax.broadcasted_iota(jnp.int32, (n_blocks, tq), 0)
    n_f = n_i.astype(F32)
    own = n_diag * i + lax.broadcasted_iota(jnp.int32, (n_blocks, tq), 1) // MOBA_BLOCK
    past = n_i < own
    kmean_parts = _split3(kmean_ref[0])
    for h in range(HEADS):
        gate = sum(_dot(part, qm_ref[h]) for part in kmean_parts)
        gate = jnp.where(past, gate, -jnp.inf)
        chosen = jnp.zeros((n_blocks, tq), F32)
        for _ in range(MOBA_TOPK):
            mx = jnp.max(gate, axis=0, keepdims=True)
            first = jnp.min(jnp.where(gate == mx, n_f, float(n_blocks)), axis=0, keepdims=True)
            pick = n_f == first
            chosen = jnp.where(pick, 1.0, chosen)
            gate = jnp.where(pick, -jnp.inf, gate)
        chosen = jnp.where(past, chosen, 0.0)
        blk_shift = n_f * float(slopes[h] * LOG2E * MOBA_BLOCK)
        rowadd_ref[h] = jnp.where(jnp.logical_or(chosen > 0.5, n_i == own), blk_shift, NEG)

    def logits(h, c, qk, shift, diag, pen):
        row = rowadd_ref[h, pl.ds(c, 1), :]
        if diag >= 0:
            return qk + bias_ref[1 + diag, h] + row
        row = row + pen
        return qk + bias_ref[0, h] + (row if shift is None else row - shift)

    _self_attending(i, 0, k_ref, vT_ref, st, logits, ksq_ref)
    _finish(o_ref, acc_ref)


def _moba(qT, k, vT, kmean, slopes):
    bsz, seq, _ = k.shape
    n_blocks = seq // MOBA_BLOCK
    return pl.pallas_call(
        functools.partial(_moba_kernel, slopes=tuple(float(s) for s in slopes), n_blocks=n_blocks),
        out_shape=jax.ShapeDtypeStruct((bsz, seq, MIX_W), BF16),
        grid=(bsz, seq // TQ_WIDE),
        in_specs=_attn_specs(2, seq, TQ_WIDE) + [
            pl.BlockSpec((1, n_blocks, MIX_W), lambda b, i: (b, 0, 2))],
        out_specs=pl.BlockSpec((1, TQ_WIDE, MIX_W), lambda b, i: (b, i, 0)),
        scratch_shapes=[pltpu.VMEM((1 + TQ_WIDE // CH, HEADS, CH, TQ_WIDE), F32),
                        pltpu.VMEM((HEADS, n_blocks, TQ_WIDE), F32),
                        pltpu.VMEM((1, LANES), F32)] + _attn_scratch(TQ_WIDE),
        compiler_params=_params(2),
        name="moba",
    )(qT, k, vT, kmean)


DIL_SPAN = 2048 // CH + 1


def _dilated_kernel(qT_ref, k_ref, vT_ref, o_ref, table_ref, ksq_ref, *st, slopes):
    qm_ref, m_ref, acc_ref = st[:3]
    b = pl.program_id(0)
    i = pl.program_id(1)

    tq = TQ_WIDE
    n_diag = tq // CH
    n_entries = n_diag - 1 + DIL_SPAN

    @pl.when(jnp.logical_and(b == 0, i == 0))
    def _bias_table():
        s_i = lax.broadcasted_iota(jnp.int32, (CH, tq), 0)
        t_i = lax.broadcasted_iota(jnp.int32, (CH, tq), 1)
        for e in range(n_entries):
            j = e - (n_diag - 1)
            d = t_i - s_i + j * CH
            ok = d >= 0
            mult = (jnp.where(jnp.logical_and(ok, d <= 128), 1.0, 0.0)
                    + jnp.where(jnp.logical_and(ok, jnp.logical_and(d <= 512, (d & 3) == 0)), 1.0, 0.0)
                    + jnp.where(jnp.logical_and(ok, jnp.logical_and(d <= 2048, (d & 15) == 0)), 1.0, 0.0))
            logm = jnp.where(mult > 0.5, jnp.log2(jnp.maximum(mult, 1.0)), NEG)
            df = d.astype(F32)
            for h in range(HEADS):
                table_ref[h, e] = logm - df * float(slopes[h] * LOG2E)
        for h in range(HEADS):
            table_ref[h, n_entries] = jnp.full((CH, tq), NEG, F32)

    _mask_heads(qT_ref, qm_ref)
    _init_state(m_ref, acc_ref)

    def logits(h, c, qk, shift, diag, pen):
        if diag >= 0:
            return qk + table_ref[h, n_diag - 1 - diag]
        x = qk + table_ref[h, jnp.where(pen < 0.0, n_entries, n_diag * i - c + (n_diag - 1))]
        return x if shift is None else x - shift

    @pl.when(i == 0)
    def _norms():
        _key_norms(k_ref, ksq_ref, k_ref.shape[1] // CH)

    _self_attending(i, jnp.maximum(n_diag * i - (DIL_SPAN - 1), 0), k_ref, vT_ref, st, logits, ksq_ref)
    _finish(o_ref, acc_ref)


def _dilated(qT, k, vT, slopes):
    bsz, seq, _ = k.shape
    return pl.pallas_call(
        functools.partial(_dilated_kernel, slopes=tuple(float(s) for s in slopes)),
        out_shape=jax.ShapeDtypeStruct((bsz, seq, MIX_W), BF16),
        grid=(bsz, seq // TQ_WIDE),
        in_specs=_attn_specs(1, seq, TQ_WIDE),
        out_specs=pl.BlockSpec((1, TQ_WIDE, MIX_W), lambda b, i: (b, i, 0)),
        scratch_shapes=[pltpu.VMEM((HEADS, TQ_WIDE // CH + DIL_SPAN, CH, TQ_WIDE), F32),
                        pltpu.VMEM((1, LANES), F32)] + _attn_scratch(TQ_WIDE),
        compiler_params=_params(2),
        name="dilated",
    )(qT, k, vT)


def _dsa_kernel(qT_ref, k_ref, vT_ref, qiT_ref, misc_ref, miscT_ref, o_ref,
                bias_ref, qi_ref, key_ref, planes_ref, alive_ref, seen_ref, *st, slopes):
    qm_ref, m_ref, acc_ref = st[:3]
    i = pl.program_id(1)
    tq = CH

    @pl.when(jnp.logical_and(pl.program_id(0) == 0, i == 0))
    def _clear_planes():
        planes_ref[0:32] = jnp.zeros((32,) + planes_ref.shape[1:], jnp.int32)
        planes_ref[32] = jnp.full(planes_ref.shape[1:], -1, jnp.int32)
    _mask_heads(qT_ref, qm_ref)
    _init_state(m_ref, acc_ref)
    _key_pos_bias(bias_ref, slopes)

    qi_all = qiT_ref[0]
    zpad = jnp.zeros((MISC_W - IDX_DIM, tq), BF16)
    for h in range(IDX_HEADS):
        qi_ref[h] = jnp.concatenate([qi_all[h * IDX_DIM:(h + 1) * IDX_DIM], zpad], axis=0)
    w_rows = miscT_ref[0, WI_ROW:WI_ROW + IDX_HEADS, :] * float(IDX_HEADS ** -0.5 * IDX_DIM ** -0.5)

    s_i = lax.broadcasted_iota(jnp.int32, (CH, tq), 0)
    t_i = lax.broadcasted_iota(jnp.int32, (CH, tq), 1)

    def score_chunk(c, diag):
        off = pl.multiple_of(c * CH, CH)
        ki = misc_ref[0, pl.ds(off, CH), :].astype(BF16)
        score = jnp.zeros((CH, tq), F32)
        for h in range(IDX_HEADS):
            rel = jnp.maximum(_dot(ki, qi_ref[h]), 0.0)
            score = score + rel * w_rows[h:h + 1, :]
        bits = lax.bitcast_convert_type(score, jnp.int32)
        key = bits ^ ((bits >> 31) & 0x7FFFFFFF)
        key = jnp.where(key == -1, 0, key)
        if diag:
            key = jnp.where(s_i <= t_i, key, INT_MIN)
        key_ref[pl.ds(off, CH), :] = key
        ukey = key ^ INT_MIN
        words = _bit_transpose32([ukey[8 * j:8 * j + 8, :] for j in range(32)])
        row0 = pl.multiple_of(c * 8, 8)
        for b in range(32):
            planes_ref[b, pl.ds(row0, 8), :] = words[31 - b]

    score_chunk(i, True)

    def score_pair(t, carry):
        score_chunk(2 * t, False)
        score_chunk(jnp.minimum(2 * t + 1, i - 1), False)
        return carry

    lax.fori_loop(0, (i + 1) // 2, score_pair, 0)
    group = 4 * 8
    n_groups = i // 4 + 1
    alive_ref[...] = jnp.where(lax.broadcasted_iota(jnp.int32, alive_ref.shape, 0) < (i + 1) * 8, -1, 0)

    def narrow(alive, rows, b_prev, took_prev):
        hit = alive & planes_ref[b_prev, rows, :]
        return jnp.where(took_prev, hit, alive ^ hit)

    def bit_step(n, state):
        thr, need, took_prev = state
        b = 31 - n
        took_prev = took_prev != 0

        def sweep(g, cnt):
            rows = pl.ds(pl.multiple_of(g * group, group), group)
            alive = narrow(alive_ref[rows, :], rows, b + 1, took_prev)
            alive_ref[rows, :] = alive
            return cnt + lax.population_count(alive & planes_ref[b, rows, :])

        cnt = lax.fori_loop(0, n_groups, sweep, jnp.zeros((group, tq), jnp.int32))
        cnt = jnp.sum(cnt, axis=0, keepdims=True)
        ok = cnt >= need
        return (jnp.where(ok, thr | jnp.left_shift(jnp.int32(1), b), thr), jnp.where(ok, need, need - cnt),
                jnp.where(ok, 1, 0))

    uthr, n_take, took_last = lax.fori_loop(
        0, 32, bit_step, (jnp.zeros((1, tq), jnp.int32), jnp.full((1, tq), TOPK_KEYS, jnp.int32),
                          jnp.ones((1, tq), jnp.int32)))
    thr = uthr ^ INT_MIN

    def count_alive(g, cnt):
        rows = pl.ds(pl.multiple_of(g * group, group), group)
        return cnt + lax.population_count(narrow(alive_ref[rows, :], rows, 0, took_last != 0))

    n_eq = jnp.sum(lax.fori_loop(0, n_groups, count_alive, jnp.zeros((group, tq), jnp.int32)), axis=0, keepdims=True)
    tied = jnp.logical_and(n_eq > n_take, thr != INT_MIN)
    any_tied = jnp.max(jnp.where(tied, 1.0, 0.0)) > 0.5

    def logits(h, c, qk, neg, diag, pen):
        pos = bias_ref[h] + (c.astype(F32) * float(slopes[h] * LOG2E * CH) + pen)
        return qk + _lanes(pos, tq) + neg

    @pl.when(jnp.logical_not(any_tied))
    def _no_ties():
        thr_lo = jnp.where(thr == INT_MIN, INT_MIN + 1, thr)

        def selection(c):
            off = pl.multiple_of(c * CH, CH)
            return jnp.where(key_ref[pl.ds(off, CH), :] >= thr_lo, 0.0, NEG)

        _attention(i, 0, k_ref, vT_ref, st, logits, selection)

    @pl.when(any_tied)
    def _ties():
        n_first = jnp.where(thr == INT_MIN, 0, n_take).astype(F32)
        rr = lax.broadcasted_iota(jnp.int32, (CH, CH), 0)
        cc = lax.broadcasted_iota(jnp.int32, (CH, CH), 1)
        tri = jnp.where(cc <= rr, 1.0, 0.0).astype(BF16)
        seen_ref[...] = jnp.zeros(seen_ref.shape, F32)

        def selection(c):
            off = pl.multiple_of(c * CH, CH)
            key = key_ref[pl.ds(off, CH), :]
            eq = key == thr
            rank = _dot(tri, jnp.where(eq, 1.0, 0.0).astype(BF16)) + seen_ref[...]
            seen_ref[...] = rank[CH - 1:CH, :]
            take = jnp.logical_or(key > thr, jnp.logical_and(eq, rank <= n_first))
            return jnp.where(take, 0.0, NEG)

        _attention(i, 0, k_ref, vT_ref, st, logits, selection)

    _finish(o_ref, acc_ref)


def _dsa(qT, k, vT, qiT, misc, miscT, slopes):
    bsz, seq, _ = k.shape
    nqi = IDX_HEADS * IDX_DIM
    plane_rows = -(-(seq // CH) // 4) * 32
    return pl.pallas_call(
        functools.partial(_dsa_kernel, slopes=tuple(float(s) for s in slopes)),
        out_shape=jax.ShapeDtypeStruct((bsz, seq, MIX_W), BF16),
        grid=(bsz, seq // CH),
        in_specs=_attn_specs(0, seq) + [
            pl.BlockSpec((1, nqi, CH), lambda b, i: (b, 0, i)),
            pl.BlockSpec((1, seq, MISC_W), lambda b, i: (b, 0, 0)),
            pl.BlockSpec((1, MISC_W, CH), lambda b, i: (b, 0, i))],
        out_specs=pl.BlockSpec((1, CH, MIX_W), lambda b, i: (b, i, 0)),
        scratch_shapes=[pltpu.VMEM((HEADS, CH, LANES), F32),
                        pltpu.VMEM((IDX_HEADS, MISC_W, CH), BF16),
                        pltpu.VMEM((seq, CH), jnp.int32),
                        pltpu.VMEM((33, plane_rows, CH), jnp.int32),
                        pltpu.VMEM((plane_rows, CH), jnp.int32),
                        pltpu.VMEM((1, CH), F32)] + _attn_scratch(),
        compiler_params=_params(2),
        name="dsa",
    )(qT, k, vT, qiT, misc, miscT)


def _merge_kernel(x_ref, ada_ref, oa_ref, ob_ref, oc_ref, od_ref, wmg_ref, wbr_ref, wout_ref,
                  g_ref, b_ref, out_ref):
    x = x_ref[0]
    sh = ada_ref[0, 0:1, :]
    sc = ada_ref[0, 1:2, :]
    g1 = ada_ref[0, 2:3, :]
    h = (x * (1.0 + sc) + sh).astype(BF16)
    mixed = None
    for m, o_ref in enumerate((oa_ref, ob_ref, oc_ref, od_ref)):
        gate = jax.nn.sigmoid(_dot(h, wmg_ref[m]))
        term = gate * _dot(o_ref[0], wbr_ref[m])
        mixed = term if mixed is None else mixed + term
    y = _dot(mixed.astype(BF16), wout_ref[...])
    out_ref[0] = _layer_norm(DN_ALPHA * x + g1 * y, g_ref[...], b_ref[...])


def _merge(x, ada, outs, w_mg, w_br, w_out, ln_g, ln_b, tm=512):
    bsz, seq, d = x.shape
    o_spec = pl.BlockSpec((1, tm, MIX_W), lambda b, i: (b, i, 0))
    return pl.pallas_call(
        _merge_kernel,
        out_shape=jax.ShapeDtypeStruct((bsz, seq, d), F32),
        grid=(bsz, seq // tm),
        in_specs=[pl.BlockSpec((1, tm, d), lambda b, i: (b, i, 0)),
                  pl.BlockSpec((1, 6, d), lambda b, i: (b, 0, 0)),
                  o_spec, o_spec, o_spec, o_spec,
                  pl.BlockSpec((N_MIX, d, d), lambda b, i: (0, 0, 0)),
                  pl.BlockSpec((N_MIX, MIX_W, d), lambda b, i: (0, 0, 0)),
                  pl.BlockSpec((d, d), lambda b, i: (0, 0)),
                  pl.BlockSpec((1, d), lambda b, i: (0, 0)),
                  pl.BlockSpec((1, d), lambda b, i: (0, 0))],
        out_specs=pl.BlockSpec((1, tm, d), lambda b, i: (b, i, 0)),
        compiler_params=_params(2),
        name="merge",
    )(x, ada, *outs, w_mg.astype(BF16), w_br.astype(BF16), w_out.astype(BF16),
      ln_g.reshape(1, d), ln_b.reshape(1, d))


def _route(logits_t, bias_col):
    per_group = N_EXPERTS // N_GROUPS
    scores = jax.nn.sigmoid(logits_t)
    biased = scores + bias_col
    s_rows = [scores[e:e + 1] for e in range(N_EXPERTS)]
    b_rows = [biased[e:e + 1] for e in range(N_EXPERTS)]
    best_g = None
    for g in range(N_GROUPS):
        r = b_rows[g * per_group:(g + 1) * per_group]
        gs = None
        for a in range(per_group):
            for b in range(a + 1, per_group):
                pair = r[a] + r[b]
                gs = pair if gs is None else jnp.maximum(gs, pair)
        if best_g is None:
            best_g, best_v = jnp.zeros_like(gs, dtype=jnp.int32), gs
        else:
            better = gs > best_v
            best_g = jnp.where(better, g, best_g)
            best_v = jnp.maximum(best_v, gs)
    masked = [jnp.where(best_g == e // per_group, b_rows[e], -jnp.inf) for e in range(N_EXPERTS)]

    def argmax_first(rows):
        idx, val = jnp.zeros_like(best_g), rows[0]
        for e in range(1, N_EXPERTS):
            better = rows[e] > val
            idx = jnp.where(better, e, idx)
            val = jnp.maximum(val, rows[e])
        return idx

    e1 = argmax_first(masked)
    e2 = argmax_first([jnp.where(e1 == e, -jnp.inf, masked[e]) for e in range(N_EXPERTS)])
    s1 = sum(jnp.where(e1 == e, s_rows[e], 0.0) for e in range(N_EXPERTS))
    s2 = sum(jnp.where(e2 == e, s_rows[e], 0.0) for e in range(N_EXPERTS))
    tot = s1 + s2
    w1, w2 = s1 / tot, s2 / tot
    rows = [jnp.where(e1 == e, w1, 0.0) + jnp.where(e2 == e, w2, 0.0) for e in range(N_EXPERTS)]
    return jnp.concatenate(rows, axis=0), best_g


def _moe_kernel(x_ref, ada_ref, wr_ref, br_ref, wg_ref, wu_ref, wd_ref, g_ref, b_ref, out_ref,
                perm_ref, xs_ref, combs_ref, ys_ref, sub_ref, *, tm, rows):
    g = pl.program_id(1)
    per_group = N_EXPERTS // N_GROUPS

    @pl.when(g == 0)
    def _route_and_sort():
        x = x_ref[...]
        h2 = x * (1.0 + ada_ref[0, 4:5, :]) + ada_ref[0, 3:4, :]
        h_hi = h2.astype(BF16)
        h_lo = (h2 - h_hi.astype(F32)).astype(BF16)
        first = _dot(h_hi, wr_ref[...])
        logits = first[:, 0:LANES] + first[:, LANES:] + _dot(h_lo, wr_ref[:, 0:LANES])
        logits_t = logits.T[0:N_EXPERTS]
        comb_t, best_g = _route(logits_t, br_ref[...])
        member = [jnp.where(best_g == g, 1.0, 0.0) for g in range(N_GROUPS)]
        grp = jnp.concatenate(member + [jnp.zeros((8 - N_GROUPS, tm), F32)], axis=0).astype(BF16)
        s_i = lax.broadcasted_iota(jnp.int32, (tm, tm), 0)
        t_i = lax.broadcasted_iota(jnp.int32, (tm, tm), 1)
        rank = _dot(grp, jnp.where(s_i <= t_i, 1.0, 0.0).astype(BF16))
        pos = jnp.zeros((1, tm), F32)
        start = jnp.zeros((1, 1), F32)
        for g in range(N_GROUPS):
            cap = jnp.ceil(rank[g:g + 1, tm - 1:tm] * (1.0 / MOE_SUB)) * MOE_SUB
            pos = pos + member[g] * (start + rank[g:g + 1] - 1.0)
            sub_ref[2 * g] = (jnp.sum(start) * (1.0 / MOE_SUB)).astype(jnp.int32)
            sub_ref[2 * g + 1] = (jnp.sum(cap) * (1.0 / MOE_SUB)).astype(jnp.int32)
            start = start + cap
        r_f = lax.broadcasted_iota(jnp.int32, (rows, tm), 0).astype(F32)
        perm = jnp.where(r_f == pos, 1.0, 0.0).astype(BF16)
        perm_ref[...] = perm
        comb = jnp.concatenate([comb_t, jnp.zeros((LANES - N_EXPERTS, tm), F32)], axis=0).T
        c_hi = comb.astype(BF16)
        c_lo = (comb - c_hi.astype(F32)).astype(BF16)
        d = h_hi.shape[1]
        moved = _dot(perm, jnp.concatenate([h_hi, c_hi, c_lo], axis=1))
        xs_ref[...] = moved[:, 0:d].astype(BF16)
        combs_ref[...] = moved[:, d:d + LANES] + moved[:, d + LANES:]
        ys_ref[...] = jnp.zeros(ys_ref.shape, BF16)

    lane = lax.broadcasted_iota(jnp.int32, (MOE_SUB, LANES), 1)

    def sub_tile(j, carry):
        r0 = pl.multiple_of((sub_ref[2 * g] + j) * MOE_SUB, MOE_SUB)
        xj = xs_ref[pl.ds(r0, MOE_SUB), :]
        cw = combs_ref[pl.ds(r0, MOE_SUB), :]
        hids = []
        for q in range(per_group):
            w_e = jnp.sum(jnp.where(lane == g * per_group + q, cw, 0.0), axis=1, keepdims=True)
            hids.append((jax.nn.silu(_dot(xj, wg_ref[0, q])) * _dot(xj, wu_ref[0, q]) * w_e).astype(BF16))
        ys_ref[pl.ds(r0, MOE_SUB), :] = _dot(jnp.concatenate(hids, axis=1), wd_ref[0, 0]).astype(BF16)
        return carry

    lax.fori_loop(0, sub_ref[2 * g + 1], sub_tile, 0)

    @pl.when(g == N_GROUPS - 1)
    def _unsort_and_norm():
        y = lax.dot_general(perm_ref[...], ys_ref[...], (((0,), (0,)), ((), ())),
                            preferred_element_type=F32)
        z = DN_ALPHA * x_ref[...] + ada_ref[0, 5:6, :] * y
        out_ref[...] = _layer_norm(z, g_ref[...], b_ref[...])


def _moe(x, ada_rows, w_router, b_router, w_gate, w_up, w_down, ln_g, ln_b, batch_len, tm=1024):
    n, d = x.shape
    per_b = batch_len // tm
    per_group = N_EXPERTS // N_GROUPS
    rows = tm + N_GROUPS * MOE_SUB
    wr32 = jnp.pad(w_router, ((0, 0), (0, LANES - N_EXPERTS)))
    wr_hi = wr32.astype(BF16)
    wr = jnp.concatenate([wr_hi, (wr32 - wr_hi.astype(F32)).astype(BF16)], axis=1)
    br = b_router.reshape(N_EXPERTS, 1)
    return pl.pallas_call(
        functools.partial(_moe_kernel, tm=tm, rows=rows),
        out_shape=jax.ShapeDtypeStruct((n, d), F32),
        grid=(n // tm, N_GROUPS),
        in_specs=[pl.BlockSpec((tm, d), lambda i, g: (i, 0)),
                  pl.BlockSpec((1, 6, d), lambda i, g: (i // per_b, 0, 0)),
                  pl.BlockSpec((d, 2 * LANES), lambda i, g: (0, 0)),
                  pl.BlockSpec((N_EXPERTS, 1), lambda i, g: (0, 0)),
                  pl.BlockSpec((1, per_group, d, D_EXPERT), lambda i, g: (0, g, 0, 0)),
                  pl.BlockSpec((1, per_group, d, D_EXPERT), lambda i, g: (0, g, 0, 0)),
                  pl.BlockSpec((1, 1, per_group * D_EXPERT, d), lambda i, g: (0, g, 0, 0)),
                  pl.BlockSpec((1, d), lambda i, g: (0, 0)),
                  pl.BlockSpec((1, d), lambda i, g: (0, 0))],
        out_specs=pl.BlockSpec((tm, d), lambda i, g: (i, 0)),
        scratch_shapes=[pltpu.VMEM((rows, tm), BF16),
                        pltpu.VMEM((rows, d), BF16),
                        pltpu.VMEM((rows, LANES), F32),
                        pltpu.VMEM((rows, d), BF16),
                        pltpu.SMEM((2 * N_GROUPS,), jnp.int32)],
        compiler_params=_params(2),
        name="moe",
    )(x, ada_rows, wr, br, w_gate[None].astype(BF16), w_up[None].astype(BF16),
      w_down.reshape(1, N_GROUPS, per_group * D_EXPERT, d).astype(BF16),
      ln_g.reshape(1, d), ln_b.reshape(1, d))


def kernel(x, c, w_in, b_forget, w_branch, w_merge_gate, w_out, w_ada, b_ada, ln1_g, ln1_b, ln2_g, ln2_b,
           w_router, b_router, w_exp_gate, w_exp_up, w_exp_down):
    bsz, seq, d = x.shape
    slopes = _alibi_slopes()
    ada_all = _ada(c, w_ada, b_ada).reshape(DEPTH, bsz, 6, d)
    for l in range(DEPTH):
        ada = ada_all[l]
        k, kmean, misc, qT, vT, qiT, miscT = _proj(x, ada, w_in[l])
        o_a = _dsa(qT, k, vT, qiT, misc, miscT, slopes[0])
        o_b = _dilated(qT, k, vT, slopes[1])
        o_c = _moba(qT, k, vT, kmean, slopes[2])
        o_d = _fox(qT, k, vT, misc, b_forget[l])
        x = _merge(x, ada, (o_a, o_b, o_c, o_d), w_merge_gate[l], w_branch[l], w_out[l], ln1_g[l], ln1_b[l])
        x = _moe(x.reshape(bsz * seq, d), ada, w_router, b_router, w_exp_gate[l], w_exp_up[l], w_exp_down[l],
                 ln2_g[l], ln2_b[l], seq).reshape(bsz, seq, d)
    return x
```

```python
import functools

import numpy as np
import jax
import jax.numpy as jnp
from jax import lax
from jax.experimental import pallas as pl
from jax.experimental.pallas import tpu as pltpu

D_MODEL = 1024
HEAD_DIM = 64
HEADS = 4
MIX_W = HEADS * HEAD_DIM
N_MIX = 4
IDX_HEADS = 8
IDX_DIM = 64
TOPK_KEYS = 256
MOBA_BLOCK = 256
MOBA_TOPK = 3
N_EXPERTS = 16
N_GROUPS = 4
D_EXPERT = 512
DEPTH = 2
DN_ALPHA = (2 * DEPTH) ** 0.25
LN_EPS = 1e-5
IN_COLS = 3 * N_MIX * MIX_W + IDX_HEADS * IDX_DIM + IDX_DIM + IDX_HEADS + HEADS

CH = 256
ACC_ROWS = HEAD_DIM + 16
MOE_SUB = 128
TQ_WIDE = CH
SHIFT_MARGIN = 96.0
LANES = 128
MISC_W = 128
WI_ROW = IDX_DIM
FL_ROW = IDX_DIM + IDX_HEADS
LOG2E = 1.4426950408889634
NEG = -1e30
QSCALE = HEAD_DIM ** -0.5 * LOG2E
INT_MIN = -(2 ** 31)
VMEM_LIMIT = 60 * 1024 * 1024

F32 = jnp.float32
BF16 = jnp.bfloat16
HI = lax.Precision.HIGHEST
NT = (((1,), (1,)), ((), ()))


def _alibi_slopes():
    n = 3 * HEADS
    s = 2.0 ** (-8.0 * np.arange(1, n + 1) / n)
    return s.reshape(HEADS, 3).T


def _dot(a, b):
    return jnp.dot(a, b, preferred_element_type=F32)


def _dot_hi(a, b):
    return jnp.dot(a, b, preferred_element_type=F32, precision=HI)


def _bit_transpose32(words):
    a = list(words)
    j, m = 16, 0x0000FFFF
    while j:
        k = 0
        while k < 32:
            t = (a[k] ^ (a[k + j] >> j)) & m
            a[k] = a[k] ^ t
            a[k + j] = a[k + j] ^ (t << j)
            k = (k + j + 1) & ~j
        j >>= 1
        m = (m ^ (m << j)) & 0xFFFFFFFF
    return a


def _split3(x):
    x1 = x.astype(BF16)
    r1 = x - x1.astype(F32)
    x2 = r1.astype(BF16)
    x3 = (r1 - x2.astype(F32)).astype(BF16)
    return x1, x2, x3


def _params(n_axes):
    return pltpu.CompilerParams(dimension_semantics=("arbitrary",) * n_axes,
                                vmem_limit_bytes=VMEM_LIMIT)


def _layer_norm(z, g, b):
    mu = jnp.mean(z, axis=-1, keepdims=True)
    var = jnp.mean(jnp.square(z - mu), axis=-1, keepdims=True)
    return (z - mu) * lax.rsqrt(var + LN_EPS) * g + b


def _ada_kernel(c_ref, w_ref, b_ref, o_ref):
    o_ref[0] = _dot_hi(c_ref[...], w_ref[0]) + b_ref[0]


def _ada(c, w_ada, b_ada):
    depth, d, n = w_ada.shape
    bsz = c.shape[0]
    tn = D_MODEL
    return pl.pallas_call(
        _ada_kernel,
        out_shape=jax.ShapeDtypeStruct((depth, bsz, n), F32),
        grid=(depth, n // tn),
        in_specs=[pl.BlockSpec((bsz, d), lambda l, j: (0, 0)),
                  pl.BlockSpec((1, d, tn), lambda l, j: (l, 0, j)),
                  pl.BlockSpec((1, 1, tn), lambda l, j: (l, 0, j))],
        out_specs=pl.BlockSpec((1, bsz, tn), lambda l, j: (l, 0, j)),
        compiler_params=_params(2),
        name="ada",
    )(c, w_ada, b_ada.reshape(depth, 1, n))


def _proj_kernel(x_ref, ada_ref, wk_ref, wm_ref, wt_ref, wmt_ref,
                 k_ref, kmean_ref, misc_ref, qT_ref, vT_ref, qiT_ref, miscT_ref, *, tm):
    x = x_ref[0]
    sh = ada_ref[0, 0:1, :]
    sc = ada_ref[0, 1:2, :]
    h = (x * (1.0 + sc) + sh).astype(BF16)
    kf = _dot(h, wk_ref[...])
    k_ref[0] = kf.astype(BF16)
    for g in range(tm // MOBA_BLOCK):
        kmean_ref[0, 0, g:g + 1, :] = jnp.mean(kf[g * MOBA_BLOCK:(g + 1) * MOBA_BLOCK], axis=0, keepdims=True)
    misc_ref[0] = _dot(h, wm_ref[...])
    t = lax.dot_general(wt_ref[...], h, NT, preferred_element_type=F32)
    nq = N_MIX * MIX_W
    qT_ref[0] = (t[0:nq] * QSCALE).astype(BF16)
    for g in range(tm // CH):
        vT_ref[0, g] = t[nq:2 * nq, g * CH:(g + 1) * CH].astype(BF16)
    qiT_ref[0] = t[2 * nq:].astype(BF16)
    miscT_ref[0] = lax.dot_general(wmt_ref[...], h, NT, preferred_element_type=F32)


def _proj(x, ada, w_in, tm=512):
    bsz, seq, d = x.shape
    nq = N_MIX * MIX_W
    nqi = IDX_HEADS * IDX_DIM
    w = w_in.astype(BF16)
    wk = w[:, nq:2 * nq]
    wm = jnp.pad(w[:, 3 * nq + nqi:], ((0, 0), (0, MISC_W - (IN_COLS - 3 * nq - nqi))))
    wt = jnp.concatenate([w[:, 0:nq], w[:, 2 * nq:3 * nq], w[:, 3 * nq:3 * nq + nqi]], axis=1).T
    wmt = wm.T
    nt = seq // tm
    full = lambda b, i: (0, 0)
    outs = pl.pallas_call(
        functools.partial(_proj_kernel, tm=tm),
        out_shape=(jax.ShapeDtypeStruct((bsz, seq, nq), BF16),
                   jax.ShapeDtypeStruct((bsz, nt, tm // MOBA_BLOCK, nq), F32),
                   jax.ShapeDtypeStruct((bsz, seq, MISC_W), F32),
                   jax.ShapeDtypeStruct((bsz, nq, seq), BF16),
                   jax.ShapeDtypeStruct((bsz, seq // CH, nq, CH), BF16),
                   jax.ShapeDtypeStruct((bsz, nqi, seq), BF16),
                   jax.ShapeDtypeStruct((bsz, MISC_W, seq), F32)),
        grid=(bsz, nt),
        in_specs=[pl.BlockSpec((1, tm, d), lambda b, i: (b, i, 0)),
                  pl.BlockSpec((1, 6, d), lambda b, i: (b, 0, 0)),
                  pl.BlockSpec(wk.shape, full),
                  pl.BlockSpec(wm.shape, full),
                  pl.BlockSpec(wt.shape, full),
                  pl.BlockSpec(wmt.shape, full)],
        out_specs=(pl.BlockSpec((1, tm, nq), lambda b, i: (b, i, 0)),
                   pl.BlockSpec((1, 1, tm // MOBA_BLOCK, nq), lambda b, i: (b, i, 0, 0)),
                   pl.BlockSpec((1, tm, MISC_W), lambda b, i: (b, i, 0)),
                   pl.BlockSpec((1, nq, tm), lambda b, i: (b, 0, i)),
                   pl.BlockSpec((1, tm // CH, nq, CH), lambda b, i: (b, i, 0, 0)),
                   pl.BlockSpec((1, nqi, tm), lambda b, i: (b, 0, i)),
                   pl.BlockSpec((1, MISC_W, tm), lambda b, i: (b, 0, i))),
        compiler_params=_params(2),
        name="proj",
    )(x, ada, wk, wm, wt, wmt)
    k, kmean, misc, qT, vT, qiT, miscT = outs
    return k, kmean.reshape(bsz, seq // MOBA_BLOCK, nq), misc, qT, vT, qiT, miscT


def _mask_heads(qT_ref, qm_ref):
    q = qT_ref[0]
    rowh = lax.broadcasted_iota(jnp.int32, q.shape, 0) // HEAD_DIM
    for h in range(HEADS):
        qm_ref[h] = jnp.where(rowh == h, q, jnp.zeros_like(q))


def _init_state(m_ref, acc_ref):
    m_ref[...] = jnp.full(m_ref.shape, NEG, F32)
    acc_ref[...] = jnp.zeros(acc_ref.shape, F32)


def _attention(i, first, k_ref, vT_ref, st, logits_fn, chunk_ctx=None, n_diag=None):
    qm_ref, m_ref, acc_ref, s_ref, mx_ref, p_ref, alpha_ref = st
    if n_diag is None:
        n_chunks = i - first + 1
    else:
        n_chunks = n_diag + (n_diag * i - first)
    ones = jnp.ones((ACC_ROWS - HEAD_DIM, CH), BF16)

    def chunk_of(n):
        n = jnp.clip(n, 0, n_chunks - 1)
        if n_diag is None:
            return first + n
        return jnp.where(n < n_diag, n_diag * i + n, first + n - n_diag)

    def position(n):
        if n_diag is None:
            return None
        return n if n < n_diag else -1

    def logits_head(h, slot, c, kc, ctx, diag, pen):
        x = logits_fn(h, c, _dot(kc, qm_ref[h]), ctx, diag, pen)
        s_ref[slot, h] = x
        mx_ref[slot, h] = jnp.max(x, axis=0, keepdims=True)

    def softmax_head(h, slot):
        m_old = m_ref[h]
        m_new = jnp.maximum(m_old, mx_ref[slot, h])
        alpha_ref[slot, h] = jnp.exp2(m_old - m_new)
        p_ref[slot, h] = jnp.exp2(s_ref[slot, h] - m_new).astype(BF16)
        m_ref[h] = m_new

    def pv_head(h, slot, vc):
        v1 = jnp.concatenate([vc[h * HEAD_DIM:(h + 1) * HEAD_DIM, :], ones], axis=0)
        acc_ref[h] = alpha_ref[slot, h] * acc_ref[h] + _dot(v1, p_ref[slot, h])

    def step(n, a, diag):
        c = chunk_of(n + 1)
        pen = jnp.where(n + 1 < n_chunks, 0.0, NEG)
        kc = k_ref[0, pl.ds(pl.multiple_of(c * CH, CH), CH), :]
        vc = vT_ref[0, chunk_of(n - 1)]
        ctx = chunk_ctx(c) if chunk_ctx is not None else None
        for h in range(HEADS):
            softmax_head(h, a)
        for h in range(HEADS):
            pv_head(h, 1 - a, vc)
        for h in range(HEADS):
            logits_head(h, 1 - a, c, kc, ctx, diag, pen)

    p_ref[1] = jnp.zeros(p_ref.shape[1:], BF16)
    alpha_ref[1] = jnp.ones(alpha_ref.shape[1:], F32)
    c0 = chunk_of(jnp.int32(0))
    k0 = k_ref[0, pl.ds(pl.multiple_of(c0 * CH, CH), CH), :]
    ctx0 = chunk_ctx(c0) if chunk_ctx is not None else None
    for h in range(HEADS):
        logits_head(h, 0, c0, k0, ctx0, position(0), 0.0)

    first_trip = 0
    if n_diag is not None and n_diag > 1:
        step(0, 0, position(1))
        step(1, 1, position(2))
        first_trip = 1
    later = position(2 * first_trip + 1)

    def pair(t, carry):
        step(2 * t, 0, later)
        step(2 * t + 1, 1, later)
        return carry

    n_trips = (n_chunks + 1) // 2
    lax.fori_loop(first_trip, n_trips, pair, 0)
    v_last = vT_ref[0, chunk_of(2 * n_trips - 1)]
    for h in range(HEADS):
        pv_head(h, 1, v_last)


def _attention_fixed_shift(i, first, k_ref, vT_ref, st, logits_fn, chunk_ctx=None):
    qm_ref, m_ref, acc_ref, _, _, p_ref, _ = st
    n_units = i - first + 1
    ones = jnp.ones((ACC_ROWS - HEAD_DIM, CH), BF16)

    def chunk_of(u):
        if chunk_ctx is None:
            return jnp.where(u == 0, i, jnp.minimum(first + u - 1, i))
        return jnp.minimum(first + u, i)

    def produce(u, slot):
        c = chunk_of(u)
        pen = jnp.where(u < n_units, 0.0, NEG)
        kc = k_ref[0, pl.ds(pl.multiple_of(c * CH, CH), CH), :]
        ctx = chunk_ctx(c) if chunk_ctx is not None else None
        for h in range(HEADS):
            if chunk_ctx is None:
                x = logits_fn(h, c, _dot(kc, qm_ref[h]), m_ref[h], -1, pen)
            else:
                x = logits_fn(h, c, _dot(kc, qm_ref[h]), ctx, None, pen)
            p_ref[slot, h] = jnp.exp2(x).astype(BF16)

    def consume(u, slot):
        vc = vT_ref[0, chunk_of(u)]
        for h in range(HEADS):
            v1 = jnp.concatenate([vc[h * HEAD_DIM:(h + 1) * HEAD_DIM, :], ones], axis=0)
            acc_ref[h] = acc_ref[h] + _dot(v1, p_ref[slot, h])

    if chunk_ctx is None:
        c0 = jnp.asarray(i, jnp.int32)
        k0 = k_ref[0, pl.ds(pl.multiple_of(c0 * CH, CH), CH), :]
        for h in range(HEADS):
            x = logits_fn(h, c0, _dot(k0, qm_ref[h]), None, 0, 0.0)
            shift = jnp.max(x, axis=0, keepdims=True) + SHIFT_MARGIN
            m_ref[h] = shift
            p_ref[0, h] = jnp.exp2(x - shift).astype(BF16)
    else:
        produce(jnp.int32(0), 0)
    produce(jnp.int32(1), 1)

    def pair(t, carry):
        consume(2 * t, 0)
        produce(2 * t + 2, 0)
        consume(2 * t + 1, 1)
        produce(2 * t + 3, 1)
        return carry

    lax.fori_loop(0, (n_units + 1) // 2, pair, 0)


def _key_norms(k_ref, ksq_ref, n_chunks):
    row = lax.broadcasted_iota(jnp.int32, (MIX_W, LANES), 0) // HEAD_DIM
    ind = jnp.where(row == lax.broadcasted_iota(jnp.int32, (MIX_W, LANES), 1), 1.0, 0.0).astype(BF16)

    def body(c, best):
        kc = k_ref[0, pl.ds(pl.multiple_of(c * CH, CH), CH), :].astype(F32)
        sq = (kc * kc * (1.0 + 2.0 ** -7)).astype(BF16)
        return jnp.maximum(best, jnp.max(_dot(sq, ind), axis=0, keepdims=True))

    ksq_ref[...] = lax.fori_loop(0, n_chunks, body, jnp.zeros((1, LANES), F32))


def _small_products(qm_ref, ksq_ref):
    lane = lax.broadcasted_iota(jnp.int32, (1, LANES), 1)
    worst = jnp.float32(0.0)
    for h in range(HEADS):
        q = qm_ref[h].astype(F32)
        qsq = jnp.max(jnp.sum(q * q, axis=0, keepdims=True))
        ksq = jnp.max(jnp.where(lane == h, ksq_ref[...], 0.0))
        worst = jnp.maximum(worst, qsq * ksq)
    return worst * 1.01 <= (SHIFT_MARGIN / 2.0) ** 2


def _self_attending(i, first, k_ref, vT_ref, st, logits_fn, ksq_ref):
    assert TQ_WIDE == CH
    small = _small_products(st[0], ksq_ref)

    @pl.when(small)
    def _fixed():
        _attention_fixed_shift(i, first, k_ref, vT_ref, st, logits_fn)

    @pl.when(jnp.logical_not(small))
    def _running():
        _attention(i, first, k_ref, vT_ref, st, logits_fn, n_diag=1)


def _finish(o_ref, acc_ref):
    parts = [acc_ref[h, 0:HEAD_DIM, :] / acc_ref[h, HEAD_DIM:HEAD_DIM + 1, :] for h in range(HEADS)]
    o_ref[0] = jnp.concatenate(parts, axis=0).T.astype(BF16)


def _lanes(tile, tq):
    return tile if tq == LANES else jnp.concatenate([tile] * (tq // LANES), axis=1)


def _causal_neg(tq, d=0):
    s_i = lax.broadcasted_iota(jnp.int32, (CH, tq), 0) + d * CH
    t_i = lax.broadcasted_iota(jnp.int32, (CH, tq), 1)
    return jnp.where(s_i <= t_i, 0.0, NEG).astype(F32)


def _key_pos_bias(bias_ref, slopes):
    s_i = lax.broadcasted_iota(jnp.int32, (CH, LANES), 0).astype(F32)
    for h in range(HEADS):
        bias_ref[h] = s_i * float(slopes[h] * LOG2E)


def _attn_specs(mixer, seq, tq=CH):
    return [pl.BlockSpec((1, MIX_W, tq), lambda b, i: (b, mixer, i)),
            pl.BlockSpec((1, seq, MIX_W), lambda b, i: (b, 0, mixer)),
            pl.BlockSpec((1, seq // CH, MIX_W, CH), lambda b, i: (b, 0, mixer, 0))]


def _attn_scratch(tq=CH):
    return [pltpu.VMEM((HEADS, MIX_W, tq), BF16),
            pltpu.VMEM((HEADS, 1, tq), F32),
            pltpu.VMEM((HEADS, ACC_ROWS, tq), F32),
            pltpu.VMEM((2, HEADS, CH, tq), F32),
            pltpu.VMEM((2, HEADS, 1, tq), F32),
            pltpu.VMEM((2, HEADS, CH, tq), BF16),
            pltpu.VMEM((2, HEADS, 1, tq), F32)]


def _fox_kernel(qT_ref, k_ref, vT_ref, misc_ref, bf_ref, o_ref,
                cum_ref, ksq_ref, *st, n_chunks):
    qm_ref, m_ref, acc_ref = st[:3]
    i = pl.program_id(1)

    @pl.when(i == 0)
    def _cumulative_gates():
        _key_norms(k_ref, ksq_ref, n_chunks)
        rr = lax.broadcasted_iota(jnp.int32, (CH, CH), 0)
        cc = lax.broadcasted_iota(jnp.int32, (CH, CH), 1)
        tri = jnp.where(cc <= rr, 1.0, 0.0).astype(BF16)

        def body(blk, carry):
            off = pl.multiple_of(blk * CH, CH)
            z = misc_ref[0, pl.ds(off, CH), :] + bf_ref[...]
            ls = jnp.minimum(z, 0.0) - jnp.log1p(jnp.exp(-jnp.abs(z)))
            csum = sum(_dot(tri, part) for part in _split3(ls))
            new = []
            for h in range(HEADS):
                col = csum[:, FL_ROW + h:FL_ROW + h + 1]
                cum = jnp.broadcast_to(col, (CH, LANES)) + carry[h]
                cum_ref[h, pl.ds(off, CH), :] = cum * LOG2E
                new.append(cum[CH - 1:CH, :])
            return tuple(new)

        lax.fori_loop(0, n_chunks, body, tuple(jnp.zeros((1, LANES), F32) for _ in range(HEADS)))

    _mask_heads(qT_ref, qm_ref)
    _init_state(m_ref, acc_ref)

    def logits(h, c, qk, shift, diag, pen):
        off = pl.multiple_of(c * CH, CH)
        cum = cum_ref[h, pl.ds(off, CH), :]
        if diag >= 0:
            return qk - _lanes(cum, TQ_WIDE) + _causal_neg(TQ_WIDE, diag)
        x = qk - _lanes(cum - pen, TQ_WIDE)
        return x if shift is None else x - shift

    _self_attending(i, 0, k_ref, vT_ref, st, logits, ksq_ref)
    _finish(o_ref, acc_ref)


def _fox(qT, k, vT, misc, b_forget_l):
    bsz, seq, _ = k.shape
    n_chunks = seq // CH
    bf = jnp.zeros((1, MISC_W), F32).at[0, FL_ROW:FL_ROW + HEADS].set(b_forget_l)
    return pl.pallas_call(
        functools.partial(_fox_kernel, n_chunks=n_chunks),
        out_shape=jax.ShapeDtypeStruct((bsz, seq, MIX_W), BF16),
        grid=(bsz, seq // TQ_WIDE),
        in_specs=_attn_specs(3, seq, TQ_WIDE) + [
            pl.BlockSpec((1, seq, MISC_W), lambda b, i: (b, 0, 0)),
            pl.BlockSpec((1, MISC_W), lambda b, i: (0, 0))],
        out_specs=pl.BlockSpec((1, TQ_WIDE, MIX_W), lambda b, i: (b, i, 0)),
        scratch_shapes=[pltpu.VMEM((HEADS, seq, LANES), F32),
                        pltpu.VMEM((1, LANES), F32)] + _attn_scratch(TQ_WIDE),
        compiler_params=_params(2),
        name="fox",
    )(qT, k, vT, misc, bf)


def _moba_kernel(qT_ref, k_ref, vT_ref, kmean_ref, o_ref,
                 bias_ref, rowadd_ref, ksq_ref, *st, slopes, n_blocks):
    qm_ref, m_ref, acc_ref = st[:3]
    i = pl.program_id(1)
    tq = TQ_WIDE
    n_diag = tq // CH
    _mask_heads(qT_ref, qm_ref)
    _init_state(m_ref, acc_ref)

    @pl.when(i == 0)
    def _norms():
        _key_norms(k_ref, ksq_ref, n_blocks * MOBA_BLOCK // CH)

    @pl.when(jnp.logical_and(pl.program_id(0) == 0, i == 0))
    def _bias_table():
        s_f = lax.broadcasted_iota(jnp.int32, (CH, tq), 0).astype(F32)
        for h in range(HEADS):
            pos = s_f * float(slopes[h] * LOG2E)
            bias_ref[0, h] = pos
            for d in range(n_diag):
                bias_ref[1 + d, h] = pos + _causal_neg(tq, d)

    n_i = lax.broadcasted_iota(jnp.int32, (n_blocks, tq), 0)
    n_f = n_i.astype(F32)
    own = n_diag * i + lax.broadcasted_iota(jnp.int32, (n_blocks, tq), 1) // MOBA_BLOCK
    past = n_i < own
    kmean_parts = _split3(kmean_ref[0])
    for h in range(HEADS):
        gate = sum(_dot(part, qm_ref[h]) for part in kmean_parts)
        gate = jnp.where(past, gate, -jnp.inf)
        chosen = jnp.zeros((n_blocks, tq), F32)
        for _ in range(MOBA_TOPK):
            mx = jnp.max(gate, axis=0, keepdims=True)
            first = jnp.min(jnp.where(gate == mx, n_f, float(n_blocks)), axis=0, keepdims=True)
            pick = n_f == first
            chosen = jnp.where(pick, 1.0, chosen)
            gate = jnp.where(pick, -jnp.inf, gate)
        chosen = jnp.where(past, chosen, 0.0)
        blk_shift = n_f * float(slopes[h] * LOG2E * MOBA_BLOCK)
        rowadd_ref[h] = jnp.where(jnp.logical_or(chosen > 0.5, n_i == own), blk_shift, NEG)

    def logits(h, c, qk, shift, diag, pen):
        row = rowadd_ref[h, pl.ds(c, 1), :]
        if diag >= 0:
            return qk + bias_ref[1 + diag, h] + row
        row = row + pen
        return qk + bias_ref[0, h] + (row if shift is None else row - shift)

    _self_attending(i, 0, k_ref, vT_ref, st, logits, ksq_ref)
    _finish(o_ref, acc_ref)


def _moba(qT, k, vT, kmean, slopes):
    bsz, seq, _ = k.shape
    n_blocks = seq // MOBA_BLOCK
    return pl.pallas_call(
        functools.partial(_moba_kernel, slopes=tuple(float(s) for s in slopes), n_blocks=n_blocks),
        out_shape=jax.ShapeDtypeStruct((bsz, seq, MIX_W), BF16),
        grid=(bsz, seq // TQ_WIDE),
        in_specs=_attn_specs(2, seq, TQ_WIDE) + [
            pl.BlockSpec((1, n_blocks, MIX_W), lambda b, i: (b, 0, 2))],
        out_specs=pl.BlockSpec((1, TQ_WIDE, MIX_W), lambda b, i: (b, i, 0)),
        scratch_shapes=[pltpu.VMEM((1 + TQ_WIDE // CH, HEADS, CH, TQ_WIDE), F32),
                        pltpu.VMEM((HEADS, n_blocks, TQ_WIDE), F32),
                        pltpu.VMEM((1, LANES), F32)] + _attn_scratch(TQ_WIDE),
        compiler_params=_params(2),
        name="moba",
    )(qT, k, vT, kmean)


DIL_SPAN = 2048 // CH + 1


def _dilated_kernel(qT_ref, k_ref, vT_ref, o_ref, table_ref, ksq_ref, *st, slopes):
    qm_ref, m_ref, acc_ref = st[:3]
    b = pl.program_id(0)
    i = pl.program_id(1)

    tq = TQ_WIDE
    n_diag = tq // CH
    n_entries = n_diag - 1 + DIL_SPAN

    @pl.when(jnp.logical_and(b == 0, i == 0))
    def _bias_table():
        s_i = lax.broadcasted_iota(jnp.int32, (CH, tq), 0)
        t_i = lax.broadcasted_iota(jnp.int32, (CH, tq), 1)
        for e in range(n_entries):
            j = e - (n_diag - 1)
            d = t_i - s_i + j * CH
            ok = d >= 0
            mult = (jnp.where(jnp.logical_and(ok, d <= 128), 1.0, 0.0)
                    + jnp.where(jnp.logical_and(ok, jnp.logical_and(d <= 512, (d & 3) == 0)), 1.0, 0.0)
                    + jnp.where(jnp.logical_and(ok, jnp.logical_and(d <= 2048, (d & 15) == 0)), 1.0, 0.0))
            logm = jnp.where(mult > 0.5, jnp.log2(jnp.maximum(mult, 1.0)), NEG)
            df = d.astype(F32)
            for h in range(HEADS):
                table_ref[h, e] = logm - df * float(slopes[h] * LOG2E)
        for h in range(HEADS):
            table_ref[h, n_entries] = jnp.full((CH, tq), NEG, F32)

    _mask_heads(qT_ref, qm_ref)
    _init_state(m_ref, acc_ref)

    def logits(h, c, qk, shift, diag, pen):
        if diag >= 0:
            return qk + table_ref[h, n_diag - 1 - diag]
        x = qk + table_ref[h, jnp.where(pen < 0.0, n_entries, n_diag * i - c + (n_diag - 1))]
        return x if shift is None else x - shift

    @pl.when(i == 0)
    def _norms():
        _key_norms(k_ref, ksq_ref, k_ref.shape[1] // CH)

    _self_attending(i, jnp.maximum(n_diag * i - (DIL_SPAN - 1), 0), k_ref, vT_ref, st, logits, ksq_ref)
    _finish(o_ref, acc_ref)


def _dilated(qT, k, vT, slopes):
    bsz, seq, _ = k.shape
    return pl.pallas_call(
        functools.partial(_dilated_kernel, slopes=tuple(float(s) for s in slopes)),
        out_shape=jax.ShapeDtypeStruct((bsz, seq, MIX_W), BF16),
        grid=(bsz, seq // TQ_WIDE),
        in_specs=_attn_specs(1, seq, TQ_WIDE),
        out_specs=pl.BlockSpec((1, TQ_WIDE, MIX_W), lambda b, i: (b, i, 0)),
        scratch_shapes=[pltpu.VMEM((HEADS, TQ_WIDE // CH + DIL_SPAN, CH, TQ_WIDE), F32),
                        pltpu.VMEM((1, LANES), F32)] + _attn_scratch(TQ_WIDE),
        compiler_params=_params(2),
        name="dilated",
    )(qT, k, vT)


def _dsa_kernel(qT_ref, k_ref, vT_ref, qiT_ref, misc_ref, miscT_ref, o_ref,
                bias_ref, qi_ref, key_ref, planes_ref, alive_ref, seen_ref, ksq_ref, *st, slopes):
    qm_ref, m_ref, acc_ref = st[:3]
    i = pl.program_id(1)
    tq = CH

    @pl.when(i == 0)
    def _norms():
        _key_norms(k_ref, ksq_ref, k_ref.shape[1] // CH)

    @pl.when(jnp.logical_and(pl.program_id(0) == 0, i == 0))
    def _clear_planes():
        planes_ref[0:32] = jnp.zeros((32,) + planes_ref.shape[1:], jnp.int32)
        planes_ref[32] = jnp.full(planes_ref.shape[1:], -1, jnp.int32)
    _mask_heads(qT_ref, qm_ref)
    _init_state(m_ref, acc_ref)
    _key_pos_bias(bias_ref, slopes)

    qi_all = qiT_ref[0]
    zpad = jnp.zeros((MISC_W - IDX_DIM, tq), BF16)
    for h in range(IDX_HEADS):
        qi_ref[h] = jnp.concatenate([qi_all[h * IDX_DIM:(h + 1) * IDX_DIM], zpad], axis=0)
    w_rows = miscT_ref[0, WI_ROW:WI_ROW + IDX_HEADS, :] * float(IDX_HEADS ** -0.5 * IDX_DIM ** -0.5)

    s_i = lax.broadcasted_iota(jnp.int32, (CH, tq), 0)
    t_i = lax.broadcasted_iota(jnp.int32, (CH, tq), 1)

    def score_chunk(c, diag):
        off = pl.multiple_of(c * CH, CH)
        ki = misc_ref[0, pl.ds(off, CH), :].astype(BF16)
        score = jnp.zeros((CH, tq), F32)
        for h in range(IDX_HEADS):
            rel = jnp.maximum(_dot(ki, qi_ref[h]), 0.0)
            score = score + rel * w_rows[h:h + 1, :]
        bits = lax.bitcast_convert_type(score, jnp.int32)
        key = bits ^ ((bits >> 31) & 0x7FFFFFFF)
        key = jnp.where(key == -1, 0, key)
        if diag:
            key = jnp.where(s_i <= t_i, key, INT_MIN)
        key_ref[pl.ds(off, CH), :] = key
        ukey = key ^ INT_MIN
        words = _bit_transpose32([ukey[8 * j:8 * j + 8, :] for j in range(32)])
        row0 = pl.multiple_of(c * 8, 8)
        for b in range(32):
            planes_ref[b, pl.ds(row0, 8), :] = words[31 - b]

    score_chunk(i, True)

    def score_pair(t, carry):
        score_chunk(2 * t, False)
        score_chunk(jnp.minimum(2 * t + 1, i - 1), False)
        return carry

    lax.fori_loop(0, (i + 1) // 2, score_pair, 0)
    group = 4 * 8
    n_groups = i // 4 + 1
    alive_ref[...] = jnp.where(lax.broadcasted_iota(jnp.int32, alive_ref.shape, 0) < (i + 1) * 8, -1, 0)

    def narrow(alive, rows, b_prev, took_prev):
        hit = alive & planes_ref[b_prev, rows, :]
        return jnp.where(took_prev, hit, alive ^ hit)

    def bit_step(n, state):
        thr, need, took_prev = state
        b = 31 - n
        took_prev = took_prev != 0

        def sweep(g, cnt):
            rows = pl.ds(pl.multiple_of(g * group, group), group)
            alive = narrow(alive_ref[rows, :], rows, b + 1, took_prev)
            alive_ref[rows, :] = alive
            return cnt + lax.population_count(alive & planes_ref[b, rows, :])

        cnt = lax.fori_loop(0, n_groups, sweep, jnp.zeros((group, tq), jnp.int32))
        cnt = jnp.sum(cnt, axis=0, keepdims=True)
        ok = cnt >= need
        return (jnp.where(ok, thr | jnp.left_shift(jnp.int32(1), b), thr), jnp.where(ok, need, need - cnt),
                jnp.where(ok, 1, 0))

    uthr, n_take, took_last = lax.fori_loop(
        0, 32, bit_step, (jnp.zeros((1, tq), jnp.int32), jnp.full((1, tq), TOPK_KEYS, jnp.int32),
                          jnp.ones((1, tq), jnp.int32)))
    thr = uthr ^ INT_MIN

    def count_alive(g, cnt):
        rows = pl.ds(pl.multiple_of(g * group, group), group)
        return cnt + lax.population_count(narrow(alive_ref[rows, :], rows, 0, took_last != 0))

    n_eq = jnp.sum(lax.fori_loop(0, n_groups, count_alive, jnp.zeros((group, tq), jnp.int32)), axis=0, keepdims=True)
    tied = jnp.logical_and(n_eq > n_take, thr != INT_MIN)
    any_tied = jnp.max(jnp.where(tied, 1.0, 0.0)) > 0.5

    def logits(h, c, qk, neg, diag, pen):
        pos = bias_ref[h] + (c.astype(F32) * float(slopes[h] * LOG2E * CH) + pen)
        return qk + _lanes(pos, tq) + neg

    thr_lo = jnp.where(thr == INT_MIN, INT_MIN + 1, thr)

    def selection(c):
        off = pl.multiple_of(c * CH, CH)
        return jnp.where(key_ref[pl.ds(off, CH), :] >= thr_lo, 0.0, NEG)

    small = _small_products(qm_ref, ksq_ref)

    @pl.when(jnp.logical_and(jnp.logical_not(any_tied), small))
    def _no_ties_fixed():
        s_f = s_i.astype(F32)

        def last_selected(c, best):
            off = pl.multiple_of(c * CH, CH)
            hit = jnp.where(key_ref[pl.ds(off, CH), :] >= thr_lo, s_f, NEG)
            return jnp.maximum(best, jnp.max(hit, axis=0, keepdims=True) + (c * CH).astype(F32))

        anchor = lax.fori_loop(0, i + 1, last_selected, jnp.full((1, tq), NEG, F32))
        for h in range(HEADS):
            m_ref[h] = anchor * float(slopes[h] * LOG2E) + SHIFT_MARGIN / 2.0

        def logits_fixed(h, c, qk, neg, diag, pen):
            pos = bias_ref[h] + (c.astype(F32) * float(slopes[h] * LOG2E * CH) + pen)
            return qk + (_lanes(pos, tq) - m_ref[h]) + neg

        _attention_fixed_shift(i, 0, k_ref, vT_ref, st, logits_fixed, selection)

    @pl.when(jnp.logical_and(jnp.logical_not(any_tied), jnp.logical_not(small)))
    def _no_ties():
        _attention(i, 0, k_ref, vT_ref, st, logits, selection)

    @pl.when(any_tied)
    def _ties():
        n_first = jnp.where(thr == INT_MIN, 0, n_take).astype(F32)
        rr = lax.broadcasted_iota(jnp.int32, (CH, CH), 0)
        cc = lax.broadcasted_iota(jnp.int32, (CH, CH), 1)
        tri = jnp.where(cc <= rr, 1.0, 0.0).astype(BF16)
        seen_ref[...] = jnp.zeros(seen_ref.shape, F32)

        def selection(c):
            off = pl.multiple_of(c * CH, CH)
            key = key_ref[pl.ds(off, CH), :]
            eq = key == thr
            rank = _dot(tri, jnp.where(eq, 1.0, 0.0).astype(BF16)) + seen_ref[...]
            seen_ref[...] = rank[CH - 1:CH, :]
            take = jnp.logical_or(key > thr, jnp.logical_and(eq, rank <= n_first))
            return jnp.where(take, 0.0, NEG)

        _attention(i, 0, k_ref, vT_ref, st, logits, selection)

    _finish(o_ref, acc_ref)


def _dsa(qT, k, vT, qiT, misc, miscT, slopes):
    bsz, seq, _ = k.shape
    nqi = IDX_HEADS * IDX_DIM
    plane_rows = -(-(seq // CH) // 4) * 32
    return pl.pallas_call(
        functools.partial(_dsa_kernel, slopes=tuple(float(s) for s in slopes)),
        out_shape=jax.ShapeDtypeStruct((bsz, seq, MIX_W), BF16),
        grid=(bsz, seq // CH),
        in_specs=_attn_specs(0, seq) + [
            pl.BlockSpec((1, nqi, CH), lambda b, i: (b, 0, i)),
            pl.BlockSpec((1, seq, MISC_W), lambda b, i: (b, 0, 0)),
            pl.BlockSpec((1, MISC_W, CH), lambda b, i: (b, 0, i))],
        out_specs=pl.BlockSpec((1, CH, MIX_W), lambda b, i: (b, i, 0)),
        scratch_shapes=[pltpu.VMEM((HEADS, CH, LANES), F32),
                        pltpu.VMEM((IDX_HEADS, MISC_W, CH), BF16),
                        pltpu.VMEM((seq, CH), jnp.int32),
                        pltpu.VMEM((33, plane_rows, CH), jnp.int32),
                        pltpu.VMEM((plane_rows, CH), jnp.int32),
                        pltpu.VMEM((1, CH), F32),
                        pltpu.VMEM((1, LANES), F32)] + _attn_scratch(),
        compiler_params=_params(2),
        name="dsa",
    )(qT, k, vT, qiT, misc, miscT)


def _merge_kernel(x_ref, ada_ref, oa_ref, ob_ref, oc_ref, od_ref, wmg_ref, wbr_ref, wout_ref,
                  g_ref, b_ref, out_ref):
    x = x_ref[0]
    sh = ada_ref[0, 0:1, :]
    sc = ada_ref[0, 1:2, :]
    g1 = ada_ref[0, 2:3, :]
    h = (x * (1.0 + sc) + sh).astype(BF16)
    mixed = None
    for m, o_ref in enumerate((oa_ref, ob_ref, oc_ref, od_ref)):
        gate = jax.nn.sigmoid(_dot(h, wmg_ref[m]))
        term = gate * _dot(o_ref[0], wbr_ref[m])
        mixed = term if mixed is None else mixed + term
    y = _dot(mixed.astype(BF16), wout_ref[...])
    out_ref[0] = _layer_norm(DN_ALPHA * x + g1 * y, g_ref[...], b_ref[...])


def _merge(x, ada, outs, w_mg, w_br, w_out, ln_g, ln_b, tm=512):
    bsz, seq, d = x.shape
    o_spec = pl.BlockSpec((1, tm, MIX_W), lambda b, i: (b, i, 0))
    return pl.pallas_call(
        _merge_kernel,
        out_shape=jax.ShapeDtypeStruct((bsz, seq, d), F32),
        grid=(bsz, seq // tm),
        in_specs=[pl.BlockSpec((1, tm, d), lambda b, i: (b, i, 0)),
                  pl.BlockSpec((1, 6, d), lambda b, i: (b, 0, 0)),
                  o_spec, o_spec, o_spec, o_spec,
                  pl.BlockSpec((N_MIX, d, d), lambda b, i: (0, 0, 0)),
                  pl.BlockSpec((N_MIX, MIX_W, d), lambda b, i: (0, 0, 0)),
                  pl.BlockSpec((d, d), lambda b, i: (0, 0)),
                  pl.BlockSpec((1, d), lambda b, i: (0, 0)),
                  pl.BlockSpec((1, d), lambda b, i: (0, 0))],
        out_specs=pl.BlockSpec((1, tm, d), lambda b, i: (b, i, 0)),
        compiler_params=_params(2),
        name="merge",
    )(x, ada, *outs, w_mg.astype(BF16), w_br.astype(BF16), w_out.astype(BF16),
      ln_g.reshape(1, d), ln_b.reshape(1, d))


def _route(logits_t, bias_col):
    per_group = N_EXPERTS // N_GROUPS
    scores = jax.nn.sigmoid(logits_t)
    biased = scores + bias_col
    s_rows = [scores[e:e + 1] for e in range(N_EXPERTS)]
    b_rows = [biased[e:e + 1] for e in range(N_EXPERTS)]
    best_g = None
    for g in range(N_GROUPS):
        r = b_rows[g * per_group:(g + 1) * per_group]
        gs = None
        for a in range(per_group):
            for b in range(a + 1, per_group):
                pair = r[a] + r[b]
                gs = pair if gs is None else jnp.maximum(gs, pair)
        if best_g is None:
            best_g, best_v = jnp.zeros_like(gs, dtype=jnp.int32), gs
        else:
            better = gs > best_v
            best_g = jnp.where(better, g, best_g)
            best_v = jnp.maximum(best_v, gs)
    masked = [jnp.where(best_g == e // per_group, b_rows[e], -jnp.inf) for e in range(N_EXPERTS)]

    def argmax_first(rows):
        idx, val = jnp.zeros_like(best_g), rows[0]
        for e in range(1, N_EXPERTS):
            better = rows[e] > val
            idx = jnp.where(better, e, idx)
            val = jnp.maximum(val, rows[e])
        return idx

    e1 = argmax_first(masked)
    e2 = argmax_first([jnp.where(e1 == e, -jnp.inf, masked[e]) for e in range(N_EXPERTS)])
    s1 = sum(jnp.where(e1 == e, s_rows[e], 0.0) for e in range(N_EXPERTS))
    s2 = sum(jnp.where(e2 == e, s_rows[e], 0.0) for e in range(N_EXPERTS))
    tot = s1 + s2
    w1, w2 = s1 / tot, s2 / tot
    rows = [jnp.where(e1 == e, w1, 0.0) + jnp.where(e2 == e, w2, 0.0) for e in range(N_EXPERTS)]
    return jnp.concatenate(rows, axis=0), best_g


def _moe_kernel(x_ref, ada_ref, wr_ref, br_ref, wg_ref, wu_ref, wd_ref, g_ref, b_ref, out_ref,
                perm_ref, xs_ref, combs_ref, ys_ref, sub_ref, *, tm, rows):
    g = pl.program_id(1)
    per_group = N_EXPERTS // N_GROUPS

    @pl.when(g == 0)
    def _route_and_sort():
        x = x_ref[...]
        h2 = x * (1.0 + ada_ref[0, 4:5, :]) + ada_ref[0, 3:4, :]
        h_hi = h2.astype(BF16)
        h_lo = (h2 - h_hi.astype(F32)).astype(BF16)
        first = _dot(h_hi, wr_ref[...])
        logits = first[:, 0:LANES] + first[:, LANES:] + _dot(h_lo, wr_ref[:, 0:LANES])
        logits_t = logits.T[0:N_EXPERTS]
        comb_t, best_g = _route(logits_t, br_ref[...])
        member = [jnp.where(best_g == g, 1.0, 0.0) for g in range(N_GROUPS)]
        grp = jnp.concatenate(member + [jnp.zeros((8 - N_GROUPS, tm), F32)], axis=0).astype(BF16)
        s_i = lax.broadcasted_iota(jnp.int32, (tm, tm), 0)
        t_i = lax.broadcasted_iota(jnp.int32, (tm, tm), 1)
        rank = _dot(grp, jnp.where(s_i <= t_i, 1.0, 0.0).astype(BF16))
        pos = jnp.zeros((1, tm), F32)
        start = jnp.zeros((1, 1), F32)
        for g in range(N_GROUPS):
            cap = jnp.ceil(rank[g:g + 1, tm - 1:tm] * (1.0 / MOE_SUB)) * MOE_SUB
            pos = pos + member[g] * (start + rank[g:g + 1] - 1.0)
            sub_ref[2 * g] = (jnp.sum(start) * (1.0 / MOE_SUB)).astype(jnp.int32)
            sub_ref[2 * g + 1] = (jnp.sum(cap) * (1.0 / MOE_SUB)).astype(jnp.int32)
            start = start + cap
        r_f = lax.broadcasted_iota(jnp.int32, (rows, tm), 0).astype(F32)
        perm = jnp.where(r_f == pos, 1.0, 0.0).astype(BF16)
        perm_ref[...] = perm
        comb = jnp.concatenate([comb_t, jnp.zeros((LANES - N_EXPERTS, tm), F32)], axis=0).T
        c_hi = comb.astype(BF16)
        c_lo = (comb - c_hi.astype(F32)).astype(BF16)
        d = h_hi.shape[1]
        moved = _dot(perm, jnp.concatenate([h_hi, c_hi, c_lo], axis=1))
        xs_ref[...] = moved[:, 0:d].astype(BF16)
        combs_ref[...] = moved[:, d:d + LANES] + moved[:, d + LANES:]
        ys_ref[...] = jnp.zeros(ys_ref.shape, BF16)

    lane = lax.broadcasted_iota(jnp.int32, (MOE_SUB, LANES), 1)

    def sub_tile(j, carry):
        r0 = pl.multiple_of((sub_ref[2 * g] + j) * MOE_SUB, MOE_SUB)
        xj = xs_ref[pl.ds(r0, MOE_SUB), :]
        cw = combs_ref[pl.ds(r0, MOE_SUB), :]
        hids = []
        for q in range(per_group):
            w_e = jnp.sum(jnp.where(lane == g * per_group + q, cw, 0.0), axis=1, keepdims=True)
            hids.append((jax.nn.silu(_dot(xj, wg_ref[0, q])) * _dot(xj, wu_ref[0, q]) * w_e).astype(BF16))
        ys_ref[pl.ds(r0, MOE_SUB), :] = _dot(jnp.concatenate(hids, axis=1), wd_ref[0, 0]).astype(BF16)
        return carry

    lax.fori_loop(0, sub_ref[2 * g + 1], sub_tile, 0)

    @pl.when(g == N_GROUPS - 1)
    def _unsort_and_norm():
        y = lax.dot_general(perm_ref[...], ys_ref[...], (((0,), (0,)), ((), ())),
                            preferred_element_type=F32)
        z = DN_ALPHA * x_ref[...] + ada_ref[0, 5:6, :] * y
        out_ref[...] = _layer_norm(z, g_ref[...], b_ref[...])


def _moe(x, ada_rows, w_router, b_router, w_gate, w_up, w_down, ln_g, ln_b, batch_len, tm=1024):
    n, d = x.shape
    per_b = batch_len // tm
    per_group = N_EXPERTS // N_GROUPS
    rows = tm + N_GROUPS * MOE_SUB
    wr32 = jnp.pad(w_router, ((0, 0), (0, LANES - N_EXPERTS)))
    wr_hi = wr32.astype(BF16)
    wr = jnp.concatenate([wr_hi, (wr32 - wr_hi.astype(F32)).astype(BF16)], axis=1)
    br = b_router.reshape(N_EXPERTS, 1)
    return pl.pallas_call(
        functools.partial(_moe_kernel, tm=tm, rows=rows),
        out_shape=jax.ShapeDtypeStruct((n, d), F32),
        grid=(n // tm, N_GROUPS),
        in_specs=[pl.BlockSpec((tm, d), lambda i, g: (i, 0)),
                  pl.BlockSpec((1, 6, d), lambda i, g: (i // per_b, 0, 0)),
                  pl.BlockSpec((d, 2 * LANES), lambda i, g: (0, 0)),
                  pl.BlockSpec((N_EXPERTS, 1), lambda i, g: (0, 0)),
                  pl.BlockSpec((1, per_group, d, D_EXPERT), lambda i, g: (0, g, 0, 0)),
                  pl.BlockSpec((1, per_group, d, D_EXPERT), lambda i, g: (0, g, 0, 0)),
                  pl.BlockSpec((1, 1, per_group * D_EXPERT, d), lambda i, g: (0, g, 0, 0)),
                  pl.BlockSpec((1, d), lambda i, g: (0, 0)),
                  pl.BlockSpec((1, d), lambda i, g: (0, 0))],
        out_specs=pl.BlockSpec((tm, d), lambda i, g: (i, 0)),
        scratch_shapes=[pltpu.VMEM((rows, tm), BF16),
                        pltpu.VMEM((rows, d), BF16),
                        pltpu.VMEM((rows, LANES), F32),
                        pltpu.VMEM((rows, d), BF16),
                        pltpu.SMEM((2 * N_GROUPS,), jnp.int32)],
        compiler_params=_params(2),
        name="moe",
    )(x, ada_rows, wr, br, w_gate[None].astype(BF16), w_up[None].astype(BF16),
      w_down.reshape(1, N_GROUPS, per_group * D_EXPERT, d).astype(BF16),
      ln_g.reshape(1, d), ln_b.reshape(1, d))


def kernel(x, c, w_in, b_forget, w_branch, w_merge_gate, w_out, w_ada, b_ada, ln1_g, ln1_b, ln2_g, ln2_b,
           w_router, b_router, w_exp_gate, w_exp_up, w_exp_down):
    bsz, seq, d = x.shape
    slopes = _alibi_slopes()
    ada_all = _ada(c, w_ada, b_ada).reshape(DEPTH, bsz, 6, d)
    for l in range(DEPTH):
        ada = ada_all[l]
        k, kmean, misc, qT, vT, qiT, miscT = _proj(x, ada, w_in[l])
        o_a = _dsa(qT, k, vT, qiT, misc, miscT, slopes[0])
        o_b = _dilated(qT, k, vT, slopes[1])
        o_c = _moba(qT, k, vT, kmean, slopes[2])
        o_d = _fox(qT, k, vT, misc, b_forget[l])
        x = _merge(x, ada, (o_a, o_b, o_c, o_d), w_merge_gate[l], w_branch[l], w_out[l], ln1_g[l], ln1_b[l])
        x = _moe(x.reshape(bsz * seq, d), ada, w_router, b_router, w_exp_gate[l], w_exp_up[l], w_exp_down[l],
                 ln2_g[l], ln2_b[l], seq).reshape(bsz, seq, d)
    return x
```

```python
import functools

import numpy as np
import jax
import jax.numpy as jnp
from jax import lax
from jax.experimental import pallas as pl
from jax.experimental.pallas import tpu as pltpu

D_MODEL = 1024
HEAD_DIM = 64
HEADS = 4
MIX_W = HEADS * HEAD_DIM
N_MIX = 4
IDX_HEADS = 8
IDX_DIM = 64
TOPK_KEYS = 256
MOBA_BLOCK = 256
MOBA_TOPK = 3
N_EXPERTS = 16
N_GROUPS = 4
D_EXPERT = 512
DEPTH = 2
DN_ALPHA = (2 * DEPTH) ** 0.25
LN_EPS = 1e-5
IN_COLS = 3 * N_MIX * MIX_W + IDX_HEADS * IDX_DIM + IDX_DIM + IDX_HEADS + HEADS

CH = 256
ACC_ROWS = HEAD_DIM + 16
MOE_SUB = 128
TQ_WIDE = CH
SHIFT_MARGIN = 96.0
LANES = 128
MISC_W = 128
WI_ROW = IDX_DIM
FL_ROW = IDX_DIM + IDX_HEADS
LOG2E = 1.4426950408889634
NEG = -1e30
QSCALE = HEAD_DIM ** -0.5 * LOG2E
INT_MIN = -(2 ** 31)
VMEM_LIMIT = 60 * 1024 * 1024

F32 = jnp.float32
BF16 = jnp.bfloat16
HI = lax.Precision.HIGHEST
NT = (((1,), (1,)), ((), ()))


def _alibi_slopes():
    n = 3 * HEADS
    s = 2.0 ** (-8.0 * np.arange(1, n + 1) / n)
    return s.reshape(HEADS, 3).T


def _dot(a, b):
    return jnp.dot(a, b, preferred_element_type=F32)


def _dot_hi(a, b):
    return jnp.dot(a, b, preferred_element_type=F32, precision=HI)


def _bit_transpose32(words):
    a = list(words)
    j, m = 16, 0x0000FFFF
    while j:
        k = 0
        while k < 32:
            t = (a[k] ^ (a[k + j] >> j)) & m
            a[k] = a[k] ^ t
            a[k + j] = a[k + j] ^ (t << j)
            k = (k + j + 1) & ~j
        j >>= 1
        m = (m ^ (m << j)) & 0xFFFFFFFF
    return a


def _split3(x):
    x1 = x.astype(BF16)
    r1 = x - x1.astype(F32)
    x2 = r1.astype(BF16)
    x3 = (r1 - x2.astype(F32)).astype(BF16)
    return x1, x2, x3


def _params(n_axes):
    return pltpu.CompilerParams(dimension_semantics=("arbitrary",) * n_axes,
                                vmem_limit_bytes=VMEM_LIMIT)


def _layer_norm(z, g, b):
    mu = jnp.mean(z, axis=-1, keepdims=True)
    var = jnp.mean(jnp.square(z - mu), axis=-1, keepdims=True)
    return (z - mu) * lax.rsqrt(var + LN_EPS) * g + b


def _ada_kernel(c_ref, w_ref, b_ref, o_ref):
    o_ref[0] = _dot_hi(c_ref[...], w_ref[0]) + b_ref[0]


def _ada(c, w_ada, b_ada):
    depth, d, n = w_ada.shape
    bsz = c.shape[0]
    tn = D_MODEL
    return pl.pallas_call(
        _ada_kernel,
        out_shape=jax.ShapeDtypeStruct((depth, bsz, n), F32),
        grid=(depth, n // tn),
        in_specs=[pl.BlockSpec((bsz, d), lambda l, j: (0, 0)),
                  pl.BlockSpec((1, d, tn), lambda l, j: (l, 0, j)),
                  pl.BlockSpec((1, 1, tn), lambda l, j: (l, 0, j))],
        out_specs=pl.BlockSpec((1, bsz, tn), lambda l, j: (l, 0, j)),
        compiler_params=_params(2),
        name="ada",
    )(c, w_ada, b_ada.reshape(depth, 1, n))


def _proj_kernel(x_ref, ada_ref, wk_ref, wm_ref, wt_ref, wmt_ref,
                 k_ref, kmean_ref, misc_ref, qT_ref, vT_ref, qiT_ref, miscT_ref, *, tm):
    x = x_ref[0]
    sh = ada_ref[0, 0:1, :]
    sc = ada_ref[0, 1:2, :]
    h = (x * (1.0 + sc) + sh).astype(BF16)
    kf = _dot(h, wk_ref[...])
    k_ref[0] = kf.astype(BF16)
    for g in range(tm // MOBA_BLOCK):
        kmean_ref[0, 0, g:g + 1, :] = jnp.mean(kf[g * MOBA_BLOCK:(g + 1) * MOBA_BLOCK], axis=0, keepdims=True)
    misc_ref[0] = _dot(h, wm_ref[...])
    t = lax.dot_general(wt_ref[...], h, NT, preferred_element_type=F32)
    nq = N_MIX * MIX_W
    qT_ref[0] = (t[0:nq] * QSCALE).astype(BF16)
    for g in range(tm // CH):
        vT_ref[0, g] = t[nq:2 * nq, g * CH:(g + 1) * CH].astype(BF16)
    qiT_ref[0] = t[2 * nq:].astype(BF16)
    miscT_ref[0] = lax.dot_general(wmt_ref[...], h, NT, preferred_element_type=F32)


def _proj(x, ada, w_in, tm=512):
    bsz, seq, d = x.shape
    nq = N_MIX * MIX_W
    nqi = IDX_HEADS * IDX_DIM
    w = w_in.astype(BF16)
    wk = w[:, nq:2 * nq]
    wm = jnp.pad(w[:, 3 * nq + nqi:], ((0, 0), (0, MISC_W - (IN_COLS - 3 * nq - nqi))))
    wt = jnp.concatenate([w[:, 0:nq], w[:, 2 * nq:3 * nq], w[:, 3 * nq:3 * nq + nqi]], axis=1).T
    wmt = wm.T
    nt = seq // tm
    full = lambda b, i: (0, 0)
    outs = pl.pallas_call(
        functools.partial(_proj_kernel, tm=tm),
        out_shape=(jax.ShapeDtypeStruct((bsz, seq, nq), BF16),
                   jax.ShapeDtypeStruct((bsz, nt, tm // MOBA_BLOCK, nq), F32),
                   jax.ShapeDtypeStruct((bsz, seq, MISC_W), F32),
                   jax.ShapeDtypeStruct((bsz, nq, seq), BF16),
                   jax.ShapeDtypeStruct((bsz, seq // CH, nq, CH), BF16),
                   jax.ShapeDtypeStruct((bsz, nqi, seq), BF16),
                   jax.ShapeDtypeStruct((bsz, MISC_W, seq), F32)),
        grid=(bsz, nt),
        in_specs=[pl.BlockSpec((1, tm, d), lambda b, i: (b, i, 0)),
                  pl.BlockSpec((1, 6, d), lambda b, i: (b, 0, 0)),
                  pl.BlockSpec(wk.shape, full),
                  pl.BlockSpec(wm.shape, full),
                  pl.BlockSpec(wt.shape, full),
                  pl.BlockSpec(wmt.shape, full)],
        out_specs=(pl.BlockSpec((1, tm, nq), lambda b, i: (b, i, 0)),
                   pl.BlockSpec((1, 1, tm // MOBA_BLOCK, nq), lambda b, i: (b, i, 0, 0)),
                   pl.BlockSpec((1, tm, MISC_W), lambda b, i: (b, i, 0)),
                   pl.BlockSpec((1, nq, tm), lambda b, i: (b, 0, i)),
                   pl.BlockSpec((1, tm // CH, nq, CH), lambda b, i: (b, i, 0, 0)),
                   pl.BlockSpec((1, nqi, tm), lambda b, i: (b, 0, i)),
                   pl.BlockSpec((1, MISC_W, tm), lambda b, i: (b, 0, i))),
        compiler_params=_params(2),
        name="proj",
    )(x, ada, wk, wm, wt, wmt)
    k, kmean, misc, qT, vT, qiT, miscT = outs
    return k, kmean.reshape(bsz, seq // MOBA_BLOCK, nq), misc, qT, vT, qiT, miscT


def _mask_heads(qT_ref, qm_ref):
    q = qT_ref[0]
    rowh = lax.broadcasted_iota(jnp.int32, q.shape, 0) // HEAD_DIM
    for h in range(HEADS):
        qm_ref[h] = jnp.where(rowh == h, q, jnp.zeros_like(q))


def _init_state(m_ref, acc_ref):
    m_ref[...] = jnp.full(m_ref.shape, NEG, F32)
    acc_ref[...] = jnp.zeros(acc_ref.shape, F32)


def _attention(i, first, k_ref, vT_ref, st, logits_fn, chunk_ctx=None, n_diag=None):
    qm_ref, m_ref, acc_ref, s_ref, mx_ref, p_ref, alpha_ref = st
    if n_diag is None:
        n_chunks = i - first + 1
    else:
        n_chunks = n_diag + (n_diag * i - first)
    ones = jnp.ones((ACC_ROWS - HEAD_DIM, CH), BF16)

    def chunk_of(n):
        n = jnp.clip(n, 0, n_chunks - 1)
        if n_diag is None:
            return first + n
        return jnp.where(n < n_diag, n_diag * i + n, first + n - n_diag)

    def position(n):
        if n_diag is None:
            return None
        return n if n < n_diag else -1

    def logits_head(h, slot, c, kc, ctx, diag, pen):
        x = logits_fn(h, c, _dot(kc, qm_ref[h]), ctx, diag, pen)
        s_ref[slot, h] = x
        mx_ref[slot, h] = jnp.max(x, axis=0, keepdims=True)

    def softmax_head(h, slot):
        m_old = m_ref[h]
        m_new = jnp.maximum(m_old, mx_ref[slot, h])
        alpha_ref[slot, h] = jnp.exp2(m_old - m_new)
        p_ref[slot, h] = jnp.exp2(s_ref[slot, h] - m_new).astype(BF16)
        m_ref[h] = m_new

    def pv_head(h, slot, vc):
        v1 = jnp.concatenate([vc[h * HEAD_DIM:(h + 1) * HEAD_DIM, :], ones], axis=0)
        acc_ref[h] = alpha_ref[slot, h] * acc_ref[h] + _dot(v1, p_ref[slot, h])

    def step(n, a, diag):
        c = chunk_of(n + 1)
        pen = jnp.where(n + 1 < n_chunks, 0.0, NEG)
        kc = k_ref[0, pl.ds(pl.multiple_of(c * CH, CH), CH), :]
        vc = vT_ref[0, chunk_of(n - 1)]
        ctx = chunk_ctx(c) if chunk_ctx is not None else None
        for h in range(HEADS):
            softmax_head(h, a)
        for h in range(HEADS):
            pv_head(h, 1 - a, vc)
        for h in range(HEADS):
            logits_head(h, 1 - a, c, kc, ctx, diag, pen)

    p_ref[1] = jnp.zeros(p_ref.shape[1:], BF16)
    alpha_ref[1] = jnp.ones(alpha_ref.shape[1:], F32)
    c0 = chunk_of(jnp.int32(0))
    k0 = k_ref[0, pl.ds(pl.multiple_of(c0 * CH, CH), CH), :]
    ctx0 = chunk_ctx(c0) if chunk_ctx is not None else None
    for h in range(HEADS):
        logits_head(h, 0, c0, k0, ctx0, position(0), 0.0)

    first_trip = 0
    if n_diag is not None and n_diag > 1:
        step(0, 0, position(1))
        step(1, 1, position(2))
        first_trip = 1
    later = position(2 * first_trip + 1)

    def pair(t, carry):
        step(2 * t, 0, later)
        step(2 * t + 1, 1, later)
        return carry

    n_trips = (n_chunks + 1) // 2
    lax.fori_loop(first_trip, n_trips, pair, 0)
    v_last = vT_ref[0, chunk_of(2 * n_trips - 1)]
    for h in range(HEADS):
        pv_head(h, 1, v_last)


def _attention_fixed_shift(i, first, k_ref, vT_ref, st, logits_fn):
    qm_ref, m_ref, acc_ref, _, _, p_ref, _ = st
    n_past = i - first
    ones = jnp.ones((ACC_ROWS - HEAD_DIM, CH), BF16)

    def chunk_of(u):
        return jnp.where(u == 0, i, jnp.minimum(first + u - 1, i))

    def produce(u, slot):
        c = chunk_of(u)
        pen = jnp.where(u <= n_past, 0.0, NEG)
        kc = k_ref[0, pl.ds(pl.multiple_of(c * CH, CH), CH), :]
        for h in range(HEADS):
            x = logits_fn(h, c, _dot(kc, qm_ref[h]), m_ref[h], -1, pen)
            p_ref[slot, h] = jnp.exp2(x).astype(BF16)

    def consume(u, slot):
        vc = vT_ref[0, chunk_of(u)]
        for h in range(HEADS):
            v1 = jnp.concatenate([vc[h * HEAD_DIM:(h + 1) * HEAD_DIM, :], ones], axis=0)
            acc_ref[h] = acc_ref[h] + _dot(v1, p_ref[slot, h])

    c0 = jnp.asarray(i, jnp.int32)
    k0 = k_ref[0, pl.ds(pl.multiple_of(c0 * CH, CH), CH), :]
    for h in range(HEADS):
        x = logits_fn(h, c0, _dot(k0, qm_ref[h]), None, 0, 0.0)
        shift = jnp.max(x, axis=0, keepdims=True) + SHIFT_MARGIN
        m_ref[h] = shift
        p_ref[0, h] = jnp.exp2(x - shift).astype(BF16)
    produce(jnp.int32(1), 1)

    def pair(t, carry):
        consume(2 * t, 0)
        produce(2 * t + 2, 0)
        consume(2 * t + 1, 1)
        produce(2 * t + 3, 1)
        return carry

    def quad(q, carry):
        pair(2 * q, carry)
        pair(2 * q + 1, carry)
        return carry

    n_quads = (n_past + 1) // 4
    lax.fori_loop(0, n_quads, quad, 0)
    lax.fori_loop(2 * n_quads, (n_past + 2) // 2, pair, 0)


def _key_norms(k_ref, ksq_ref, n_chunks):
    row = lax.broadcasted_iota(jnp.int32, (MIX_W, LANES), 0) // HEAD_DIM
    ind = jnp.where(row == lax.broadcasted_iota(jnp.int32, (MIX_W, LANES), 1), 1.0, 0.0).astype(BF16)

    def body(c, best):
        kc = k_ref[0, pl.ds(pl.multiple_of(c * CH, CH), CH), :].astype(F32)
        sq = (kc * kc * (1.0 + 2.0 ** -7)).astype(BF16)
        return jnp.maximum(best, jnp.max(_dot(sq, ind), axis=0, keepdims=True))

    ksq_ref[...] = lax.fori_loop(0, n_chunks, body, jnp.zeros((1, LANES), F32))


def _small_products(qm_ref, ksq_ref):
    lane = lax.broadcasted_iota(jnp.int32, (1, LANES), 1)
    worst = jnp.float32(0.0)
    for h in range(HEADS):
        q = qm_ref[h].astype(F32)
        qsq = jnp.max(jnp.sum(q * q, axis=0, keepdims=True))
        ksq = jnp.max(jnp.where(lane == h, ksq_ref[...], 0.0))
        worst = jnp.maximum(worst, qsq * ksq)
    return worst * 1.01 <= (SHIFT_MARGIN / 2.0) ** 2


def _self_attending(i, first, k_ref, vT_ref, st, logits_fn, ksq_ref):
    assert TQ_WIDE == CH
    small = _small_products(st[0], ksq_ref)

    @pl.when(small)
    def _fixed():
        _attention_fixed_shift(i, first, k_ref, vT_ref, st, logits_fn)

    @pl.when(jnp.logical_not(small))
    def _running():
        _attention(i, first, k_ref, vT_ref, st, logits_fn, n_diag=1)


def _finish(o_ref, acc_ref):
    parts = [acc_ref[h, 0:HEAD_DIM, :] / acc_ref[h, HEAD_DIM:HEAD_DIM + 1, :] for h in range(HEADS)]
    o_ref[0] = jnp.concatenate(parts, axis=0).T.astype(BF16)


def _lanes(tile, tq):
    return tile if tq == LANES else jnp.concatenate([tile] * (tq // LANES), axis=1)


def _causal_neg(tq, d=0):
    s_i = lax.broadcasted_iota(jnp.int32, (CH, tq), 0) + d * CH
    t_i = lax.broadcasted_iota(jnp.int32, (CH, tq), 1)
    return jnp.where(s_i <= t_i, 0.0, NEG).astype(F32)


def _key_pos_bias(bias_ref, slopes):
    s_i = lax.broadcasted_iota(jnp.int32, (CH, LANES), 0).astype(F32)
    for h in range(HEADS):
        bias_ref[h] = s_i * float(slopes[h] * LOG2E)


def _attn_specs(mixer, seq, tq=CH):
    return [pl.BlockSpec((1, MIX_W, tq), lambda b, i: (b, mixer, i)),
            pl.BlockSpec((1, seq, MIX_W), lambda b, i: (b, 0, mixer)),
            pl.BlockSpec((1, seq // CH, MIX_W, CH), lambda b, i: (b, 0, mixer, 0))]


def _attn_scratch(tq=CH):
    return [pltpu.VMEM((HEADS, MIX_W, tq), BF16),
            pltpu.VMEM((HEADS, 1, tq), F32),
            pltpu.VMEM((HEADS, ACC_ROWS, tq), F32),
            pltpu.VMEM((2, HEADS, CH, tq), F32),
            pltpu.VMEM((2, HEADS, 1, tq), F32),
            pltpu.VMEM((2, HEADS, CH, tq), BF16),
            pltpu.VMEM((2, HEADS, 1, tq), F32)]


def _fox_kernel(qT_ref, k_ref, vT_ref, misc_ref, bf_ref, o_ref,
                cum_ref, ksq_ref, *st, n_chunks):
    qm_ref, m_ref, acc_ref = st[:3]
    i = pl.program_id(1)

    @pl.when(i == 0)
    def _cumulative_gates():
        _key_norms(k_ref, ksq_ref, n_chunks)
        rr = lax.broadcasted_iota(jnp.int32, (CH, CH), 0)
        cc = lax.broadcasted_iota(jnp.int32, (CH, CH), 1)
        tri = jnp.where(cc <= rr, 1.0, 0.0).astype(BF16)

        def body(blk, carry):
            off = pl.multiple_of(blk * CH, CH)
            z = misc_ref[0, pl.ds(off, CH), :] + bf_ref[...]
            ls = jnp.minimum(z, 0.0) - jnp.log1p(jnp.exp(-jnp.abs(z)))
            csum = sum(_dot(tri, part) for part in _split3(ls))
            new = []
            for h in range(HEADS):
                col = csum[:, FL_ROW + h:FL_ROW + h + 1]
                cum = jnp.broadcast_to(col, (CH, LANES)) + carry[h]
                cum_ref[h, pl.ds(off, CH), :] = cum * LOG2E
                new.append(cum[CH - 1:CH, :])
            return tuple(new)

        lax.fori_loop(0, n_chunks, body, tuple(jnp.zeros((1, LANES), F32) for _ in range(HEADS)))

    _mask_heads(qT_ref, qm_ref)
    _init_state(m_ref, acc_ref)

    def logits(h, c, qk, shift, diag, pen):
        off = pl.multiple_of(c * CH, CH)
        cum = cum_ref[h, pl.ds(off, CH), :]
        if diag >= 0:
            return qk - _lanes(cum, TQ_WIDE) + _causal_neg(TQ_WIDE, diag)
        x = qk - _lanes(cum - pen, TQ_WIDE)
        return x if shift is None else x - shift

    _self_attending(i, 0, k_ref, vT_ref, st, logits, ksq_ref)
    _finish(o_ref, acc_ref)


def _fox(qT, k, vT, misc, b_forget_l):
    bsz, seq, _ = k.shape
    n_chunks = seq // CH
    bf = jnp.zeros((1, MISC_W), F32).at[0, FL_ROW:FL_ROW + HEADS].set(b_forget_l)
    return pl.pallas_call(
        functools.partial(_fox_kernel, n_chunks=n_chunks),
        out_shape=jax.ShapeDtypeStruct((bsz, seq, MIX_W), BF16),
        grid=(bsz, seq // TQ_WIDE),
        in_specs=_attn_specs(3, seq, TQ_WIDE) + [
            pl.BlockSpec((1, seq, MISC_W), lambda b, i: (b, 0, 0)),
            pl.BlockSpec((1, MISC_W), lambda b, i: (0, 0))],
        out_specs=pl.BlockSpec((1, TQ_WIDE, MIX_W), lambda b, i: (b, i, 0)),
        scratch_shapes=[pltpu.VMEM((HEADS, seq, LANES), F32),
                        pltpu.VMEM((1, LANES), F32)] + _attn_scratch(TQ_WIDE),
        compiler_params=_params(2),
        name="fox",
    )(qT, k, vT, misc, bf)


def _moba_kernel(qT_ref, k_ref, vT_ref, kmean_ref, o_ref,
                 bias_ref, rowadd_ref, ksq_ref, *st, slopes, n_blocks):
    qm_ref, m_ref, acc_ref = st[:3]
    i = pl.program_id(1)
    tq = TQ_WIDE
    n_diag = tq // CH
    _mask_heads(qT_ref, qm_ref)
    _init_state(m_ref, acc_ref)

    @pl.when(i == 0)
    def _norms():
        _key_norms(k_ref, ksq_ref, n_blocks * MOBA_BLOCK // CH)

    @pl.when(jnp.logical_and(pl.program_id(0) == 0, i == 0))
    def _bias_table():
        s_f = lax.broadcasted_iota(jnp.int32, (CH, tq), 0).astype(F32)
        for h in range(HEADS):
            pos = s_f * float(slopes[h] * LOG2E)
            bias_ref[0, h] = pos
            for d in range(n_diag):
                bias_ref[1 + d, h] = pos + _causal_neg(tq, d)

    n_i = lax.broadcasted_iota(jnp.int32, (n_blocks, tq), 0)
    n_f = n_i.astype(F32)
    own = n_diag * i + lax.broadcasted_iota(jnp.int32, (n_blocks, tq), 1) // MOBA_BLOCK
    past = n_i < own
    kmean_parts = _split3(kmean_ref[0])
    for h in range(HEADS):
        gate = sum(_dot(part, qm_ref[h]) for part in kmean_parts)
        gate = jnp.where(past, gate, -jnp.inf)
        chosen = jnp.zeros((n_blocks, tq), F32)
        for _ in range(MOBA_TOPK):
            mx = jnp.max(gate, axis=0, keepdims=True)
            first = jnp.min(jnp.where(gate == mx, n_f, float(n_blocks)), axis=0, keepdims=True)
            pick = n_f == first
            chosen = jnp.where(pick, 1.0, chosen)
            gate = jnp.where(pick, -jnp.inf, gate)
        chosen = jnp.where(past, chosen, 0.0)
        blk_shift = n_f * float(slopes[h] * LOG2E * MOBA_BLOCK)
        rowadd_ref[h] = jnp.where(jnp.logical_or(chosen > 0.5, n_i == own), blk_shift, NEG)

    def logits(h, c, qk, shift, diag, pen):
        row = rowadd_ref[h, pl.ds(c, 1), :]
        if diag >= 0:
            return qk + bias_ref[1 + diag, h] + row
        row = row + pen
        return qk + bias_ref[0, h] + (row if shift is None else row - shift)

    _self_attending(i, 0, k_ref, vT_ref, st, logits, ksq_ref)
    _finish(o_ref, acc_ref)


def _moba(qT, k, vT, kmean, slopes):
    bsz, seq, _ = k.shape
    n_blocks = seq // MOBA_BLOCK
    return pl.pallas_call(
        functools.partial(_moba_kernel, slopes=tuple(float(s) for s in slopes), n_blocks=n_blocks),
        out_shape=jax.ShapeDtypeStruct((bsz, seq, MIX_W), BF16),
        grid=(bsz, seq // TQ_WIDE),
        in_specs=_attn_specs(2, seq, TQ_WIDE) + [
            pl.BlockSpec((1, n_blocks, MIX_W), lambda b, i: (b, 0, 2))],
        out_specs=pl.BlockSpec((1, TQ_WIDE, MIX_W), lambda b, i: (b, i, 0)),
        scratch_shapes=[pltpu.VMEM((1 + TQ_WIDE // CH, HEADS, CH, TQ_WIDE), F32),
                        pltpu.VMEM((HEADS, n_blocks, TQ_WIDE), F32),
                        pltpu.VMEM((1, LANES), F32)] + _attn_scratch(TQ_WIDE),
        compiler_params=_params(2),
        name="moba",
    )(qT, k, vT, kmean)


DIL_SPAN = 2048 // CH + 1


def _dilated_kernel(qT_ref, k_ref, vT_ref, o_ref, table_ref, ksq_ref, *st, slopes):
    qm_ref, m_ref, acc_ref = st[:3]
    b = pl.program_id(0)
    i = pl.program_id(1)

    tq = TQ_WIDE
    n_diag = tq // CH
    n_entries = n_diag - 1 + DIL_SPAN

    @pl.when(jnp.logical_and(b == 0, i == 0))
    def _bias_table():
        s_i = lax.broadcasted_iota(jnp.int32, (CH, tq), 0)
        t_i = lax.broadcasted_iota(jnp.int32, (CH, tq), 1)
        for e in range(n_entries):
            j = e - (n_diag - 1)
            d = t_i - s_i + j * CH
            ok = d >= 0
            mult = (jnp.where(jnp.logical_and(ok, d <= 128), 1.0, 0.0)
                    + jnp.where(jnp.logical_and(ok, jnp.logical_and(d <= 512, (d & 3) == 0)), 1.0, 0.0)
                    + jnp.where(jnp.logical_and(ok, jnp.logical_and(d <= 2048, (d & 15) == 0)), 1.0, 0.0))
            logm = jnp.where(mult > 0.5, jnp.log2(jnp.maximum(mult, 1.0)), NEG)
            df = d.astype(F32)
            for h in range(HEADS):
                table_ref[h, e] = logm - df * float(slopes[h] * LOG2E)
        for h in range(HEADS):
            table_ref[h, n_entries] = jnp.full((CH, tq), NEG, F32)

    _mask_heads(qT_ref, qm_ref)
    _init_state(m_ref, acc_ref)

    def logits(h, c, qk, shift, diag, pen):
        if diag >= 0:
            return qk + table_ref[h, n_diag - 1 - diag]
        x = qk + table_ref[h, jnp.where(pen < 0.0, n_entries, n_diag * i - c + (n_diag - 1))]
        return x if shift is None else x - shift

    @pl.when(i == 0)
    def _norms():
        _key_norms(k_ref, ksq_ref, k_ref.shape[1] // CH)

    _self_attending(i, jnp.maximum(n_diag * i - (DIL_SPAN - 1), 0), k_ref, vT_ref, st, logits, ksq_ref)
    _finish(o_ref, acc_ref)


def _dilated(qT, k, vT, slopes):
    bsz, seq, _ = k.shape
    return pl.pallas_call(
        functools.partial(_dilated_kernel, slopes=tuple(float(s) for s in slopes)),
        out_shape=jax.ShapeDtypeStruct((bsz, seq, MIX_W), BF16),
        grid=(bsz, seq // TQ_WIDE),
        in_specs=_attn_specs(1, seq, TQ_WIDE),
        out_specs=pl.BlockSpec((1, TQ_WIDE, MIX_W), lambda b, i: (b, i, 0)),
        scratch_shapes=[pltpu.VMEM((HEADS, TQ_WIDE // CH + DIL_SPAN, CH, TQ_WIDE), F32),
                        pltpu.VMEM((1, LANES), F32)] + _attn_scratch(TQ_WIDE),
        compiler_params=_params(2),
        name="dilated",
    )(qT, k, vT)


def _dsa_kernel(qT_ref, k_ref, vT_ref, qiT_ref, misc_ref, miscT_ref, o_ref,
                bias_ref, qi_ref, key_ref, planes_ref, alive_ref, seen_ref, *st, slopes):
    qm_ref, m_ref, acc_ref = st[:3]
    i = pl.program_id(1)
    tq = CH

    @pl.when(jnp.logical_and(pl.program_id(0) == 0, i == 0))
    def _clear_planes():
        planes_ref[0:32] = jnp.zeros((32,) + planes_ref.shape[1:], jnp.int32)
        planes_ref[32] = jnp.full(planes_ref.shape[1:], -1, jnp.int32)
    _mask_heads(qT_ref, qm_ref)
    _init_state(m_ref, acc_ref)
    _key_pos_bias(bias_ref, slopes)

    qi_all = qiT_ref[0]
    zpad = jnp.zeros((MISC_W - IDX_DIM, tq), BF16)
    for h in range(IDX_HEADS):
        qi_ref[h] = jnp.concatenate([qi_all[h * IDX_DIM:(h + 1) * IDX_DIM], zpad], axis=0)
    w_rows = miscT_ref[0, WI_ROW:WI_ROW + IDX_HEADS, :] * float(IDX_HEADS ** -0.5 * IDX_DIM ** -0.5)

    s_i = lax.broadcasted_iota(jnp.int32, (CH, tq), 0)
    t_i = lax.broadcasted_iota(jnp.int32, (CH, tq), 1)

    def score_chunk(c, diag):
        off = pl.multiple_of(c * CH, CH)
        ki = misc_ref[0, pl.ds(off, CH), :].astype(BF16)
        score = jnp.zeros((CH, tq), F32)
        for h in range(IDX_HEADS):
            rel = jnp.maximum(_dot(ki, qi_ref[h]), 0.0)
            score = score + rel * w_rows[h:h + 1, :]
        bits = lax.bitcast_convert_type(score, jnp.int32)
        key = bits ^ ((bits >> 31) & 0x7FFFFFFF)
        key = jnp.where(key == -1, 0, key)
        if diag:
            key = jnp.where(s_i <= t_i, key, INT_MIN)
        key_ref[pl.ds(off, CH), :] = key
        ukey = key ^ INT_MIN
        words = _bit_transpose32([ukey[8 * j:8 * j + 8, :] for j in range(32)])
        row0 = pl.multiple_of(c * 8, 8)
        for b in range(32):
            planes_ref[b, pl.ds(row0, 8), :] = words[31 - b]

    score_chunk(i, True)

    def score_pair(t, carry):
        score_chunk(2 * t, False)
        score_chunk(jnp.minimum(2 * t + 1, i - 1), False)
        return carry

    lax.fori_loop(0, (i + 1) // 2, score_pair, 0)
    group = 4 * 8
    n_groups = i // 4 + 1
    alive_ref[...] = jnp.where(lax.broadcasted_iota(jnp.int32, alive_ref.shape, 0) < (i + 1) * 8, -1, 0)

    def narrow(alive, rows, b_prev, took_prev):
        hit = alive & planes_ref[b_prev, rows, :]
        return jnp.where(took_prev, hit, alive ^ hit)

    def bit_step(n, state):
        thr, need, took_prev = state
        b = 31 - n
        took_prev = took_prev != 0

        def sweep(g, cnt):
            rows = pl.ds(pl.multiple_of(g * group, group), group)
            alive = narrow(alive_ref[rows, :], rows, b + 1, took_prev)
            alive_ref[rows, :] = alive
            return cnt + lax.population_count(alive & planes_ref[b, rows, :])

        cnt = lax.fori_loop(0, n_groups, sweep, jnp.zeros((group, tq), jnp.int32))
        cnt = jnp.sum(cnt, axis=0, keepdims=True)
        ok = cnt >= need
        return (jnp.where(ok, thr | jnp.left_shift(jnp.int32(1), b), thr), jnp.where(ok, need, need - cnt),
                jnp.where(ok, 1, 0))

    uthr, n_take, took_last = lax.fori_loop(
        0, 32, bit_step, (jnp.zeros((1, tq), jnp.int32), jnp.full((1, tq), TOPK_KEYS, jnp.int32),
                          jnp.ones((1, tq), jnp.int32)))
    thr = uthr ^ INT_MIN

    def count_alive(g, cnt):
        rows = pl.ds(pl.multiple_of(g * group, group), group)
        return cnt + lax.population_count(narrow(alive_ref[rows, :], rows, 0, took_last != 0))

    n_eq = jnp.sum(lax.fori_loop(0, n_groups, count_alive, jnp.zeros((group, tq), jnp.int32)), axis=0, keepdims=True)
    tied = jnp.logical_and(n_eq > n_take, thr != INT_MIN)
    any_tied = jnp.max(jnp.where(tied, 1.0, 0.0)) > 0.5

    def logits(h, c, qk, neg, diag, pen):
        pos = bias_ref[h] + (c.astype(F32) * float(slopes[h] * LOG2E * CH) + pen)
        return qk + _lanes(pos, tq) + neg

    @pl.when(jnp.logical_not(any_tied))
    def _no_ties():
        thr_lo = jnp.where(thr == INT_MIN, INT_MIN + 1, thr)

        def selection(c):
            off = pl.multiple_of(c * CH, CH)
            return jnp.where(key_ref[pl.ds(off, CH), :] >= thr_lo, 0.0, NEG)

        _attention(i, 0, k_ref, vT_ref, st, logits, selection)

    @pl.when(any_tied)
    def _ties():
        n_first = jnp.where(thr == INT_MIN, 0, n_take).astype(F32)
        rr = lax.broadcasted_iota(jnp.int32, (CH, CH), 0)
        cc = lax.broadcasted_iota(jnp.int32, (CH, CH), 1)
        tri = jnp.where(cc <= rr, 1.0, 0.0).astype(BF16)
        seen_ref[...] = jnp.zeros(seen_ref.shape, F32)

        def selection(c):
            off = pl.multiple_of(c * CH, CH)
            key = key_ref[pl.ds(off, CH), :]
            eq = key == thr
            rank = _dot(tri, jnp.where(eq, 1.0, 0.0).astype(BF16)) + seen_ref[...]
            seen_ref[...] = rank[CH - 1:CH, :]
            take = jnp.logical_or(key > thr, jnp.logical_and(eq, rank <= n_first))
            return jnp.where(take, 0.0, NEG)

        _attention(i, 0, k_ref, vT_ref, st, logits, selection)

    _finish(o_ref, acc_ref)


def _dsa(qT, k, vT, qiT, misc, miscT, slopes):
    bsz, seq, _ = k.shape
    nqi = IDX_HEADS * IDX_DIM
    plane_rows = -(-(seq // CH) // 4) * 32
    return pl.pallas_call(
        functools.partial(_dsa_kernel, slopes=tuple(float(s) for s in slopes)),
        out_shape=jax.ShapeDtypeStruct((bsz, seq, MIX_W), BF16),
        grid=(bsz, seq // CH),
        in_specs=_attn_specs(0, seq) + [
            pl.BlockSpec((1, nqi, CH), lambda b, i: (b, 0, i)),
            pl.BlockSpec((1, seq, MISC_W), lambda b, i: (b, 0, 0)),
            pl.BlockSpec((1, MISC_W, CH), lambda b, i: (b, 0, i))],
        out_specs=pl.BlockSpec((1, CH, MIX_W), lambda b, i: (b, i, 0)),
        scratch_shapes=[pltpu.VMEM((HEADS, CH, LANES), F32),
                        pltpu.VMEM((IDX_HEADS, MISC_W, CH), BF16),
                        pltpu.VMEM((seq, CH), jnp.int32),
                        pltpu.VMEM((33, plane_rows, CH), jnp.int32),
                        pltpu.VMEM((plane_rows, CH), jnp.int32),
                        pltpu.VMEM((1, CH), F32)] + _attn_scratch(),
        compiler_params=_params(2),
        name="dsa",
    )(qT, k, vT, qiT, misc, miscT)


def _merge_kernel(x_ref, ada_ref, oa_ref, ob_ref, oc_ref, od_ref, wmg_ref, wbr_ref, wout_ref,
                  g_ref, b_ref, out_ref):
    x = x_ref[0]
    sh = ada_ref[0, 0:1, :]
    sc = ada_ref[0, 1:2, :]
    g1 = ada_ref[0, 2:3, :]
    h = (x * (1.0 + sc) + sh).astype(BF16)
    mixed = None
    for m, o_ref in enumerate((oa_ref, ob_ref, oc_ref, od_ref)):
        gate = jax.nn.sigmoid(_dot(h, wmg_ref[m]))
        term = gate * _dot(o_ref[0], wbr_ref[m])
        mixed = term if mixed is None else mixed + term
    y = _dot(mixed.astype(BF16), wout_ref[...])
    out_ref[0] = _layer_norm(DN_ALPHA * x + g1 * y, g_ref[...], b_ref[...])


def _merge(x, ada, outs, w_mg, w_br, w_out, ln_g, ln_b, tm=512):
    bsz, seq, d = x.shape
    o_spec = pl.BlockSpec((1, tm, MIX_W), lambda b, i: (b, i, 0))
    return pl.pallas_call(
        _merge_kernel,
        out_shape=jax.ShapeDtypeStruct((bsz, seq, d), F32),
        grid=(bsz, seq // tm),
        in_specs=[pl.BlockSpec((1, tm, d), lambda b, i: (b, i, 0)),
                  pl.BlockSpec((1, 6, d), lambda b, i: (b, 0, 0)),
                  o_spec, o_spec, o_spec, o_spec,
                  pl.BlockSpec((N_MIX, d, d), lambda b, i: (0, 0, 0)),
                  pl.BlockSpec((N_MIX, MIX_W, d), lambda b, i: (0, 0, 0)),
                  pl.BlockSpec((d, d), lambda b, i: (0, 0)),
                  pl.BlockSpec((1, d), lambda b, i: (0, 0)),
                  pl.BlockSpec((1, d), lambda b, i: (0, 0))],
        out_specs=pl.BlockSpec((1, tm, d), lambda b, i: (b, i, 0)),
        compiler_params=_params(2),
        name="merge",
    )(x, ada, *outs, w_mg.astype(BF16), w_br.astype(BF16), w_out.astype(BF16),
      ln_g.reshape(1, d), ln_b.reshape(1, d))


def _route(logits_t, bias_col):
    per_group = N_EXPERTS // N_GROUPS
    scores = jax.nn.sigmoid(logits_t)
    biased = scores + bias_col
    s_rows = [scores[e:e + 1] for e in range(N_EXPERTS)]
    b_rows = [biased[e:e + 1] for e in range(N_EXPERTS)]
    best_g = None
    for g in range(N_GROUPS):
        r = b_rows[g * per_group:(g + 1) * per_group]
        gs = None
        for a in range(per_group):
            for b in range(a + 1, per_group):
                pair = r[a] + r[b]
                gs = pair if gs is None else jnp.maximum(gs, pair)
        if best_g is None:
            best_g, best_v = jnp.zeros_like(gs, dtype=jnp.int32), gs
        else:
            better = gs > best_v
            best_g = jnp.where(better, g, best_g)
            best_v = jnp.maximum(best_v, gs)
    masked = [jnp.where(best_g == e // per_group, b_rows[e], -jnp.inf) for e in range(N_EXPERTS)]

    def argmax_first(rows):
        idx, val = jnp.zeros_like(best_g), rows[0]
        for e in range(1, N_EXPERTS):
            better = rows[e] > val
            idx = jnp.where(better, e, idx)
            val = jnp.maximum(val, rows[e])
        return idx

    e1 = argmax_first(masked)
    e2 = argmax_first([jnp.where(e1 == e, -jnp.inf, masked[e]) for e in range(N_EXPERTS)])
    s1 = sum(jnp.where(e1 == e, s_rows[e], 0.0) for e in range(N_EXPERTS))
    s2 = sum(jnp.where(e2 == e, s_rows[e], 0.0) for e in range(N_EXPERTS))
    tot = s1 + s2
    w1, w2 = s1 / tot, s2 / tot
    rows = [jnp.where(e1 == e, w1, 0.0) + jnp.where(e2 == e, w2, 0.0) for e in range(N_EXPERTS)]
    return jnp.concatenate(rows, axis=0), best_g


def _moe_kernel(x_ref, ada_ref, wr_ref, br_ref, wg_ref, wu_ref, wd_ref, g_ref, b_ref, out_ref,
                perm_ref, xs_ref, combs_ref, ys_ref, sub_ref, *, tm, rows):
    g = pl.program_id(1)
    per_group = N_EXPERTS // N_GROUPS

    @pl.when(g == 0)
    def _route_and_sort():
        x = x_ref[...]
        h2 = x * (1.0 + ada_ref[0, 4:5, :]) + ada_ref[0, 3:4, :]
        h_hi = h2.astype(BF16)
        h_lo = (h2 - h_hi.astype(F32)).astype(BF16)
        first = _dot(h_hi, wr_ref[...])
        logits = first[:, 0:LANES] + first[:, LANES:] + _dot(h_lo, wr_ref[:, 0:LANES])
        logits_t = logits.T[0:N_EXPERTS]
        comb_t, best_g = _route(logits_t, br_ref[...])
        member = [jnp.where(best_g == g, 1.0, 0.0) for g in range(N_GROUPS)]
        grp = jnp.concatenate(member + [jnp.zeros((8 - N_GROUPS, tm), F32)], axis=0).astype(BF16)
        s_i = lax.broadcasted_iota(jnp.int32, (tm, tm), 0)
        t_i = lax.broadcasted_iota(jnp.int32, (tm, tm), 1)
        rank = _dot(grp, jnp.where(s_i <= t_i, 1.0, 0.0).astype(BF16))
        pos = jnp.zeros((1, tm), F32)
        start = jnp.zeros((1, 1), F32)
        for g in range(N_GROUPS):
            cap = jnp.ceil(rank[g:g + 1, tm - 1:tm] * (1.0 / MOE_SUB)) * MOE_SUB
            pos = pos + member[g] * (start + rank[g:g + 1] - 1.0)
            sub_ref[2 * g] = (jnp.sum(start) * (1.0 / MOE_SUB)).astype(jnp.int32)
            sub_ref[2 * g + 1] = (jnp.sum(cap) * (1.0 / MOE_SUB)).astype(jnp.int32)
            start = start + cap
        r_f = lax.broadcasted_iota(jnp.int32, (rows, tm), 0).astype(F32)
        perm = jnp.where(r_f == pos, 1.0, 0.0).astype(BF16)
        perm_ref[...] = perm
        comb = jnp.concatenate([comb_t, jnp.zeros((LANES - N_EXPERTS, tm), F32)], axis=0).T
        c_hi = comb.astype(BF16)
        c_lo = (comb - c_hi.astype(F32)).astype(BF16)
        d = h_hi.shape[1]
        moved = _dot(perm, jnp.concatenate([h_hi, c_hi, c_lo], axis=1))
        xs_ref[...] = moved[:, 0:d].astype(BF16)
        combs_ref[...] = moved[:, d:d + LANES] + moved[:, d + LANES:]
        ys_ref[...] = jnp.zeros(ys_ref.shape, BF16)

    lane = lax.broadcasted_iota(jnp.int32, (MOE_SUB, LANES), 1)

    def sub_tile(j, carry):
        r0 = pl.multiple_of((sub_ref[2 * g] + j) * MOE_SUB, MOE_SUB)
        xj = xs_ref[pl.ds(r0, MOE_SUB), :]
        cw = combs_ref[pl.ds(r0, MOE_SUB), :]
        hids = []
        for q in range(per_group):
            w_e = jnp.sum(jnp.where(lane == g * per_group + q, cw, 0.0), axis=1, keepdims=True)
            hids.append((jax.nn.silu(_dot(xj, wg_ref[0, q])) * _dot(xj, wu_ref[0, q]) * w_e).astype(BF16))
        ys_ref[pl.ds(r0, MOE_SUB), :] = _dot(jnp.concatenate(hids, axis=1), wd_ref[0, 0]).astype(BF16)
        return carry

    lax.fori_loop(0, sub_ref[2 * g + 1], sub_tile, 0)

    @pl.when(g == N_GROUPS - 1)
    def _unsort_and_norm():
        y = lax.dot_general(perm_ref[...], ys_ref[...], (((0,), (0,)), ((), ())),
                            preferred_element_type=F32)
        z = DN_ALPHA * x_ref[...] + ada_ref[0, 5:6, :] * y
        out_ref[...] = _layer_norm(z, g_ref[...], b_ref[...])


def _moe(x, ada_rows, w_router, b_router, w_gate, w_up, w_down, ln_g, ln_b, batch_len, tm=1024):
    n, d = x.shape
    per_b = batch_len // tm
    per_group = N_EXPERTS // N_GROUPS
    rows = tm + N_GROUPS * MOE_SUB
    wr32 = jnp.pad(w_router, ((0, 0), (0, LANES - N_EXPERTS)))
    wr_hi = wr32.astype(BF16)
    wr = jnp.concatenate([wr_hi, (wr32 - wr_hi.astype(F32)).astype(BF16)], axis=1)
    br = b_router.reshape(N_EXPERTS, 1)
    return pl.pallas_call(
        functools.partial(_moe_kernel, tm=tm, rows=rows),
        out_shape=jax.ShapeDtypeStruct((n, d), F32),
        grid=(n // tm, N_GROUPS),
        in_specs=[pl.BlockSpec((tm, d), lambda i, g: (i, 0)),
                  pl.BlockSpec((1, 6, d), lambda i, g: (i // per_b, 0, 0)),
                  pl.BlockSpec((d, 2 * LANES), lambda i, g: (0, 0)),
                  pl.BlockSpec((N_EXPERTS, 1), lambda i, g: (0, 0)),
                  pl.BlockSpec((1, per_group, d, D_EXPERT), lambda i, g: (0, g, 0, 0)),
                  pl.BlockSpec((1, per_group, d, D_EXPERT), lambda i, g: (0, g, 0, 0)),
                  pl.BlockSpec((1, 1, per_group * D_EXPERT, d), lambda i, g: (0, g, 0, 0)),
                  pl.BlockSpec((1, d), lambda i, g: (0, 0)),
                  pl.BlockSpec((1, d), lambda i, g: (0, 0))],
        out_specs=pl.BlockSpec((tm, d), lambda i, g: (i, 0)),
        scratch_shapes=[pltpu.VMEM((rows, tm), BF16),
                        pltpu.VMEM((rows, d), BF16),
                        pltpu.VMEM((rows, LANES), F32),
                        pltpu.VMEM((rows, d), BF16),
                        pltpu.SMEM((2 * N_GROUPS,), jnp.int32)],
        compiler_params=_params(2),
        name="moe",
    )(x, ada_rows, wr, br, w_gate[None].astype(BF16), w_up[None].astype(BF16),
      w_down.reshape(1, N_GROUPS, per_group * D_EXPERT, d).astype(BF16),
      ln_g.reshape(1, d), ln_b.reshape(1, d))


def kernel(x, c, w_in, b_forget, w_branch, w_merge_gate, w_out, w_ada, b_ada, ln1_g, ln1_b, ln2_g, ln2_b,
           w_router, b_router, w_exp_gate, w_exp_up, w_exp_down):
    bsz, seq, d = x.shape
    slopes = _alibi_slopes()
    ada_all = _ada(c, w_ada, b_ada).reshape(DEPTH, bsz, 6, d)
    for l in range(DEPTH):
        ada = ada_all[l]
        k, kmean, misc, qT, vT, qiT, miscT = _proj(x, ada, w_in[l])
        o_a = _dsa(qT, k, vT, qiT, misc, miscT, slopes[0])
        o_b = _dilated(qT, k, vT, slopes[1])
        o_c = _moba(qT, k, vT, kmean, slopes[2])
        o_d = _fox(qT, k, vT, misc, b_forget[l])
        x = _merge(x, ada, (o_a, o_b, o_c, o_d), w_merge_gate[l], w_branch[l], w_out[l], ln1_g[l], ln1_b[l])
        x = _moe(x.reshape(bsz * seq, d), ada, w_router, b_router, w_exp_gate[l], w_exp_up[l], w_exp_down[l],
                 ln2_g[l], ln2_b[l], seq).reshape(bsz, seq, d)
    return x
```

```python
import functools

import numpy as np
import jax
import jax.numpy as jnp
from jax import lax
from jax.experimental import pallas as pl
from jax.experimental.pallas import tpu as pltpu

D_MODEL = 1024
HEAD_DIM = 64
HEADS = 4
MIX_W = HEADS * HEAD_DIM
N_MIX = 4
IDX_HEADS = 8
IDX_DIM = 64
TOPK_KEYS = 256
MOBA_BLOCK = 256
MOBA_TOPK = 3
N_EXPERTS = 16
N_GROUPS = 4
D_EXPERT = 512
DEPTH = 2
DN_ALPHA = (2 * DEPTH) ** 0.25
LN_EPS = 1e-5
IN_COLS = 3 * N_MIX * MIX_W + IDX_HEADS * IDX_DIM + IDX_DIM + IDX_HEADS + HEADS

CH = 256
ACC_ROWS = HEAD_DIM + 16
MOE_SUB = 128
TQ_WIDE = CH
SHIFT_MARGIN = 96.0
LANES = 128
MISC_W = 128
WI_ROW = IDX_DIM
FL_ROW = IDX_DIM + IDX_HEADS
LOG2E = 1.4426950408889634
NEG = -1e30
QSCALE = HEAD_DIM ** -0.5 * LOG2E
INT_MIN = -(2 ** 31)
VMEM_LIMIT = 60 * 1024 * 1024

F32 = jnp.float32
BF16 = jnp.bfloat16
HI = lax.Precision.HIGHEST
NT = (((1,), (1,)), ((), ()))


def _alibi_slopes():
    n = 3 * HEADS
    s = 2.0 ** (-8.0 * np.arange(1, n + 1) / n)
    return s.reshape(HEADS, 3).T


def _dot(a, b):
    return jnp.dot(a, b, preferred_element_type=F32)


def _dot_hi(a, b):
    return jnp.dot(a, b, preferred_element_type=F32, precision=HI)


def _bit_transpose32(words):
    a = list(words)
    j, m = 16, 0x0000FFFF
    while j:
        k = 0
        while k < 32:
            t = (a[k] ^ (a[k + j] >> j)) & m
            a[k] = a[k] ^ t
            a[k + j] = a[k + j] ^ (t << j)
            k = (k + j + 1) & ~j
        j >>= 1
        m = (m ^ (m << j)) & 0xFFFFFFFF
    return a


def _split3(x):
    x1 = x.astype(BF16)
    r1 = x - x1.astype(F32)
    x2 = r1.astype(BF16)
    x3 = (r1 - x2.astype(F32)).astype(BF16)
    return x1, x2, x3


def _params(n_axes):
    return pltpu.CompilerParams(dimension_semantics=("arbitrary",) * n_axes,
                                vmem_limit_bytes=VMEM_LIMIT)


def _layer_norm(z, g, b):
    mu = jnp.mean(z, axis=-1, keepdims=True)
    var = jnp.mean(jnp.square(z - mu), axis=-1, keepdims=True)
    return (z - mu) * lax.rsqrt(var + LN_EPS) * g + b


def _ada_kernel(c_ref, w_ref, b_ref, o_ref):
    o_ref[0] = _dot_hi(c_ref[...], w_ref[0]) + b_ref[0]


def _ada(c, w_ada, b_ada):
    depth, d, n = w_ada.shape
    bsz = c.shape[0]
    tn = D_MODEL
    return pl.pallas_call(
        _ada_kernel,
        out_shape=jax.ShapeDtypeStruct((depth, bsz, n), F32),
        grid=(depth, n // tn),
        in_specs=[pl.BlockSpec((bsz, d), lambda l, j: (0, 0)),
                  pl.BlockSpec((1, d, tn), lambda l, j: (l, 0, j)),
                  pl.BlockSpec((1, 1, tn), lambda l, j: (l, 0, j))],
        out_specs=pl.BlockSpec((1, bsz, tn), lambda l, j: (l, 0, j)),
        compiler_params=_params(2),
        name="ada",
    )(c, w_ada, b_ada.reshape(depth, 1, n))


def _proj_kernel(x_ref, ada_ref, wk_ref, wm_ref, wt_ref, wmt_ref,
                 k_ref, kmean_ref, misc_ref, qT_ref, vT_ref, qiT_ref, miscT_ref, *, tm):
    x = x_ref[0]
    sh = ada_ref[0, 0:1, :]
    sc = ada_ref[0, 1:2, :]
    h = (x * (1.0 + sc) + sh).astype(BF16)
    kf = _dot(h, wk_ref[...])
    k_ref[0] = kf.astype(BF16)
    for g in range(tm // MOBA_BLOCK):
        kmean_ref[0, 0, g:g + 1, :] = jnp.mean(kf[g * MOBA_BLOCK:(g + 1) * MOBA_BLOCK], axis=0, keepdims=True)
    misc_ref[0] = _dot(h, wm_ref[...])
    t = lax.dot_general(wt_ref[...], h, NT, preferred_element_type=F32)
    nq = N_MIX * MIX_W
    qT_ref[0] = (t[0:nq] * QSCALE).astype(BF16)
    for g in range(tm // CH):
        vT_ref[0, g] = t[nq:2 * nq, g * CH:(g + 1) * CH].astype(BF16)
    qiT_ref[0] = t[2 * nq:].astype(BF16)
    miscT_ref[0] = lax.dot_general(wmt_ref[...], h, NT, preferred_element_type=F32)


def _proj(x, ada, w_in, tm=512):
    bsz, seq, d = x.shape
    nq = N_MIX * MIX_W
    nqi = IDX_HEADS * IDX_DIM
    w = w_in.astype(BF16)
    wk = w[:, nq:2 * nq]
    wm = jnp.pad(w[:, 3 * nq + nqi:], ((0, 0), (0, MISC_W - (IN_COLS - 3 * nq - nqi))))
    wt = jnp.concatenate([w[:, 0:nq], w[:, 2 * nq:3 * nq], w[:, 3 * nq:3 * nq + nqi]], axis=1).T
    wmt = wm.T
    nt = seq // tm
    full = lambda b, i: (0, 0)
    outs = pl.pallas_call(
        functools.partial(_proj_kernel, tm=tm),
        out_shape=(jax.ShapeDtypeStruct((bsz, seq, nq), BF16),
                   jax.ShapeDtypeStruct((bsz, nt, tm // MOBA_BLOCK, nq), F32),
                   jax.ShapeDtypeStruct((bsz, seq, MISC_W), F32),
                   jax.ShapeDtypeStruct((bsz, nq, seq), BF16),
                   jax.ShapeDtypeStruct((bsz, seq // CH, nq, CH), BF16),
                   jax.ShapeDtypeStruct((bsz, nqi, seq), BF16),
                   jax.ShapeDtypeStruct((bsz, MISC_W, seq), F32)),
        grid=(bsz, nt),
        in_specs=[pl.BlockSpec((1, tm, d), lambda b, i: (b, i, 0)),
                  pl.BlockSpec((1, 6, d), lambda b, i: (b, 0, 0)),
                  pl.BlockSpec(wk.shape, full),
                  pl.BlockSpec(wm.shape, full),
                  pl.BlockSpec(wt.shape, full),
                  pl.BlockSpec(wmt.shape, full)],
        out_specs=(pl.BlockSpec((1, tm, nq), lambda b, i: (b, i, 0)),
                   pl.BlockSpec((1, 1, tm // MOBA_BLOCK, nq), lambda b, i: (b, i, 0, 0)),
                   pl.BlockSpec((1, tm, MISC_W), lambda b, i: (b, i, 0)),
                   pl.BlockSpec((1, nq, tm), lambda b, i: (b, 0, i)),
                   pl.BlockSpec((1, tm // CH, nq, CH), lambda b, i: (b, i, 0, 0)),
                   pl.BlockSpec((1, nqi, tm), lambda b, i: (b, 0, i)),
                   pl.BlockSpec((1, MISC_W, tm), lambda b, i: (b, 0, i))),
        compiler_params=_params(2),
        name="proj",
    )(x, ada, wk, wm, wt, wmt)
    k, kmean, misc, qT, vT, qiT, miscT = outs
    return k, kmean.reshape(bsz, seq // MOBA_BLOCK, nq), misc, qT, vT, qiT, miscT


def _mask_heads(qT_ref, qm_ref):
    q = qT_ref[0]
    rowh = lax.broadcasted_iota(jnp.int32, q.shape, 0) // HEAD_DIM
    for h in range(HEADS):
        qm_ref[h] = jnp.where(rowh == h, q, jnp.zeros_like(q))


def _init_state(m_ref, acc_ref):
    m_ref[...] = jnp.full(m_ref.shape, NEG, F32)
    acc_ref[...] = jnp.zeros(acc_ref.shape, F32)


def _attention(i, first, k_ref, vT_ref, st, logits_fn, chunk_ctx=None, n_diag=None):
    qm_ref, m_ref, acc_ref, s_ref, mx_ref, p_ref, alpha_ref = st
    if n_diag is None:
        n_chunks = i - first + 1
    else:
        n_chunks = n_diag + (n_diag * i - first)
    ones = jnp.ones((ACC_ROWS - HEAD_DIM, CH), BF16)

    def chunk_of(n):
        n = jnp.clip(n, 0, n_chunks - 1)
        if n_diag is None:
            return first + n
        return jnp.where(n < n_diag, n_diag * i + n, first + n - n_diag)

    def position(n):
        if n_diag is None:
            return None
        return n if n < n_diag else -1

    def logits_head(h, slot, c, kc, ctx, diag, pen):
        x = logits_fn(h, c, _dot(kc, qm_ref[h]), ctx, diag, pen)
        s_ref[slot, h] = x
        mx_ref[slot, h] = jnp.max(x, axis=0, keepdims=True)

    def softmax_head(h, slot):
        m_old = m_ref[h]
        m_new = jnp.maximum(m_old, mx_ref[slot, h])
        alpha_ref[slot, h] = jnp.exp2(m_old - m_new)
        p_ref[slot, h] = jnp.exp2(s_ref[slot, h] - m_new).astype(BF16)
        m_ref[h] = m_new

    def pv_head(h, slot, vc):
        v1 = jnp.concatenate([vc[h * HEAD_DIM:(h + 1) * HEAD_DIM, :], ones], axis=0)
        acc_ref[h] = alpha_ref[slot, h] * acc_ref[h] + _dot(v1, p_ref[slot, h])

    def step(n, a, diag):
        c = chunk_of(n + 1)
        pen = jnp.where(n + 1 < n_chunks, 0.0, NEG)
        kc = k_ref[0, pl.ds(pl.multiple_of(c * CH, CH), CH), :]
        vc = vT_ref[0, chunk_of(n - 1)]
        ctx = chunk_ctx(c) if chunk_ctx is not None else None
        for h in range(HEADS):
            softmax_head(h, a)
        for h in range(HEADS):
            pv_head(h, 1 - a, vc)
        for h in range(HEADS):
            logits_head(h, 1 - a, c, kc, ctx, diag, pen)

    p_ref[1] = jnp.zeros(p_ref.shape[1:], BF16)
    alpha_ref[1] = jnp.ones(alpha_ref.shape[1:], F32)
    c0 = chunk_of(jnp.int32(0))
    k0 = k_ref[0, pl.ds(pl.multiple_of(c0 * CH, CH), CH), :]
    ctx0 = chunk_ctx(c0) if chunk_ctx is not None else None
    for h in range(HEADS):
        logits_head(h, 0, c0, k0, ctx0, position(0), 0.0)

    first_trip = 0
    if n_diag is not None and n_diag > 1:
        step(0, 0, position(1))
        step(1, 1, position(2))
        first_trip = 1
    later = position(2 * first_trip + 1)

    def pair(t, carry):
        step(2 * t, 0, later)
        step(2 * t + 1, 1, later)
        return carry

    n_trips = (n_chunks + 1) // 2
    lax.fori_loop(first_trip, n_trips, pair, 0)
    v_last = vT_ref[0, chunk_of(2 * n_trips - 1)]
    for h in range(HEADS):
        pv_head(h, 1, v_last)


def _attention_fixed_shift(i, first, k_ref, vT_ref, st, logits_fn):
    qm_ref, m_ref, acc_ref, _, _, p_ref, _ = st
    n_past = i - first
    ones = jnp.ones((ACC_ROWS - HEAD_DIM, CH), BF16)

    def chunk_of(u):
        return jnp.where(u == 0, i, jnp.minimum(first + u - 1, i))

    def produce(u, slot):
        c = chunk_of(u)
        pen = jnp.where(u <= n_past, 0.0, NEG)
        kc = k_ref[0, pl.ds(pl.multiple_of(c * CH, CH), CH), :]
        for h in range(HEADS):
            x = logits_fn(h, c, _dot(kc, qm_ref[h]), m_ref[h], -1, pen)
            p_ref[slot, h] = jnp.exp2(x).astype(BF16)

    def consume(u, slot):
        vc = vT_ref[0, chunk_of(u)]
        for h in range(HEADS):
            v1 = jnp.concatenate([vc[h * HEAD_DIM:(h + 1) * HEAD_DIM, :], ones], axis=0)
            acc_ref[h] = acc_ref[h] + _dot(v1, p_ref[slot, h])

    c0 = jnp.asarray(i, jnp.int32)
    k0 = k_ref[0, pl.ds(pl.multiple_of(c0 * CH, CH), CH), :]
    for h in range(HEADS):
        x = logits_fn(h, c0, _dot(k0, qm_ref[h]), None, 0, 0.0)
        shift = jnp.max(x, axis=0, keepdims=True) + SHIFT_MARGIN
        m_ref[h] = shift
        p_ref[0, h] = jnp.exp2(x - shift).astype(BF16)
    produce(jnp.int32(1), 1)

    def pair(t, carry):
        consume(2 * t, 0)
        produce(2 * t + 2, 0)
        consume(2 * t + 1, 1)
        produce(2 * t + 3, 1)
        return carry

    def quad(q, carry):
        pair(2 * q, carry)
        pair(2 * q + 1, carry)
        return carry

    def octet(o, carry):
        quad(2 * o, carry)
        quad(2 * o + 1, carry)
        return carry

    n_octets = (n_past + 1) // 8
    n_quads = (n_past + 1) // 4
    lax.fori_loop(0, n_octets, octet, 0)
    lax.fori_loop(2 * n_octets, n_quads, quad, 0)
    lax.fori_loop(2 * n_quads, (n_past + 2) // 2, pair, 0)


def _key_norms(k_ref, ksq_ref, n_chunks):
    row = lax.broadcasted_iota(jnp.int32, (MIX_W, LANES), 0) // HEAD_DIM
    ind = jnp.where(row == lax.broadcasted_iota(jnp.int32, (MIX_W, LANES), 1), 1.0, 0.0).astype(BF16)

    def body(c, best):
        kc = k_ref[0, pl.ds(pl.multiple_of(c * CH, CH), CH), :].astype(F32)
        sq = (kc * kc * (1.0 + 2.0 ** -7)).astype(BF16)
        return jnp.maximum(best, jnp.max(_dot(sq, ind), axis=0, keepdims=True))

    ksq_ref[...] = lax.fori_loop(0, n_chunks, body, jnp.zeros((1, LANES), F32))


def _small_products(qm_ref, ksq_ref):
    lane = lax.broadcasted_iota(jnp.int32, (1, LANES), 1)
    worst = jnp.float32(0.0)
    for h in range(HEADS):
        q = qm_ref[h].astype(F32)
        qsq = jnp.max(jnp.sum(q * q, axis=0, keepdims=True))
        ksq = jnp.max(jnp.where(lane == h, ksq_ref[...], 0.0))
        worst = jnp.maximum(worst, qsq * ksq)
    return worst * 1.01 <= (SHIFT_MARGIN / 2.0) ** 2


def _self_attending(i, first, k_ref, vT_ref, st, logits_fn, ksq_ref):
    assert TQ_WIDE == CH
    small = _small_products(st[0], ksq_ref)

    @pl.when(small)
    def _fixed():
        _attention_fixed_shift(i, first, k_ref, vT_ref, st, logits_fn)

    @pl.when(jnp.logical_not(small))
    def _running():
        _attention(i, first, k_ref, vT_ref, st, logits_fn, n_diag=1)


def _finish(o_ref, acc_ref):
    parts = [acc_ref[h, 0:HEAD_DIM, :] / acc_ref[h, HEAD_DIM:HEAD_DIM + 1, :] for h in range(HEADS)]
    o_ref[0] = jnp.concatenate(parts, axis=0).T.astype(BF16)


def _lanes(tile, tq):
    return tile if tq == LANES else jnp.concatenate([tile] * (tq // LANES), axis=1)


def _causal_neg(tq, d=0):
    s_i = lax.broadcasted_iota(jnp.int32, (CH, tq), 0) + d * CH
    t_i = lax.broadcasted_iota(jnp.int32, (CH, tq), 1)
    return jnp.where(s_i <= t_i, 0.0, NEG).astype(F32)


def _key_pos_bias(bias_ref, slopes):
    s_i = lax.broadcasted_iota(jnp.int32, (CH, LANES), 0).astype(F32)
    for h in range(HEADS):
        bias_ref[h] = s_i * float(slopes[h] * LOG2E)


def _attn_specs(mixer, seq, tq=CH):
    return [pl.BlockSpec((1, MIX_W, tq), lambda b, i: (b, mixer, i)),
            pl.BlockSpec((1, seq, MIX_W), lambda b, i: (b, 0, mixer)),
            pl.BlockSpec((1, seq // CH, MIX_W, CH), lambda b, i: (b, 0, mixer, 0))]


def _attn_scratch(tq=CH):
    return [pltpu.VMEM((HEADS, MIX_W, tq), BF16),
            pltpu.VMEM((HEADS, 1, tq), F32),
            pltpu.VMEM((HEADS, ACC_ROWS, tq), F32),
            pltpu.VMEM((2, HEADS, CH, tq), F32),
            pltpu.VMEM((2, HEADS, 1, tq), F32),
            pltpu.VMEM((2, HEADS, CH, tq), BF16),
            pltpu.VMEM((2, HEADS, 1, tq), F32)]


def _fox_kernel(qT_ref, k_ref, vT_ref, misc_ref, bf_ref, o_ref,
                cum_ref, ksq_ref, *st, n_chunks):
    qm_ref, m_ref, acc_ref = st[:3]
    i = pl.program_id(1)

    @pl.when(i == 0)
    def _cumulative_gates():
        _key_norms(k_ref, ksq_ref, n_chunks)
        rr = lax.broadcasted_iota(jnp.int32, (CH, CH), 0)
        cc = lax.broadcasted_iota(jnp.int32, (CH, CH), 1)
        tri = jnp.where(cc <= rr, 1.0, 0.0).astype(BF16)

        def body(blk, carry):
            off = pl.multiple_of(blk * CH, CH)
            z = misc_ref[0, pl.ds(off, CH), :] + bf_ref[...]
            ls = jnp.minimum(z, 0.0) - jnp.log1p(jnp.exp(-jnp.abs(z)))
            csum = sum(_dot(tri, part) for part in _split3(ls))
            new = []
            for h in range(HEADS):
                col = csum[:, FL_ROW + h:FL_ROW + h + 1]
                cum = jnp.broadcast_to(col, (CH, LANES)) + carry[h]
                cum_ref[h, pl.ds(off, CH), :] = cum * LOG2E
                new.append(cum[CH - 1:CH, :])
            return tuple(new)

        lax.fori_loop(0, n_chunks, body, tuple(jnp.zeros((1, LANES), F32) for _ in range(HEADS)))

    _mask_heads(qT_ref, qm_ref)
    _init_state(m_ref, acc_ref)

    def logits(h, c, qk, shift, diag, pen):
        off = pl.multiple_of(c * CH, CH)
        cum = cum_ref[h, pl.ds(off, CH), :]
        if diag >= 0:
            return qk - _lanes(cum, TQ_WIDE) + _causal_neg(TQ_WIDE, diag)
        x = qk - _lanes(cum - pen, TQ_WIDE)
        return x if shift is None else x - shift

    _self_attending(i, 0, k_ref, vT_ref, st, logits, ksq_ref)
    _finish(o_ref, acc_ref)


def _fox(qT, k, vT, misc, b_forget_l):
    bsz, seq, _ = k.shape
    n_chunks = seq // CH
    bf = jnp.zeros((1, MISC_W), F32).at[0, FL_ROW:FL_ROW + HEADS].set(b_forget_l)
    return pl.pallas_call(
        functools.partial(_fox_kernel, n_chunks=n_chunks),
        out_shape=jax.ShapeDtypeStruct((bsz, seq, MIX_W), BF16),
        grid=(bsz, seq // TQ_WIDE),
        in_specs=_attn_specs(3, seq, TQ_WIDE) + [
            pl.BlockSpec((1, seq, MISC_W), lambda b, i: (b, 0, 0)),
            pl.BlockSpec((1, MISC_W), lambda b, i: (0, 0))],
        out_specs=pl.BlockSpec((1, TQ_WIDE, MIX_W), lambda b, i: (b, i, 0)),
        scratch_shapes=[pltpu.VMEM((HEADS, seq, LANES), F32),
                        pltpu.VMEM((1, LANES), F32)] + _attn_scratch(TQ_WIDE),
        compiler_params=_params(2),
        name="fox",
    )(qT, k, vT, misc, bf)


def _moba_kernel(qT_ref, k_ref, vT_ref, kmean_ref, o_ref,
                 bias_ref, rowadd_ref, ksq_ref, *st, slopes, n_blocks):
    qm_ref, m_ref, acc_ref = st[:3]
    i = pl.program_id(1)
    tq = TQ_WIDE
    n_diag = tq // CH
    _mask_heads(qT_ref, qm_ref)
    _init_state(m_ref, acc_ref)

    @pl.when(i == 0)
    def _norms():
        _key_norms(k_ref, ksq_ref, n_blocks * MOBA_BLOCK // CH)

    @pl.when(jnp.logical_and(pl.program_id(0) == 0, i == 0))
    def _bias_table():
        s_f = lax.broadcasted_iota(jnp.int32, (CH, tq), 0).astype(F32)
        for h in range(HEADS):
            pos = s_f * float(slopes[h] * LOG2E)
            bias_ref[0, h] = pos
            for d in range(n_diag):
                bias_ref[1 + d, h] = pos + _causal_neg(tq, d)

    n_i = lax.broadcasted_iota(jnp.int32, (n_blocks, tq), 0)
    n_f = n_i.astype(F32)
    own = n_diag * i + lax.broadcasted_iota(jnp.int32, (n_blocks, tq), 1) // MOBA_BLOCK
    past = n_i < own
    kmean_parts = _split3(kmean_ref[0])
    for h in range(HEADS):
        gate = sum(_dot(part, qm_ref[h]) for part in kmean_parts)
        gate = jnp.where(past, gate, -jnp.inf)
        chosen = jnp.zeros((n_blocks, tq), F32)
        for _ in range(MOBA_TOPK):
            mx = jnp.max(gate, axis=0, keepdims=True)
            first = jnp.min(jnp.where(gate == mx, n_f, float(n_blocks)), axis=0, keepdims=True)
            pick = n_f == first
            chosen = jnp.where(pick, 1.0, chosen)
            gate = jnp.where(pick, -jnp.inf, gate)
        chosen = jnp.where(past, chosen, 0.0)
        blk_shift = n_f * float(slopes[h] * LOG2E * MOBA_BLOCK)
        rowadd_ref[h] = jnp.where(jnp.logical_or(chosen > 0.5, n_i == own), blk_shift, NEG)

    def logits(h, c, qk, shift, diag, pen):
        row = rowadd_ref[h, pl.ds(c, 1), :]
        if diag >= 0:
            return qk + bias_ref[1 + diag, h] + row
        row = row + pen
        return qk + bias_ref[0, h] + (row if shift is None else row - shift)

    _self_attending(i, 0, k_ref, vT_ref, st, logits, ksq_ref)
    _finish(o_ref, acc_ref)


def _moba(qT, k, vT, kmean, slopes):
    bsz, seq, _ = k.shape
    n_blocks = seq // MOBA_BLOCK
    return pl.pallas_call(
        functools.partial(_moba_kernel, slopes=tuple(float(s) for s in slopes), n_blocks=n_blocks),
        out_shape=jax.ShapeDtypeStruct((bsz, seq, MIX_W), BF16),
        grid=(bsz, seq // TQ_WIDE),
        in_specs=_attn_specs(2, seq, TQ_WIDE) + [
            pl.BlockSpec((1, n_blocks, MIX_W), lambda b, i: (b, 0, 2))],
        out_specs=pl.BlockSpec((1, TQ_WIDE, MIX_W), lambda b, i: (b, i, 0)),
        scratch_shapes=[pltpu.VMEM((1 + TQ_WIDE // CH, HEADS, CH, TQ_WIDE), F32),
                        pltpu.VMEM((HEADS, n_blocks, TQ_WIDE), F32),
                        pltpu.VMEM((1, LANES), F32)] + _attn_scratch(TQ_WIDE),
        compiler_params=_params(2),
        name="moba",
    )(qT, k, vT, kmean)


DIL_SPAN = 2048 // CH + 1


def _dilated_kernel(qT_ref, k_ref, vT_ref, o_ref, table_ref, ksq_ref, *st, slopes):
    qm_ref, m_ref, acc_ref = st[:3]
    b = pl.program_id(0)
    i = pl.program_id(1)

    tq = TQ_WIDE
    n_diag = tq // CH
    n_entries = n_diag - 1 + DIL_SPAN

    @pl.when(jnp.logical_and(b == 0, i == 0))
    def _bias_table():
        s_i = lax.broadcasted_iota(jnp.int32, (CH, tq), 0)
        t_i = lax.broadcasted_iota(jnp.int32, (CH, tq), 1)
        for e in range(n_entries):
            j = e - (n_diag - 1)
            d = t_i - s_i + j * CH
            ok = d >= 0
            mult = (jnp.where(jnp.logical_and(ok, d <= 128), 1.0, 0.0)
                    + jnp.where(jnp.logical_and(ok, jnp.logical_and(d <= 512, (d & 3) == 0)), 1.0, 0.0)
                    + jnp.where(jnp.logical_and(ok, jnp.logical_and(d <= 2048, (d & 15) == 0)), 1.0, 0.0))
            logm = jnp.where(mult > 0.5, jnp.log2(jnp.maximum(mult, 1.0)), NEG)
            df = d.astype(F32)
            for h in range(HEADS):
                table_ref[h, e] = logm - df * float(slopes[h] * LOG2E)
        for h in range(HEADS):
            table_ref[h, n_entries] = jnp.full((CH, tq), NEG, F32)

    _mask_heads(qT_ref, qm_ref)
    _init_state(m_ref, acc_ref)

    def logits(h, c, qk, shift, diag, pen):
        if diag >= 0:
            return qk + table_ref[h, n_diag - 1 - diag]
        x = qk + table_ref[h, jnp.where(pen < 0.0, n_entries, n_diag * i - c + (n_diag - 1))]
        return x if shift is None else x - shift

    @pl.when(i == 0)
    def _norms():
        _key_norms(k_ref, ksq_ref, k_ref.shape[1] // CH)

    _self_attending(i, jnp.maximum(n_diag * i - (DIL_SPAN - 1), 0), k_ref, vT_ref, st, logits, ksq_ref)
    _finish(o_ref, acc_ref)


def _dilated(qT, k, vT, slopes):
    bsz, seq, _ = k.shape
    return pl.pallas_call(
        functools.partial(_dilated_kernel, slopes=tuple(float(s) for s in slopes)),
        out_shape=jax.ShapeDtypeStruct((bsz, seq, MIX_W), BF16),
        grid=(bsz, seq // TQ_WIDE),
        in_specs=_attn_specs(1, seq, TQ_WIDE),
        out_specs=pl.BlockSpec((1, TQ_WIDE, MIX_W), lambda b, i: (b, i, 0)),
        scratch_shapes=[pltpu.VMEM((HEADS, TQ_WIDE // CH + DIL_SPAN, CH, TQ_WIDE), F32),
                        pltpu.VMEM((1, LANES), F32)] + _attn_scratch(TQ_WIDE),
        compiler_params=_params(2),
        name="dilated",
    )(qT, k, vT)


def _dsa_kernel(qT_ref, k_ref, vT_ref, qiT_ref, misc_ref, miscT_ref, o_ref,
                bias_ref, qi_ref, key_ref, planes_ref, alive_ref, seen_ref, *st, slopes):
    qm_ref, m_ref, acc_ref = st[:3]
    i = pl.program_id(1)
    tq = CH

    @pl.when(jnp.logical_and(pl.program_id(0) == 0, i == 0))
    def _clear_planes():
        planes_ref[0:32] = jnp.zeros((32,) + planes_ref.shape[1:], jnp.int32)
        planes_ref[32] = jnp.full(planes_ref.shape[1:], -1, jnp.int32)
    _mask_heads(qT_ref, qm_ref)
    _init_state(m_ref, acc_ref)
    _key_pos_bias(bias_ref, slopes)

    qi_all = qiT_ref[0]
    zpad = jnp.zeros((MISC_W - IDX_DIM, tq), BF16)
    for h in range(IDX_HEADS):
        qi_ref[h] = jnp.concatenate([qi_all[h * IDX_DIM:(h + 1) * IDX_DIM], zpad], axis=0)
    w_rows = miscT_ref[0, WI_ROW:WI_ROW + IDX_HEADS, :] * float(IDX_HEADS ** -0.5 * IDX_DIM ** -0.5)

    s_i = lax.broadcasted_iota(jnp.int32, (CH, tq), 0)
    t_i = lax.broadcasted_iota(jnp.int32, (CH, tq), 1)

    def score_chunk(c, diag):
        off = pl.multiple_of(c * CH, CH)
        ki = misc_ref[0, pl.ds(off, CH), :].astype(BF16)
        score = jnp.zeros((CH, tq), F32)
        for h in range(IDX_HEADS):
            rel = jnp.maximum(_dot(ki, qi_ref[h]), 0.0)
            score = score + rel * w_rows[h:h + 1, :]
        bits = lax.bitcast_convert_type(score, jnp.int32)
        key = bits ^ ((bits >> 31) & 0x7FFFFFFF)
        key = jnp.where(key == -1, 0, key)
        if diag:
            key = jnp.where(s_i <= t_i, key, INT_MIN)
        key_ref[pl.ds(off, CH), :] = key
        ukey = key ^ INT_MIN
        words = _bit_transpose32([ukey[8 * j:8 * j + 8, :] for j in range(32)])
        row0 = pl.multiple_of(c * 8, 8)
        for b in range(32):
            planes_ref[b, pl.ds(row0, 8), :] = words[31 - b]

    score_chunk(i, True)

    def score_pair(t, carry):
        score_chunk(2 * t, False)
        score_chunk(jnp.minimum(2 * t + 1, i - 1), False)
        return carry

    lax.fori_loop(0, (i + 1) // 2, score_pair, 0)
    group = 4 * 8
    n_groups = i // 4 + 1
    alive_ref[...] = jnp.where(lax.broadcasted_iota(jnp.int32, alive_ref.shape, 0) < (i + 1) * 8, -1, 0)

    def narrow(alive, rows, b_prev, took_prev):
        hit = alive & planes_ref[b_prev, rows, :]
        return jnp.where(took_prev, hit, alive ^ hit)

    def bit_step(n, state):
        thr, need, took_prev = state
        b = 31 - n
        took_prev = took_prev != 0

        def sweep(g, cnt):
            rows = pl.ds(pl.multiple_of(g * group, group), group)
            alive = narrow(alive_ref[rows, :], rows, b + 1, took_prev)
            alive_ref[rows, :] = alive
            return cnt + lax.population_count(alive & planes_ref[b, rows, :])

        cnt = lax.fori_loop(0, n_groups, sweep, jnp.zeros((group, tq), jnp.int32))
        cnt = jnp.sum(cnt, axis=0, keepdims=True)
        ok = cnt >= need
        return (jnp.where(ok, thr | jnp.left_shift(jnp.int32(1), b), thr), jnp.where(ok, need, need - cnt),
                jnp.where(ok, 1, 0))

    uthr, n_take, took_last = lax.fori_loop(
        0, 32, bit_step, (jnp.zeros((1, tq), jnp.int32), jnp.full((1, tq), TOPK_KEYS, jnp.int32),
                          jnp.ones((1, tq), jnp.int32)))
    thr = uthr ^ INT_MIN

    def count_alive(g, cnt):
        rows = pl.ds(pl.multiple_of(g * group, group), group)
        return cnt + lax.population_count(narrow(alive_ref[rows, :], rows, 0, took_last != 0))

    n_eq = jnp.sum(lax.fori_loop(0, n_groups, count_alive, jnp.zeros((group, tq), jnp.int32)), axis=0, keepdims=True)
    tied = jnp.logical_and(n_eq > n_take, thr != INT_MIN)
    any_tied = jnp.max(jnp.where(tied, 1.0, 0.0)) > 0.5

    def logits(h, c, qk, neg, diag, pen):
        pos = bias_ref[h] + (c.astype(F32) * float(slopes[h] * LOG2E * CH) + pen)
        return qk + _lanes(pos, tq) + neg

    @pl.when(jnp.logical_not(any_tied))
    def _no_ties():
        thr_lo = jnp.where(thr == INT_MIN, INT_MIN + 1, thr)

        def selection(c):
            off = pl.multiple_of(c * CH, CH)
            return jnp.where(key_ref[pl.ds(off, CH), :] >= thr_lo, 0.0, NEG)

        _attention(i, 0, k_ref, vT_ref, st, logits, selection)

    @pl.when(any_tied)
    def _ties():
        n_first = jnp.where(thr == INT_MIN, 0, n_take).astype(F32)
        rr = lax.broadcasted_iota(jnp.int32, (CH, CH), 0)
        cc = lax.broadcasted_iota(jnp.int32, (CH, CH), 1)
        tri = jnp.where(cc <= rr, 1.0, 0.0).astype(BF16)
        seen_ref[...] = jnp.zeros(seen_ref.shape, F32)

        def selection(c):
            off = pl.multiple_of(c * CH, CH)
            key = key_ref[pl.ds(off, CH), :]
            eq = key == thr
            rank = _dot(tri, jnp.where(eq, 1.0, 0.0).astype(BF16)) + seen_ref[...]
            seen_ref[...] = rank[CH - 1:CH, :]
            take = jnp.logical_or(key > thr, jnp.logical_and(eq, rank <= n_first))
            return jnp.where(take, 0.0, NEG)

        _attention(i, 0, k_ref, vT_ref, st, logits, selection)

    _finish(o_ref, acc_ref)


def _dsa(qT, k, vT, qiT, misc, miscT, slopes):
    bsz, seq, _ = k.shape
    nqi = IDX_HEADS * IDX_DIM
    plane_rows = -(-(seq // CH) // 4) * 32
    return pl.pallas_call(
        functools.partial(_dsa_kernel, slopes=tuple(float(s) for s in slopes)),
        out_shape=jax.ShapeDtypeStruct((bsz, seq, MIX_W), BF16),
        grid=(bsz, seq // CH),
        in_specs=_attn_specs(0, seq) + [
            pl.BlockSpec((1, nqi, CH), lambda b, i: (b, 0, i)),
            pl.BlockSpec((1, seq, MISC_W), lambda b, i: (b, 0, 0)),
            pl.BlockSpec((1, MISC_W, CH), lambda b, i: (b, 0, i))],
        out_specs=pl.BlockSpec((1, CH, MIX_W), lambda b, i: (b, i, 0)),
        scratch_shapes=[pltpu.VMEM((HEADS, CH, LANES), F32),
                        pltpu.VMEM((IDX_HEADS, MISC_W, CH), BF16),
                        pltpu.VMEM((seq, CH), jnp.int32),
                        pltpu.VMEM((33, plane_rows, CH), jnp.int32),
                        pltpu.VMEM((plane_rows, CH), jnp.int32),
                        pltpu.VMEM((1, CH), F32)] + _attn_scratch(),
        compiler_params=_params(2),
        name="dsa",
    )(qT, k, vT, qiT, misc, miscT)


def _merge_kernel(x_ref, ada_ref, oa_ref, ob_ref, oc_ref, od_ref, wmg_ref, wbr_ref, wout_ref,
                  g_ref, b_ref, out_ref):
    x = x_ref[0]
    sh = ada_ref[0, 0:1, :]
    sc = ada_ref[0, 1:2, :]
    g1 = ada_ref[0, 2:3, :]
    h = (x * (1.0 + sc) + sh).astype(BF16)
    mixed = None
    for m, o_ref in enumerate((oa_ref, ob_ref, oc_ref, od_ref)):
        gate = jax.nn.sigmoid(_dot(h, wmg_ref[m]))
        term = gate * _dot(o_ref[0], wbr_ref[m])
        mixed = term if mixed is None else mixed + term
    y = _dot(mixed.astype(BF16), wout_ref[...])
    out_ref[0] = _layer_norm(DN_ALPHA * x + g1 * y, g_ref[...], b_ref[...])


def _merge(x, ada, outs, w_mg, w_br, w_out, ln_g, ln_b, tm=512):
    bsz, seq, d = x.shape
    o_spec = pl.BlockSpec((1, tm, MIX_W), lambda b, i: (b, i, 0))
    return pl.pallas_call(
        _merge_kernel,
        out_shape=jax.ShapeDtypeStruct((bsz, seq, d), F32),
        grid=(bsz, seq // tm),
        in_specs=[pl.BlockSpec((1, tm, d), lambda b, i: (b, i, 0)),
                  pl.BlockSpec((1, 6, d), lambda b, i: (b, 0, 0)),
                  o_spec, o_spec, o_spec, o_spec,
                  pl.BlockSpec((N_MIX, d, d), lambda b, i: (0, 0, 0)),
                  pl.BlockSpec((N_MIX, MIX_W, d), lambda b, i: (0, 0, 0)),
                  pl.BlockSpec((d, d), lambda b, i: (0, 0)),
                  pl.BlockSpec((1, d), lambda b, i: (0, 0)),
                  pl.BlockSpec((1, d), lambda b, i: (0, 0))],
        out_specs=pl.BlockSpec((1, tm, d), lambda b, i: (b, i, 0)),
        compiler_params=_params(2),
        name="merge",
    )(x, ada, *outs, w_mg.astype(BF16), w_br.astype(BF16), w_out.astype(BF16),
      ln_g.reshape(1, d), ln_b.reshape(1, d))


def _route(logits_t, bias_col):
    per_group = N_EXPERTS // N_GROUPS
    scores = jax.nn.sigmoid(logits_t)
    biased = scores + bias_col
    s_rows = [scores[e:e + 1] for e in range(N_EXPERTS)]
    b_rows = [biased[e:e + 1] for e in range(N_EXPERTS)]
    best_g = None
    for g in range(N_GROUPS):
        r = b_rows[g * per_group:(g + 1) * per_group]
        gs = None
        for a in range(per_group):
            for b in range(a + 1, per_group):
                pair = r[a] + r[b]
                gs = pair if gs is None else jnp.maximum(gs, pair)
        if best_g is None:
            best_g, best_v = jnp.zeros_like(gs, dtype=jnp.int32), gs
        else:
            better = gs > best_v
            best_g = jnp.where(better, g, best_g)
            best_v = jnp.maximum(best_v, gs)
    masked = [jnp.where(best_g == e // per_group, b_rows[e], -jnp.inf) for e in range(N_EXPERTS)]

    def argmax_first(rows):
        idx, val = jnp.zeros_like(best_g), rows[0]
        for e in range(1, N_EXPERTS):
            better = rows[e] > val
            idx = jnp.where(better, e, idx)
            val = jnp.maximum(val, rows[e])
        return idx

    e1 = argmax_first(masked)
    e2 = argmax_first([jnp.where(e1 == e, -jnp.inf, masked[e]) for e in range(N_EXPERTS)])
    s1 = sum(jnp.where(e1 == e, s_rows[e], 0.0) for e in range(N_EXPERTS))
    s2 = sum(jnp.where(e2 == e, s_rows[e], 0.0) for e in range(N_EXPERTS))
    tot = s1 + s2
    w1, w2 = s1 / tot, s2 / tot
    rows = [jnp.where(e1 == e, w1, 0.0) + jnp.where(e2 == e, w2, 0.0) for e in range(N_EXPERTS)]
    return jnp.concatenate(rows, axis=0), best_g


def _moe_kernel(x_ref, ada_ref, wr_ref, br_ref, wg_ref, wu_ref, wd_ref, g_ref, b_ref, out_ref,
                perm_ref, xs_ref, combs_ref, ys_ref, sub_ref, *, tm, rows):
    g = pl.program_id(1)
    per_group = N_EXPERTS // N_GROUPS

    @pl.when(g == 0)
    def _route_and_sort():
        x = x_ref[...]
        h2 = x * (1.0 + ada_ref[0, 4:5, :]) + ada_ref[0, 3:4, :]
        h_hi = h2.astype(BF16)
        h_lo = (h2 - h_hi.astype(F32)).astype(BF16)
        first = _dot(h_hi, wr_ref[...])
        logits = first[:, 0:LANES] + first[:, LANES:] + _dot(h_lo, wr_ref[:, 0:LANES])
        logits_t = logits.T[0:N_EXPERTS]
        comb_t, best_g = _route(logits_t, br_ref[...])
        member = [jnp.where(best_g == g, 1.0, 0.0) for g in range(N_GROUPS)]
        grp = jnp.concatenate(member + [jnp.zeros((8 - N_GROUPS, tm), F32)], axis=0).astype(BF16)
        s_i = lax.broadcasted_iota(jnp.int32, (tm, tm), 0)
        t_i = lax.broadcasted_iota(jnp.int32, (tm, tm), 1)
        rank = _dot(grp, jnp.where(s_i <= t_i, 1.0, 0.0).astype(BF16))
        pos = jnp.zeros((1, tm), F32)
        start = jnp.zeros((1, 1), F32)
        for g in range(N_GROUPS):
            cap = jnp.ceil(rank[g:g + 1, tm - 1:tm] * (1.0 / MOE_SUB)) * MOE_SUB
            pos = pos + member[g] * (start + rank[g:g + 1] - 1.0)
            sub_ref[2 * g] = (jnp.sum(start) * (1.0 / MOE_SUB)).astype(jnp.int32)
            sub_ref[2 * g + 1] = (jnp.sum(cap) * (1.0 / MOE_SUB)).astype(jnp.int32)
            start = start + cap
        r_f = lax.broadcasted_iota(jnp.int32, (rows, tm), 0).astype(F32)
        perm = jnp.where(r_f == pos, 1.0, 0.0).astype(BF16)
        perm_ref[...] = perm
        comb = jnp.concatenate([comb_t, jnp.zeros((LANES - N_EXPERTS, tm), F32)], axis=0).T
        c_hi = comb.astype(BF16)
        c_lo = (comb - c_hi.astype(F32)).astype(BF16)
        d = h_hi.shape[1]
        moved = _dot(perm, jnp.concatenate([h_hi, c_hi, c_lo], axis=1))
        xs_ref[...] = moved[:, 0:d].astype(BF16)
        combs_ref[...] = moved[:, d:d + LANES] + moved[:, d + LANES:]
        ys_ref[...] = jnp.zeros(ys_ref.shape, BF16)

    lane = lax.broadcasted_iota(jnp.int32, (MOE_SUB, LANES), 1)

    def sub_tile(j, carry):
        r0 = pl.multiple_of((sub_ref[2 * g] + j) * MOE_SUB, MOE_SUB)
        xj = xs_ref[pl.ds(r0, MOE_SUB), :]
        cw = combs_ref[pl.ds(r0, MOE_SUB), :]
        hids = []
        for q in range(per_group):
            w_e = jnp.sum(jnp.where(lane == g * per_group + q, cw, 0.0), axis=1, keepdims=True)
            hids.append((jax.nn.silu(_dot(xj, wg_ref[0, q])) * _dot(xj, wu_ref[0, q]) * w_e).astype(BF16))
        ys_ref[pl.ds(r0, MOE_SUB), :] = _dot(jnp.concatenate(hids, axis=1), wd_ref[0, 0]).astype(BF16)
        return carry

    lax.fori_loop(0, sub_ref[2 * g + 1], sub_tile, 0)

    @pl.when(g == N_GROUPS - 1)
    def _unsort_and_norm():
        y = lax.dot_general(perm_ref[...], ys_ref[...], (((0,), (0,)), ((), ())),
                            preferred_element_type=F32)
        z = DN_ALPHA * x_ref[...] + ada_ref[0, 5:6, :] * y
        out_ref[...] = _layer_norm(z, g_ref[...], b_ref[...])


def _moe(x, ada_rows, w_router, b_router, w_gate, w_up, w_down, ln_g, ln_b, batch_len, tm=1024):
    n, d = x.shape
    per_b = batch_len // tm
    per_group = N_EXPERTS // N_GROUPS
    rows = tm + N_GROUPS * MOE_SUB
    wr32 = jnp.pad(w_router, ((0, 0), (0, LANES - N_EXPERTS)))
    wr_hi = wr32.astype(BF16)
    wr = jnp.concatenate([wr_hi, (wr32 - wr_hi.astype(F32)).astype(BF16)], axis=1)
    br = b_router.reshape(N_EXPERTS, 1)
    return pl.pallas_call(
        functools.partial(_moe_kernel, tm=tm, rows=rows),
        out_shape=jax.ShapeDtypeStruct((n, d), F32),
        grid=(n // tm, N_GROUPS),
        in_specs=[pl.BlockSpec((tm, d), lambda i, g: (i, 0)),
                  pl.BlockSpec((1, 6, d), lambda i, g: (i // per_b, 0, 0)),
                  pl.BlockSpec((d, 2 * LANES), lambda i, g: (0, 0)),
                  pl.BlockSpec((N_EXPERTS, 1), lambda i, g: (0, 0)),
                  pl.BlockSpec((1, per_group, d, D_EXPERT), lambda i, g: (0, g, 0, 0)),
                  pl.BlockSpec((1, per_group, d, D_EXPERT), lambda i, g: (0, g, 0, 0)),
                  pl.BlockSpec((1, 1, per_group * D_EXPERT, d), lambda i, g: (0, g, 0, 0)),
                  pl.BlockSpec((1, d), lambda i, g: (0, 0)),
                  pl.BlockSpec((1, d), lambda i, g: (0, 0))],
        out_specs=pl.BlockSpec((tm, d), lambda i, g: (i, 0)),
        scratch_shapes=[pltpu.VMEM((rows, tm), BF16),
                        pltpu.VMEM((rows, d), BF16),
                        pltpu.VMEM((rows, LANES), F32),
                        pltpu.VMEM((rows, d), BF16),
                        pltpu.SMEM((2 * N_GROUPS,), jnp.int32)],
        compiler_params=_params(2),
        name="moe",
    )(x, ada_rows, wr, br, w_gate[None].astype(BF16), w_up[None].astype(BF16),
      w_down.reshape(1, N_GROUPS, per_group * D_EXPERT, d).astype(BF16),
      ln_g.reshape(1, d), ln_b.reshape(1, d))


def kernel(x, c, w_in, b_forget, w_branch, w_merge_gate, w_out, w_ada, b_ada, ln1_g, ln1_b, ln2_g, ln2_b,
           w_router, b_router, w_exp_gate, w_exp_up, w_exp_down):
    bsz, seq, d = x.shape
    slopes = _alibi_slopes()
    ada_all = _ada(c, w_ada, b_ada).reshape(DEPTH, bsz, 6, d)
    for l in range(DEPTH):
        ada = ada_all[l]
        k, kmean, misc, qT, vT, qiT, miscT = _proj(x, ada, w_in[l])
        o_a = _dsa(qT, k, vT, qiT, misc, miscT, slopes[0])
        o_b = _dilated(qT, k, vT, slopes[1])
        o_c = _moba(qT, k, vT, kmean, slopes[2])
        o_d = _fox(qT, k, vT, misc, b_forget[l])
        x = _merge(x, ada, (o_a, o_b, o_c, o_d), w_merge_gate[l], w_branch[l], w_out[l], ln1_g[l], ln1_b[l])
        x = _moe(x.reshape(bsz * seq, d), ada, w_router, b_router, w_exp_gate[l], w_exp_up[l], w_exp_down[l],
                 ln2_g[l], ln2_b[l], seq).reshape(bsz, seq, d)
    return x
```
